```python
import jax, jax.numpy as jnp
from jax import lax
import numpy as np

D_MODEL = 1024
BATCH = 2
SEQ = 8192
DEPTH = 1
DEC_BATCH = 128
DEC_SEQ = 1
PAST_LEN = 2048
PAGE_SIZE = 128

N_HEADS = 8
KV_HEADS = 2
HEADS_PER_GROUP = N_HEADS // KV_HEADS
HEAD_DIM = 64
ROPE_DIM = HEAD_DIM // 4
ROPE_THETA = 500000.0
CMP_BLOCK = 32
CMP_STRIDE = 16
CMP_HIDDEN = 256
SEL_BLOCK = 64
N_SEL = 16
WINDOW = 512
Q_BLOCK = 128
FORCED_SCORE = 1e6
CONV_WIDTH = D_MODEL // 2
CONV_K = 3
ATTN_WIDTH = N_HEADS * HEAD_DIM
KV_WIDTH = KV_HEADS * HEAD_DIM
N_BRANCH = 2
SPLIT_SIZES = (ATTN_WIDTH, 6 * KV_WIDTH, 3 * N_HEADS, ATTN_WIDTH, 3 * CONV_WIDTH, CONV_WIDTH, N_BRANCH * D_MODEL)
IN_WIDTH = sum(SPLIT_SIZES)
RMS_EPS = 1e-6

kernel_name = "nsa_shortconv_gated_hybrid_step"


def rmsnorm(x, g):
    x32 = x.astype(jnp.float32)
    y = x32 * lax.rsqrt(jnp.mean(x32 * x32, axis=-1, keepdims=True) + RMS_EPS)
    return (y * g.astype(jnp.float32)).astype(x.dtype)


def rope(x, pos):
    half = ROPE_DIM // 2
    inv = ROPE_THETA ** (-jnp.arange(half, dtype=jnp.float32) / half)
    ang = pos.astype(jnp.float32)[:, None] * inv[None, :]
    cos = jnp.cos(ang)[None, :, None, :]
    sin = jnp.sin(ang)[None, :, None, :]
    xr = x[..., :ROPE_DIM].astype(jnp.float32)
    x1, x2 = xr[..., :half], xr[..., half:]
    rot = jnp.concatenate([x1 * cos - x2 * sin, x1 * sin + x2 * cos], axis=-1).astype(x.dtype)
    return jnp.concatenate([rot, x[..., ROPE_DIM:]], axis=-1)


def masked_softmax(s, mask):
    s = jnp.where(mask, s, -jnp.inf)
    m = jnp.max(s, axis=-1, keepdims=True)
    m = jnp.where(jnp.isfinite(m), m, 0.0)
    e = jnp.exp(s - m)
    return e / jnp.maximum(jnp.sum(e, axis=-1, keepdims=True), 1e-30)


def compress(k, pe, w1, w2):
    B, T, G, hd = k.shape
    r = CMP_BLOCK // CMP_STRIDE
    n_chunk = T // CMP_STRIDE
    n_cmp = n_chunk - r + 1
    kc = k[:, :n_chunk * CMP_STRIDE].reshape(B, n_chunk, CMP_STRIDE, G, hd)
    blocks = jnp.concatenate([kc[:, i:i + n_cmp] for i in range(r)], axis=2)
    blocks = blocks + pe[None, None, :, None, :].astype(k.dtype)
    flat = jnp.transpose(blocks, (0, 1, 3, 2, 4)).reshape(B, n_cmp, G, CMP_BLOCK * hd)
    return jax.nn.silu(flat @ w1) @ w2


def cmp_sel_overlap(n_cmp, n_selb):
    cs = jnp.arange(n_cmp) * CMP_STRIDE
    ss = jnp.arange(n_selb) * SEL_BLOCK
    ov = (cs[:, None] < ss[None, :] + SEL_BLOCK) & (cs[:, None] + CMP_BLOCK > ss[None, :])
    return ov.astype(jnp.float32)


def sel_blocks(k):
    B, T, G, hd = k.shape
    n_selb = -(-T // SEL_BLOCK)
    k = jnp.pad(k, ((0, 0), (0, n_selb * SEL_BLOCK - T), (0, 0), (0, 0)))
    return jnp.transpose(k.reshape(B, n_selb, SEL_BLOCK, G, hd), (0, 3, 1, 2, 4))


def gather_blocks(blocks, idx):
    return jax.vmap(jax.vmap(lambda blk, ix: blk[ix]))(blocks, idx)


def nsa_attend(q, gates, qpos, kc, vc, ksb, vsb, ovl, kw, vw, kwpos):
    f32 = jnp.float32
    scale = HEAD_DIM ** -0.5
    n_cmp = kc.shape[1]
    s = jnp.einsum('bqghd,bcgd->bghqc', q, kc, preferred_element_type=f32) * scale
    c_end = jnp.arange(n_cmp) * CMP_STRIDE + (CMP_BLOCK - 1)
    p_cmp = masked_softmax(s, c_end[None, :] <= qpos[:, None])
    o_cmp = jnp.einsum('bghqc,bcgd->bqghd', p_cmp.astype(vc.dtype), vc)
    n_selb = ksb.shape[2]
    imp = jnp.einsum('bghqc,cj->bgqj', p_cmp, ovl)
    j = jnp.arange(n_selb)[None, :]
    cur = (qpos // SEL_BLOCK)[:, None]
    forced = (j == 0) | (j == cur) | (j == cur - 1)
    valid = j <= cur
    score = jnp.where(forced, FORCED_SCORE, jnp.where(valid, imp, -1.0))
    _, idx = lax.top_k(score, min(N_SEL, n_selb))
    B, G, Tq, kk = idx.shape
    kg = gather_blocks(ksb, idx).reshape(B, G, Tq, kk * SEL_BLOCK, HEAD_DIM)
    vg = gather_blocks(vsb, idx).reshape(B, G, Tq, kk * SEL_BLOCK, HEAD_DIM)
    kpos = (idx[..., None] * SEL_BLOCK + jnp.arange(SEL_BLOCK)).reshape(B, G, Tq, kk * SEL_BLOCK)
    s = jnp.einsum('bqghd,bgqsd->bghqs', q, kg, preferred_element_type=f32) * scale
    p = masked_softmax(s, (kpos <= qpos[:, None])[:, :, None])
    o_sel = jnp.einsum('bghqs,bgqsd->bqghd', p.astype(vg.dtype), vg)
    s = jnp.einsum('bqghd,bsgd->bghqs', q, kw, preferred_element_type=f32) * scale
    dt = qpos[:, None] - kwpos[None, :]
    p = masked_softmax(s, (dt >= 0) & (dt < WINDOW) & (kwpos >= 0)[None, :])
    o_win = jnp.einsum('bghqs,bsgd->bqghd', p.astype(vw.dtype), vw)
    return gates[..., 0:1] * o_cmp + gates[..., 1:2] * o_sel + gates[..., 2:3] * o_win


def project(x, c, pos, p):
    B, T, _ = x.shape
    mod = jax.nn.silu(c) @ p['w_ada'] + p['b_ada']
    shift, scale, gate = jnp.split(mod[:, None, :], 3, axis=-1)
    h = rmsnorm(x, p['g_pre']) * (1 + scale) + shift
    z = h @ p['w_in']
    zq, zkv, zg, za, zconv, zcg, zm = jnp.split(z, np.cumsum(SPLIT_SIZES)[:-1].tolist(), axis=-1)
    q = rope(zq.reshape(B, T, N_HEADS, HEAD_DIM), pos).reshape(B, T, KV_HEADS, HEADS_PER_GROUP, HEAD_DIM)
    kv = zkv.reshape(B, T, 6, KV_HEADS, HEAD_DIM)
    kv = jnp.stack([rope(kv[:, :, i], pos) if i % 2 == 0 else kv[:, :, i] for i in range(6)], axis=2)
    nsa_g = jax.nn.sigmoid(zg).reshape(B, T, KV_HEADS, HEADS_PER_GROUP, 3)
    cb, cc, cx = jnp.split(zconv, 3, axis=-1)
    return dict(q=q, kv=kv, nsa_g=nsa_g, a_gate=za, cb=cb, u=cc * cx, cgate=zcg, merge=zm, gate=gate)


def causal_conv(up, w):
    T = up.shape[1] - (CONV_K - 1)
    out = w[0] * up[:, 0:T]
    for j in range(1, CONV_K):
        out = out + w[j] * up[:, j:j + T]
    return out


def finish(x, o_attn, conv_out, pr, p):
    ya = (o_attn * jax.nn.silu(pr['a_gate'])) @ p['w_br_a']
    yb = (pr['cb'] * conv_out * jax.nn.silu(pr['cgate'])) @ p['w_br_b']
    ga, gb = jnp.split(jax.nn.sigmoid(pr['merge']), 2, axis=-1)
    o = (ga * ya + gb * yb) @ p['w_out']
    return x + pr['gate'] * rmsnorm(o, p['g_post'])


def prompt_layer(x, c, p):
    B, T, _ = x.shape
    pr = project(x, c, jnp.arange(T), p)
    kv = pr['kv']
    kc = compress(kv[:, :, 0], p['pe_cmp'][0], p['w_cmp1'][0], p['w_cmp2'][0])
    vc = compress(kv[:, :, 1], p['pe_cmp'][1], p['w_cmp1'][1], p['w_cmp2'][1])
    ksb = sel_blocks(kv[:, :, 2])
    vsb = sel_blocks(kv[:, :, 3])
    ovl = cmp_sel_overlap(kc.shape[1], ksb.shape[2])
    kwp = jnp.pad(kv[:, :, 4:6], ((0, 0), (WINDOW, 0), (0, 0), (0, 0), (0, 0)))
    n_qb = T // Q_BLOCK
    qb = jnp.moveaxis(pr['q'].reshape(B, n_qb, Q_BLOCK, KV_HEADS, HEADS_PER_GROUP, HEAD_DIM), 1, 0)
    gb = jnp.moveaxis(pr['nsa_g'].reshape(B, n_qb, Q_BLOCK, KV_HEADS, HEADS_PER_GROUP, 3), 1, 0)

    def body(args):
        q_blk, g_blk, b = args
        qpos = b * Q_BLOCK + jnp.arange(Q_BLOCK)
        kw = lax.dynamic_slice_in_dim(kwp, b * Q_BLOCK, WINDOW + Q_BLOCK, axis=1)
        kwpos = b * Q_BLOCK - WINDOW + jnp.arange(WINDOW + Q_BLOCK)
        return nsa_attend(q_blk, g_blk, qpos, kc, vc, ksb, vsb, ovl, kw[:, :, 0], kw[:, :, 1], kwpos)

    o = lax.map(body, (qb, gb, jnp.arange(n_qb)))
    o = jnp.moveaxis(o, 0, 1).reshape(B, T, ATTN_WIDTH)
    up = jnp.pad(pr['u'], ((0, 0), (CONV_K - 1, 0), (0, 0)))
    y = finish(x, o, causal_conv(up, p['conv_w']), pr, p)
    return y, kv[:, :, :4], kv[:, T - min(WINDOW, T):, 4:6], up[:, -(CONV_K - 1):]


def sample_layer(x, c, cache_pages, page_table, win_buf, conv_buf, p):
    B, T, _ = x.shape
    past_len = page_table.shape[1] * cache_pages.shape[1]
    pos = past_len + jnp.arange(T)
    pr = project(x, c, pos, p)
    kv = pr['kv']
    past = cache_pages[page_table].reshape(B, past_len, 4, KV_HEADS, HEAD_DIM)
    full = jnp.concatenate([past, kv[:, :, :4]], axis=1)
    kc = compress(full[:, :, 0], p['pe_cmp'][0], p['w_cmp1'][0], p['w_cmp2'][0])
    vc = compress(full[:, :, 1], p['pe_cmp'][1], p['w_cmp1'][1], p['w_cmp2'][1])
    ksb = sel_blocks(full[:, :, 2])
    vsb = sel_blocks(full[:, :, 3])
    ovl = cmp_sel_overlap(kc.shape[1], ksb.shape[2])
    wbuf = win_buf.shape[1]
    wk = jnp.concatenate([win_buf, kv[:, :, 4:6]], axis=1)
    kwpos = past_len - wbuf + jnp.arange(wbuf + T)
    o = nsa_attend(pr['q'], pr['nsa_g'], pos, kc, vc, ksb, vsb, ovl, wk[:, :, 0], wk[:, :, 1], kwpos)
    o = o.reshape(B, T, ATTN_WIDTH)
    up = jnp.concatenate([conv_buf, pr['u']], axis=1)
    y = finish(x, o, causal_conv(up, p['conv_w']), pr, p)
    n_keep = min(WINDOW, wbuf + T)
    return y, kv[:, :, :4], wk[:, wbuf + T - n_keep:], up[:, -(CONV_K - 1):]


def setup_inputs(seed: int = 0) -> dict:
    key = jax.random.key(seed)
    ks = jax.random.split(key, 24)
    n_pages = PAST_LEN // PAGE_SIZE
    n_phys = (5 * DEC_BATCH * n_pages) // 4
    wbuf = min(WINDOW, PAST_LEN)
    f32 = jnp.float32

    def nrm(k, shape, s):
        return jax.random.normal(k, shape, f32) * s

    page_table = jax.random.permutation(ks[0], n_phys)[:DEC_BATCH * n_pages].reshape(DEC_BATCH, n_pages).astype(jnp.int32)
    return {
        "x_prompt": nrm(ks[1], (BATCH, SEQ, D_MODEL), 1.0),
        "x_sample": nrm(ks[2], (DEC_BATCH, DEC_SEQ, D_MODEL), 1.0),
        "cache_kv_pages": nrm(ks[3], (DEPTH, n_phys, PAGE_SIZE, 4, KV_HEADS, HEAD_DIM), 1.0),
        "state_win_kv": nrm(ks[4], (DEPTH, DEC_BATCH, wbuf, 2, KV_HEADS, HEAD_DIM), 1.0),
        "state_conv": nrm(ks[5], (DEPTH, DEC_BATCH, CONV_K - 1, CONV_WIDTH), 1.0),
        "page_table": page_table,
        "c_prompt": nrm(ks[6], (BATCH, D_MODEL), 1.0),
        "c_sample": nrm(ks[7], (DEC_BATCH, D_MODEL), 1.0),
        "w_ada": nrm(ks[8], (DEPTH, D_MODEL, 3 * D_MODEL), 0.5 * D_MODEL ** -0.5),
        "b_ada": nrm(ks[9], (DEPTH, 3 * D_MODEL), 0.01),
        "g_pre": 1.0 + nrm(ks[10], (DEPTH, D_MODEL), 0.02),
        "g_post": 1.0 + nrm(ks[11], (DEPTH, D_MODEL), 0.02),
        "w_in": nrm(ks[12], (DEPTH, D_MODEL, IN_WIDTH), D_MODEL ** -0.5),
        "pe_cmp": nrm(ks[13], (DEPTH, 2, CMP_BLOCK, HEAD_DIM), 0.02),
        "w_cmp1": nrm(ks[14], (DEPTH, 2, CMP_BLOCK * HEAD_DIM, CMP_HIDDEN), (CMP_BLOCK * HEAD_DIM) ** -0.5),
        "w_cmp2": nrm(ks[15], (DEPTH, 2, CMP_HIDDEN, HEAD_DIM), CMP_HIDDEN ** -0.5),
        "conv_w": nrm(ks[16], (DEPTH, CONV_K, CONV_WIDTH), CONV_K ** -0.5),
        "w_br_a": nrm(ks[17], (DEPTH, ATTN_WIDTH, D_MODEL), ATTN_WIDTH ** -0.5),
        "w_br_b": nrm(ks[18], (DEPTH, CONV_WIDTH, D_MODEL), CONV_WIDTH ** -0.5),
        "w_out": nrm(ks[19], (DEPTH, D_MODEL, D_MODEL), D_MODEL ** -0.5),
    }


def reference(x_prompt, x_sample, cache_kv_pages, state_win_kv, state_conv, page_table, c_prompt, c_sample,
              w_ada, b_ada, g_pre, g_post, w_in, pe_cmp, w_cmp1, w_cmp2, conv_w, w_br_a, w_br_b, w_out):
    hp, hs = x_prompt, x_sample
    kvp_l, wp_l, cp_l, kvs_l, ws_l, cs_l = [], [], [], [], [], []
    for l in range(DEPTH):
        p = dict(w_ada=w_ada[l], b_ada=b_ada[l], g_pre=g_pre[l], g_post=g_post[l], w_in=w_in[l],
                 pe_cmp=pe_cmp[l], w_cmp1=w_cmp1[l], w_cmp2=w_cmp2[l], conv_w=conv_w[l],
                 w_br_a=w_br_a[l], w_br_b=w_br_b[l], w_out=w_out[l])
        hp, kvp, wp, cp = prompt_layer(hp, c_prompt, p)
        hs, kvs, ws, cs = sample_layer(hs, c_sample, cache_kv_pages[l], page_table, state_win_kv[l], state_conv[l], p)
        kvp_l.append(kvp); wp_l.append(wp); cp_l.append(cp)
        kvs_l.append(kvs); ws_l.append(ws); cs_l.append(cs)
    kv_rows_prompt = jnp.stack(kvp_l)
    win_kv_prompt = jnp.stack(wp_l)
    conv_state_prompt = jnp.stack(cp_l)
    kv_rows_sample = jnp.stack(kvs_l)
    win_kv_sample = jnp.stack(ws_l)
    conv_state_sample = jnp.stack(cs_l)
    return (hp, hs, kv_rows_prompt, win_kv_prompt, conv_state_prompt, kv_rows_sample, win_kv_sample, conv_state_sample)
```

```python
import functools

import jax
import jax.numpy as jnp
from jax import lax
from jax.experimental import pallas as pl
from jax.experimental.pallas import tpu as pltpu

F32 = jnp.float32
BF16 = jnp.bfloat16

D_MODEL = 1024
N_HEADS = 8
KV_HEADS = 2
HEADS_PER_GROUP = N_HEADS // KV_HEADS
HEAD_DIM = 64
ROPE_DIM = HEAD_DIM // 4
ROPE_THETA = 500000.0
CMP_BLOCK = 32
CMP_STRIDE = 16
CMP_HIDDEN = 256
SEL_BLOCK = 64
N_SEL = 16
WINDOW = 512
Q_BLOCK = 128
FORCED_SCORE = 1e6
CONV_WIDTH = D_MODEL // 2
CONV_K = 3
ATTN_WIDTH = N_HEADS * HEAD_DIM
KV_WIDTH = KV_HEADS * HEAD_DIM
RMS_EPS = 1e-6

LANES = 128
SUBLANES = 8
VMEM_LIMIT = 56 * 1024 * 1024

C_Q = 0
C_KV = C_Q + ATTN_WIDTH
C_G = C_KV + 6 * KV_WIDTH
C_A = C_G + LANES
C_CB = C_A + ATTN_WIDTH
C_CC = C_CB + CONV_WIDTH
C_CX = C_CC + CONV_WIDTH
C_CG = C_CX + CONV_WIDTH
C_MA = C_CG + CONV_WIDTH
C_MB = C_MA + D_MODEL
IN_PAD = C_MB + D_MODEL

NEG_BIG = -1e30
KV_TILE = 512
ROW_CHUNK = 256


def _sigmoid(x):
    return 1.0 / (1.0 + jnp.exp(-x))


def _silu(x):
    return x * _sigmoid(x)


def _dot(a, b):
    return jnp.dot(a, b, preferred_element_type=F32)


def _dot_nt(a, b):
    return lax.dot_general(a, b, (((1,), (1,)), ((), ())), preferred_element_type=F32)


def _rope_tables(pos):
    half = ROPE_DIM // 2
    inv = ROPE_THETA ** (-jnp.arange(half, dtype=F32) / half)
    ang = pos.astype(F32)[:, None] * inv[None, :]
    cos, sin = jnp.cos(ang), jnp.sin(ang)
    n = pos.shape[0]
    a = jnp.concatenate([cos, cos, jnp.ones((n, HEAD_DIM - ROPE_DIM), F32)], axis=1)
    p = jnp.concatenate([jnp.zeros((n, half), F32), sin, jnp.zeros((n, HEAD_DIM - ROPE_DIM), F32)], axis=1)
    m = jnp.concatenate([-sin, jnp.zeros((n, HEAD_DIM - half), F32)], axis=1)
    tile = lambda t: jnp.concatenate([t, t], axis=1)
    return tile(a), tile(p), tile(m)


def _rope(x, ra, rp, rm):
    half = ROPE_DIM // 2
    return x * ra + pltpu.roll(x, half, 1) * rp + pltpu.roll(x, LANES - half, 1) * rm


def _ada_kernel(c_ref, w_ref, b_ref, o_ref):
    c = _silu(c_ref[...]).astype(BF16)
    o_ref[...] = _dot(c, w_ref[...].astype(BF16)) + b_ref[...]


def _ada(c_all, w_ada, b_ada):
    n = c_all.shape[0]
    tn = 512
    return pl.pallas_call(
        _ada_kernel,
        grid=(3 * D_MODEL // tn,),
        in_specs=[
            pl.BlockSpec((n, D_MODEL), lambda j: (0, 0)),
            pl.BlockSpec((D_MODEL, tn), lambda j: (0, j)),
            pl.BlockSpec((1, tn), lambda j: (0, j)),
        ],
        out_specs=pl.BlockSpec((n, tn), lambda j: (0, j)),
        out_shape=jax.ShapeDtypeStruct((n, 3 * D_MODEL), F32),
        compiler_params=pltpu.CompilerParams(dimension_semantics=("arbitrary",), vmem_limit_bytes=VMEM_LIMIT),
        name="ada",
    )(c_all, w_ada, b_ada.reshape(1, -1))


def _proj_common(x_ref, shift_ref, scale_ref, gpre_ref, w_ref, ra_ref, rp_ref, rm_ref):
    x = x_ref[...]
    ms = jnp.mean(x * x, axis=-1, keepdims=True)
    xn = x * lax.rsqrt(ms + RMS_EPS) * gpre_ref[...]
    h = xn * (1.0 + scale_ref[...]) + shift_ref[...]
    hb = h.astype(BF16)
    ra, rp, rm = ra_ref[...], rp_ref[...], rm_ref[...]

    def seg(lo, hi):
        return _dot(hb, w_ref[:, lo:hi])

    return seg, (ra, rp, rm)


def _emit_q(seg, rope, qpad_ref):
    zq = seg(C_Q, C_Q + ATTN_WIDTH)
    lane = lax.broadcasted_iota(jnp.int32, (zq.shape[0], LANES), 1)
    lower = lane < HEAD_DIM
    for j in range(ATTN_WIDTH // LANES):
        c = _rope(zq[:, j * LANES:(j + 1) * LANES], *rope) * (HEAD_DIM ** -0.5)
        r = pltpu.roll(c, HEAD_DIM, 1)
        if (2 * j) // HEADS_PER_GROUP == 0:
            even, odd = jnp.where(lower, c, 0.0), jnp.where(lower, r, 0.0)
        else:
            even, odd = jnp.where(lower, 0.0, r), jnp.where(lower, 0.0, c)
        qpad_ref[:, (2 * j) * LANES:(2 * j + 1) * LANES] = even.astype(BF16)
        qpad_ref[:, (2 * j + 1) * LANES:(2 * j + 2) * LANES] = odd.astype(BF16)


def _kv_pieces(seg, rope):
    zkv = seg(C_KV, C_KV + 6 * KV_WIDTH)
    pieces = []
    for p in range(6):
        c = zkv[:, p * LANES:(p + 1) * LANES]
        pieces.append(_rope(c, *rope) if p % 2 == 0 else c)
    return pieces


def _branch_b(seg, um2, um1, u, convw_ref, wbrb_ref):
    cb = seg(C_CB, C_CB + CONV_WIDTH)
    conv = convw_ref[0:1, :] * um2
    conv = conv + convw_ref[1:2, :] * um1
    conv = conv + convw_ref[2:3, :] * u
    ybin = cb * conv * _silu(seg(C_CG, C_CG + CONV_WIDTH))
    yb = _dot(ybin.astype(BF16), wbrb_ref[...])
    gb = _sigmoid(seg(C_MB, C_MB + D_MODEL))
    return gb * yb


def _proj_prompt_kernel(x_ref, shift_ref, scale_ref, gpre_ref, w_ref, ra_ref, rp_ref, rm_ref, convw_ref, wbrb_ref,
                        qpad_ref, kvrows_ref, kwin_ref, ksel_ref, kwinb_ref, gates_ref, sa_ref, ga_ref, pb_ref,
                        utail_ref, carry_ref):
    ti = pl.program_id(1)
    tm = x_ref.shape[0]
    seg, rope = _proj_common(x_ref, shift_ref, scale_ref, gpre_ref, w_ref, ra_ref, rp_ref, rm_ref)
    _emit_q(seg, rope, qpad_ref)

    pieces = _kv_pieces(seg, rope)
    for p in range(4):
        kvrows_ref[:, p * LANES:(p + 1) * LANES] = pieces[p]
    kwin_ref[:, 0:LANES] = pieces[4]
    kwin_ref[:, LANES:2 * LANES] = pieces[5]
    kwinb_ref[:, 0:LANES] = pieces[4].astype(BF16)
    kwinb_ref[:, LANES:2 * LANES] = pieces[5].astype(BF16)
    row = ti * tm + lax.broadcasted_iota(jnp.int32, (tm, LANES), 0)
    lane = lax.broadcasted_iota(jnp.int32, (tm, LANES), 1)
    onehot = jnp.where(lane == row // SEL_BLOCK, 1.0, 0.0)
    ksel_ref[:, 0:LANES] = pieces[2].astype(BF16)
    ksel_ref[:, LANES:2 * LANES] = onehot.astype(BF16)
    ksel_ref[:, 2 * LANES:3 * LANES] = pieces[3].astype(BF16)

    gates_ref[...] = _sigmoid(seg(C_G, C_G + LANES))
    sa_ref[...] = _silu(seg(C_A, C_A + ATTN_WIDTH))
    ga_ref[...] = _sigmoid(seg(C_MA, C_MA + D_MODEL))

    @pl.when(ti == 0)
    def _():
        carry_ref[...] = jnp.zeros_like(carry_ref)

    u = seg(C_CC, C_CC + CONV_WIDTH) * seg(C_CX, C_CX + CONV_WIDTH)
    r = lax.broadcasted_iota(jnp.int32, u.shape, 0)
    c7 = carry_ref[SUBLANES - 1:SUBLANES, :]
    c6 = carry_ref[SUBLANES - 2:SUBLANES - 1, :]
    um1 = jnp.where(r == 0, c7, pltpu.roll(u, 1, 0))
    um2 = jnp.where(r == 0, c6, jnp.where(r == 1, c7, pltpu.roll(u, 2, 0)))
    pb_ref[...] = _branch_b(seg, um2, um1, u, convw_ref, wbrb_ref)
    tail = u[tm - SUBLANES:tm, :]
    carry_ref[...] = tail
    utail_ref[0] = tail


def _proj_prompt(x2d, shift, scale, g_pre, w_pad, rope_tabs, conv_w, w_br_b, batch, seq, tm):
    n = batch * seq
    nt = seq // tm
    row = lambda w: pl.BlockSpec((tm, w), lambda b, t: (b * nt + t, 0))
    per_b = lambda w: pl.BlockSpec((None, 1, w), lambda b, t: (b, 0, 0))
    const = lambda shp: pl.BlockSpec(shp, lambda b, t: (0,) * len(shp))
    tab = pl.BlockSpec((tm, LANES), lambda b, t: (t, 0))
    out_shapes = (
        jax.ShapeDtypeStruct((n, N_HEADS * LANES), BF16),
        jax.ShapeDtypeStruct((n, 4 * KV_WIDTH), F32),
        jax.ShapeDtypeStruct((n, 2 * KV_WIDTH), F32),
        jax.ShapeDtypeStruct((n, 3 * LANES), BF16),
        jax.ShapeDtypeStruct((n, 2 * KV_WIDTH), BF16),
        jax.ShapeDtypeStruct((n, LANES), F32),
        jax.ShapeDtypeStruct((n, ATTN_WIDTH), F32),
        jax.ShapeDtypeStruct((n, D_MODEL), F32),
        jax.ShapeDtypeStruct((n, D_MODEL), F32),
        jax.ShapeDtypeStruct((batch, SUBLANES, CONV_WIDTH), F32),
    )
    out_specs = (
        row(N_HEADS * LANES), row(4 * KV_WIDTH), row(2 * KV_WIDTH), row(3 * LANES), row(2 * KV_WIDTH),
        row(LANES), row(ATTN_WIDTH), row(D_MODEL), row(D_MODEL),
        pl.BlockSpec((1, SUBLANES, CONV_WIDTH), lambda b, t: (b, 0, 0)),
    )
    return pl.pallas_call(
        _proj_prompt_kernel,
        grid=(batch, nt),
        in_specs=[row(D_MODEL), per_b(D_MODEL), per_b(D_MODEL), const((1, D_MODEL)), const((D_MODEL, IN_PAD)),
                  tab, tab, tab, const((CONV_K, CONV_WIDTH)), const((CONV_WIDTH, D_MODEL))],
        out_specs=out_specs,
        out_shape=out_shapes,
        scratch_shapes=[pltpu.VMEM((SUBLANES, CONV_WIDTH), F32)],
        compiler_params=pltpu.CompilerParams(dimension_semantics=("arbitrary", "arbitrary"),
                                             vmem_limit_bytes=VMEM_LIMIT),
        name="proj_prompt",
    )(x2d, shift[:, None, :], scale[:, None, :], g_pre, w_pad, *rope_tabs, conv_w, w_br_b)


def _proj_sample_kernel(x_ref, shift_ref, scale_ref, gpre_ref, w_ref, ra_ref, rp_ref, rm_ref, convw_ref, wbrb_ref,
                        cbuf_ref, qpad_ref, kvnew_ref, gates_ref, sa_ref, ga_ref, pb_ref, u_ref):
    seg, rope = _proj_common(x_ref, shift_ref, scale_ref, gpre_ref, w_ref, ra_ref, rp_ref, rm_ref)
    _emit_q(seg, rope, qpad_ref)
    pieces = _kv_pieces(seg, rope)
    for p in range(6):
        kvnew_ref[:, p * LANES:(p + 1) * LANES] = pieces[p]
    gates_ref[...] = _sigmoid(seg(C_G, C_G + LANES))
    sa_ref[...] = _silu(seg(C_A, C_A + ATTN_WIDTH))
    ga_ref[...] = _sigmoid(seg(C_MA, C_MA + D_MODEL))
    u = seg(C_CC, C_CC + CONV_WIDTH) * seg(C_CX, C_CX + CONV_WIDTH)
    um2 = cbuf_ref[:, 0:CONV_WIDTH]
    um1 = cbuf_ref[:, CONV_WIDTH:2 * CONV_WIDTH]
    pb_ref[...] = _branch_b(seg, um2, um1, u, convw_ref, wbrb_ref)
    u_ref[...] = u


def _proj_sample(x2d, shift, scale, g_pre, w_pad, rope_tabs, conv_w, w_br_b, cbuf):
    n = x2d.shape[0]
    full = lambda shp: pl.BlockSpec(shp, lambda i: (0,) * len(shp))
    out_shapes = (
        jax.ShapeDtypeStruct((n, N_HEADS * LANES), BF16),
        jax.ShapeDtypeStruct((n, 6 * KV_WIDTH), F32),
        jax.ShapeDtypeStruct((n, LANES), F32),
        jax.ShapeDtypeStruct((n, ATTN_WIDTH), F32),
        jax.ShapeDtypeStruct((n, D_MODEL), F32),
        jax.ShapeDtypeStruct((n, D_MODEL), F32),
        jax.ShapeDtypeStruct((n, CONV_WIDTH), F32),
    )
    return pl.pallas_call(
        _proj_sample_kernel,
        grid=(1,),
        in_specs=[full((n, D_MODEL)), full((n, D_MODEL)), full((n, D_MODEL)), full((1, D_MODEL)),
                  full((D_MODEL, IN_PAD)), full((1, LANES)), full((1, LANES)), full((1, LANES)),
                  full((CONV_K, CONV_WIDTH)), full((CONV_WIDTH, D_MODEL)), full((n, 2 * CONV_WIDTH))],
        out_specs=tuple(full(s.shape) for s in out_shapes),
        out_shape=out_shapes,
        compiler_params=pltpu.CompilerParams(dimension_semantics=("arbitrary",), vmem_limit_bytes=VMEM_LIMIT),
        name="proj_sample",
    )(x2d, shift, scale, g_pre, w_pad, *rope_tabs, conv_w, w_br_b, cbuf)


CHUNK_LANES = CMP_STRIDE * KV_WIDTH


def _compress_weights(pe_cmp, w_cmp1, w_cmp2):
    zeros = jnp.zeros((2, CMP_STRIDE, HEAD_DIM, CMP_HIDDEN), w_cmp1.dtype)

    def both_groups(w_half):
        w = w_half.reshape(2, CMP_STRIDE, HEAD_DIM, CMP_HIDDEN)
        g0 = jnp.concatenate([w, zeros], axis=2)
        g1 = jnp.concatenate([zeros, w], axis=2)
        return jnp.concatenate([g0, g1], axis=3).reshape(2, CHUNK_LANES, KV_HEADS * CMP_HIDDEN).astype(BF16)

    half = CMP_STRIDE * HEAD_DIM
    z2 = jnp.zeros_like(w_cmp2)
    w2 = jnp.concatenate([jnp.concatenate([w_cmp2, z2], axis=2), jnp.concatenate([z2, w_cmp2], axis=2)], axis=1)
    pe = jnp.concatenate([pe_cmp, pe_cmp], axis=2)
    return dict(w1_lo=both_groups(w_cmp1[:, :half]), w1_hi=both_groups(w_cmp1[:, half:]), w2=w2.astype(BF16),
                pe_lo=pe[:, :CMP_STRIDE].reshape(2, 1, CHUNK_LANES), pe_hi=pe[:, CMP_STRIDE:].reshape(2, 1, CHUNK_LANES))


def _compress_chunks(c, kind, pelo_ref, pehi_ref, w1lo_ref, w1hi_ref):
    lo = _dot((c + pelo_ref[kind]).astype(BF16), w1lo_ref[kind])
    hi = _dot((c + pehi_ref[kind]).astype(BF16), w1hi_ref[kind])
    return lo, hi


def _compress_prompt_kernel(kc_ref, vc_ref, pelo_ref, pehi_ref, w1lo_ref, w1hi_ref, w2_ref, o_ref):
    n_chunk = kc_ref.shape[0] // CMP_STRIDE
    for kind, src_ref in enumerate((kc_ref, vc_ref)):
        c = jnp.concatenate([src_ref[pl.ds(r, n_chunk, stride=CMP_STRIDE), :] for r in range(CMP_STRIDE)], axis=1)
        lo, hi = _compress_chunks(c, kind, pelo_ref, pehi_ref, w1lo_ref, w1hi_ref)
        hid = lo + pltpu.roll(hi, n_chunk - 1, 0)
        out = _dot(_silu(hid).astype(BF16), w2_ref[kind])
        o_ref[0, :, kind * KV_WIDTH:(kind + 1) * KV_WIDTH] = out.astype(BF16)


def _compress_prompt(kvrows, cw, batch, seq):
    n_chunk = seq // CMP_STRIDE
    const = lambda a: pl.BlockSpec(a.shape, lambda b: (0,) * a.ndim)
    return pl.pallas_call(
        _compress_prompt_kernel,
        grid=(batch,),
        in_specs=[
            pl.BlockSpec((seq, KV_WIDTH), lambda b: (b, 0)),
            pl.BlockSpec((seq, KV_WIDTH), lambda b: (b, 1)),
            const(cw["pe_lo"]), const(cw["pe_hi"]), const(cw["w1_lo"]), const(cw["w1_hi"]), const(cw["w2"]),
        ],
        out_specs=pl.BlockSpec((1, n_chunk, 2 * KV_WIDTH), lambda b: (b, 0, 0)),
        out_shape=jax.ShapeDtypeStruct((batch, n_chunk, 2 * KV_WIDTH), BF16),
        compiler_params=pltpu.CompilerParams(dimension_semantics=("arbitrary",), vmem_limit_bytes=VMEM_LIMIT),
        name="compress_prompt",
    )(kvrows, kvrows, cw["pe_lo"], cw["pe_hi"], cw["w1_lo"], cw["w1_hi"], cw["w2"])


def _top_k_bias(score):
    lane = lax.broadcasted_iota(jnp.int32, score.shape, 1).astype(F32)
    bias = jnp.full(score.shape, NEG_BIG, F32)
    for _ in range(N_SEL):
        m = jnp.max(score, axis=1, keepdims=True)
        idx = jnp.min(jnp.where(score == m, lane, float(LANES)), axis=1, keepdims=True)
        hit = lane == idx
        bias = jnp.where(hit, 0.0, bias)
        score = jnp.where(hit, -jnp.inf, score)
    return bias


def _split_bf16(x):
    hi = x.astype(BF16)
    lo = (x - hi.astype(F32)).astype(BF16)
    return hi, lo


def _masked_softmax_parts(s, valid):
    s = jnp.where(valid, s, -jnp.inf)
    m = jnp.max(s, axis=-1, keepdims=True)
    m = jnp.where(m == -jnp.inf, 0.0, m)
    e = jnp.exp(s - m)
    den = jnp.maximum(jnp.sum(e, axis=-1, keepdims=True), 1e-30)
    return e, den


def _attn_prompt_kernel(qpad_ref, kaug_ref, vsel_ref, kwin_ref, vwin_ref, kvcmp_ref, gates_ref, ovl_ref,
                        o_ref, qaug_ref, m_ref, l_ref, acc_ref, oc_ref, ow_ref):
    qb = pl.program_id(1)
    q0 = qb * Q_BLOCK
    seq = kaug_ref.shape[0]
    n_cmp = kvcmp_ref.shape[1]
    n_rows = N_HEADS * Q_BLOCK
    n_chunks = n_rows // ROW_CHUNK
    heads_per_chunk = ROW_CHUNK // Q_BLOCK

    for n in range(N_HEADS):
        qaug_ref[n * Q_BLOCK:(n + 1) * Q_BLOCK, 0:LANES] = qpad_ref[:, n * LANES:(n + 1) * LANES]

    def qpos_rows(rows):
        r = lax.broadcasted_iota(jnp.int32, (rows, 1), 0)
        return q0 + (r & (Q_BLOCK - 1))

    kc = kvcmp_ref[0, :, 0:LANES]
    vc = kvcmp_ref[0, :, LANES:2 * LANES]
    c_end = lax.broadcasted_iota(jnp.int32, (1, n_cmp), 1) * CMP_STRIDE + (CMP_BLOCK - 1)
    psum = [None] * KV_HEADS
    for rc in range(n_chunks):
        rows = slice(rc * ROW_CHUNK, (rc + 1) * ROW_CHUNK)
        s = _dot_nt(qaug_ref[rows, 0:LANES], kc)
        e, den = _masked_softmax_parts(s, c_end <= qpos_rows(ROW_CHUNK))
        p = e * (1.0 / den)
        oc_ref[rows, :] = _dot(p.astype(BF16), vc)
        g = (rc * heads_per_chunk) // HEADS_PER_GROUP
        part = p[0:Q_BLOCK]
        for h in range(1, heads_per_chunk):
            part = part + p[h * Q_BLOCK:(h + 1) * Q_BLOCK]
        psum[g] = part if psum[g] is None else psum[g] + part

    qpos = q0 + lax.broadcasted_iota(jnp.int32, (Q_BLOCK, 1), 0)
    cur = qpos // SEL_BLOCK
    j = lax.broadcasted_iota(jnp.int32, (Q_BLOCK, LANES), 1)
    forced = (j == 0) | (j == cur) | (j == cur - 1)
    for g in range(KV_HEADS):
        hi, lo = _split_bf16(psum[g])
        imp = _dot(hi, ovl_ref[...]) + _dot(lo, ovl_ref[...])
        score = jnp.where(forced, FORCED_SCORE, jnp.where(j <= cur, imp, -1.0))
        bias = _top_k_bias(score).astype(BF16)
        for h in range(HEADS_PER_GROUP):
            n = g * HEADS_PER_GROUP + h
            qaug_ref[n * Q_BLOCK:(n + 1) * Q_BLOCK, LANES:2 * LANES] = bias

    m_ref[...] = jnp.full(m_ref.shape, NEG_BIG, F32)
    l_ref[...] = jnp.zeros(l_ref.shape, F32)
    acc_ref[...] = jnp.zeros(acc_ref.shape, F32)
    kt_last = (q0 + Q_BLOCK - 1) // KV_TILE

    def sel_tile(kt, causal):
        k0 = pl.multiple_of(kt * KV_TILE, KV_TILE)
        k = kaug_ref[pl.ds(k0, KV_TILE), :]
        v = vsel_ref[pl.ds(k0, KV_TILE), :]
        for rc in range(n_chunks):
            rows = slice(rc * ROW_CHUNK, (rc + 1) * ROW_CHUNK)
            s = _dot_nt(qaug_ref[rows, :], k)
            if causal:
                kpos = k0 + lax.broadcasted_iota(jnp.int32, (1, KV_TILE), 1)
                s = jnp.where(kpos <= qpos_rows(ROW_CHUNK), s, NEG_BIG)
            m_old = m_ref[rows, :]
            m_new = jnp.maximum(m_old, jnp.max(s, axis=-1, keepdims=True))
            alpha = jnp.exp(m_old - m_new)
            p = jnp.exp(s - m_new)
            l_ref[rows, :] = alpha * l_ref[rows, :] + jnp.sum(p, axis=-1, keepdims=True)
            acc_ref[rows, :] = alpha * acc_ref[rows, :] + _dot(p.astype(BF16), v)
            m_ref[rows, :] = m_new

    def body(kt, carry):
        sel_tile(kt, False)
        return carry

    lax.fori_loop(0, kt_last, body, 0)
    sel_tile(kt_last, True)

    wk = kwin_ref.shape[0] if kwin_ref.shape[0] < WINDOW + Q_BLOCK else WINDOW + Q_BLOCK
    start = pl.multiple_of(jnp.maximum(q0 - WINDOW, 0), Q_BLOCK)
    kw = kwin_ref[pl.ds(start, wk), :]
    vw = vwin_ref[pl.ds(start, wk), :]
    kwpos = start + lax.broadcasted_iota(jnp.int32, (1, wk), 1)
    for rc in range(n_chunks):
        rows = slice(rc * ROW_CHUNK, (rc + 1) * ROW_CHUNK)
        s = _dot_nt(qaug_ref[rows, 0:LANES], kw)
        dt = qpos_rows(ROW_CHUNK) - kwpos
        e, den = _masked_softmax_parts(s, (dt >= 0) & (dt < WINDOW))
        ow_ref[rows, :] = _dot(e.astype(BF16), vw) * (1.0 / den)

    lane = lax.broadcasted_iota(jnp.int32, (Q_BLOCK, LANES), 1)
    lower = lane < HEAD_DIM
    gates = gates_ref[...]
    for pair in range(N_HEADS // 2):
        halves = []
        for n in (2 * pair, 2 * pair + 1):
            rows = slice(n * Q_BLOCK, (n + 1) * Q_BLOCK)
            o_sel = acc_ref[rows, :] * (1.0 / l_ref[rows, :])
            o = (gates[:, 3 * n:3 * n + 1] * oc_ref[rows, :] + gates[:, 3 * n + 1:3 * n + 2] * o_sel
                 + gates[:, 3 * n + 2:3 * n + 3] * ow_ref[rows, :])
            halves.append(o)
        g = (2 * pair) // HEADS_PER_GROUP
        if g == 0:
            merged = jnp.where(lower, halves[0], pltpu.roll(halves[1], HEAD_DIM, 1))
        else:
            merged = jnp.where(lower, pltpu.roll(halves[0], HEAD_DIM, 1), halves[1])
        o_ref[:, pair * LANES:(pair + 1) * LANES] = merged


def _attn_prompt(qpad, ksel, kwinb, kvcmp, gates, ovl, batch, seq):
    nq = seq // Q_BLOCK
    n_cmp = kvcmp.shape[1]
    n_rows = N_HEADS * Q_BLOCK
    rowq = lambda w: pl.BlockSpec((Q_BLOCK, w), lambda b, i: (b * nq + i, 0))
    return pl.pallas_call(
        _attn_prompt_kernel,
        grid=(batch, nq),
        in_specs=[
            rowq(N_HEADS * LANES),
            pl.BlockSpec((seq, 2 * LANES), lambda b, i: (b, 0)),
            pl.BlockSpec((seq, LANES), lambda b, i: (b, 2)),
            pl.BlockSpec((seq, LANES), lambda b, i: (b, 0)),
            pl.BlockSpec((seq, LANES), lambda b, i: (b, 1)),
            pl.BlockSpec((1, n_cmp, 2 * LANES), lambda b, i: (b, 0, 0)),
            rowq(LANES),
            pl.BlockSpec((n_cmp, LANES), lambda b, i: (0, 0)),
        ],
        out_specs=rowq(ATTN_WIDTH),
        out_shape=jax.ShapeDtypeStruct((batch * seq, ATTN_WIDTH), F32),
        scratch_shapes=[
            pltpu.VMEM((n_rows, 2 * LANES), BF16),
            pltpu.VMEM((n_rows, 1), F32),
            pltpu.VMEM((n_rows, 1), F32),
            pltpu.VMEM((n_rows, LANES), F32),
            pltpu.VMEM((n_rows, LANES), F32),
            pltpu.VMEM((n_rows, LANES), F32),
        ],
        compiler_params=pltpu.CompilerParams(dimension_semantics=("arbitrary", "arbitrary"),
                                             vmem_limit_bytes=VMEM_LIMIT),
        name="attn_prompt",
    )(qpad, ksel, ksel, kwinb, kwinb, kvcmp, gates, ovl)


def _finish_kernel(x_ref, o_ref, sa_ref, ga_ref, pb_ref, gate_ref, gpost_ref, wbra_ref, wout_ref, y_ref):
    ya = _dot((o_ref[...] * sa_ref[...]).astype(BF16), wbra_ref[...])
    mix = ga_ref[...] * ya + pb_ref[...]
    o = _dot(mix.astype(BF16), wout_ref[...])
    ms = jnp.mean(o * o, axis=-1, keepdims=True)
    on = o * lax.rsqrt(ms + RMS_EPS) * gpost_ref[...]
    y_ref[...] = x_ref[...] + gate_ref[...] * on


def _finish(x2d, o_attn, sa, ga, pb, gate, g_post, w_br_a, w_out, tm, rows_per_gate):
    n = x2d.shape[0]
    row = lambda w: pl.BlockSpec((tm, w), lambda i: (i, 0))
    const = lambda shp: pl.BlockSpec(shp, lambda i: (0,) * len(shp))
    if rows_per_gate == 1:
        gate_spec = row(D_MODEL)
    else:
        tiles_per_gate = rows_per_gate // tm
        gate = gate[:, None, :]
        gate_spec = pl.BlockSpec((None, 1, D_MODEL), lambda i: (i // tiles_per_gate, 0, 0))
    return pl.pallas_call(
        _finish_kernel,
        grid=(n // tm,),
        in_specs=[row(D_MODEL), row(ATTN_WIDTH), row(ATTN_WIDTH), row(D_MODEL), row(D_MODEL), gate_spec,
                  const((1, D_MODEL)), const((ATTN_WIDTH, D_MODEL)), const((D_MODEL, D_MODEL))],
        out_specs=row(D_MODEL),
        out_shape=jax.ShapeDtypeStruct((n, D_MODEL), F32),
        compiler_params=pltpu.CompilerParams(dimension_semantics=("arbitrary",), vmem_limit_bytes=VMEM_LIMIT),
        name="finish",
    )(x2d, o_attn, sa, ga, pb, gate, g_post, w_br_a, w_out)


PAGE_PIECES = 4
WIN_PIECES = 2


def _attn_decode_kernel(pt_ref, *refs, n_pages, page_size, wbuf):
    del pt_ref
    page_refs = refs[:n_pages]
    (win_ref, q_ref, kvnew_ref, gates_ref, pelo_ref, pehi_ref, w1lo_ref, w1hi_ref, w2_ref, ovl_ref, expand_ref,
     o_ref, winout_ref) = refs[n_pages:]
    past_len = n_pages * page_size
    n_chunk = past_len // CMP_STRIDE
    chunks_per_page = page_size // CMP_STRIDE
    q8f = q_ref[...]
    q8 = q8f.astype(BF16)

    def piece(ref, which, n_rows, n_pieces):
        return ref[pl.ds(which, n_rows, stride=n_pieces), :]

    kvc = []
    for kind in range(2):
        rows = []
        for pr in page_refs:
            rows.append(jnp.concatenate(
                [pr[pl.ds(PAGE_PIECES * r + kind, chunks_per_page, stride=PAGE_PIECES * CMP_STRIDE), :]
                 for r in range(CMP_STRIDE)], axis=1))
        c = jnp.concatenate(rows, axis=0)
        lo, hi = _compress_chunks(c, kind, pelo_ref, pehi_ref, w1lo_ref, w1hi_ref)
        hid = lo + pltpu.roll(hi, n_chunk - 1, 0)
        kvc.append(_dot(_silu(hid).astype(BF16), w2_ref[kind]).astype(BF16))
    kc, vc = kvc

    c_end = lax.broadcasted_iota(jnp.int32, (1, n_chunk), 1) * CMP_STRIDE + (CMP_BLOCK - 1)
    e, den = _masked_softmax_parts(_dot_nt(q8, kc), c_end <= past_len)
    p = e * (1.0 / den)
    o_cmp = _dot(p.astype(BF16), vc)

    row = lax.broadcasted_iota(jnp.int32, p.shape, 0)
    in_g0 = row < HEADS_PER_GROUP
    g0 = jnp.sum(jnp.where(in_g0, p, 0.0), axis=0, keepdims=True)
    g1 = jnp.sum(jnp.where(in_g0, 0.0, p), axis=0, keepdims=True)
    hi, lo = _split_bf16(jnp.where(in_g0, g0, g1))
    imp = _dot(hi, ovl_ref[...]) + _dot(lo, ovl_ref[...])
    cur = past_len // SEL_BLOCK
    j = lax.broadcasted_iota(jnp.int32, imp.shape, 1)
    forced = (j == 0) | (j == cur) | (j == cur - 1)
    score = jnp.where(j > cur, -jnp.inf, jnp.where(forced, FORCED_SCORE, imp))
    bias_keys = _dot(_top_k_bias(score).astype(BF16), expand_ref[...])

    s = jnp.concatenate([_dot_nt(q8, piece(pr, 2, page_size, PAGE_PIECES).astype(BF16)) for pr in page_refs], axis=1)
    s = s + bias_keys
    s_new = jnp.sum(q8f * kvnew_ref[2:3, :], axis=1, keepdims=True)
    m = jnp.maximum(jnp.max(s, axis=1, keepdims=True), s_new)
    e = jnp.exp(s - m)
    e_new = jnp.exp(s_new - m)
    den = jnp.sum(e, axis=1, keepdims=True) + e_new
    acc = e_new * kvnew_ref[3:4, :]
    for i, pr in enumerate(page_refs):
        acc = acc + _dot(e[:, i * page_size:(i + 1) * page_size].astype(BF16),
                         piece(pr, 3, page_size, PAGE_PIECES).astype(BF16))
    o_sel = acc * (1.0 / den)

    kwpos = past_len - wbuf + lax.broadcasted_iota(jnp.int32, (1, wbuf), 1)
    dt = past_len - kwpos
    valid = (dt >= 0) & (dt < WINDOW) & (kwpos >= 0)
    s = jnp.where(valid, _dot_nt(q8, piece(win_ref, 0, wbuf, WIN_PIECES).astype(BF16)), -jnp.inf)
    s_new = jnp.sum(q8f * kvnew_ref[4:5, :], axis=1, keepdims=True)
    m = jnp.maximum(jnp.max(s, axis=1, keepdims=True), s_new)
    e = jnp.exp(s - m)
    e_new = jnp.exp(s_new - m)
    den = jnp.sum(e, axis=1, keepdims=True) + e_new
    o_win = (_dot(e.astype(BF16), piece(win_ref, 1, wbuf, WIN_PIECES).astype(BF16)) + e_new * kvnew_ref[5:6, :]) * (1.0 / den)

    gates = gates_ref[...]
    o = gates[:, 0:1] * o_cmp + gates[:, 1:2] * o_sel + gates[:, 2:3] * o_win
    o_ref[...] = jnp.where(in_g0, o, pltpu.roll(o, HEAD_DIM, 1))

    n_win = WIN_PIECES * wbuf
    winout_ref[0:n_win - WIN_PIECES, :] = win_ref[WIN_PIECES:n_win, :]
    winout_ref[n_win - WIN_PIECES:n_win, :] = kvnew_ref[4:6, :]


def _attn_decode(cache, page_table, win, q8, kvnew, gates8, cw, ovl, expand):
    n_phys, page_size = cache.shape[0], cache.shape[1]
    batch, n_pages = page_table.shape
    wbuf = win.shape[1]
    cache2d = cache.reshape(n_phys * page_size * PAGE_PIECES, LANES)
    win2d = win.reshape(batch * wbuf * WIN_PIECES, LANES)
    page_rows = page_size * PAGE_PIECES
    win_rows = wbuf * WIN_PIECES

    def page_spec(k):
        return pl.BlockSpec((page_rows, LANES), lambda b, pt: (pt[b, k], 0))

    per_b = lambda a: pl.BlockSpec((None,) + a.shape[1:], lambda b, pt: (b,) + (0,) * (a.ndim - 1))
    const = lambda a: pl.BlockSpec(a.shape, lambda b, pt: (0,) * a.ndim)
    consts = [cw["pe_lo"], cw["pe_hi"], cw["w1_lo"], cw["w1_hi"], cw["w2"], ovl, expand]
    grid_spec = pltpu.PrefetchScalarGridSpec(
        num_scalar_prefetch=1,
        grid=(batch,),
        in_specs=[page_spec(k) for k in range(n_pages)]
        + [pl.BlockSpec((win_rows, LANES), lambda b, pt: (b, 0)), per_b(q8), per_b(kvnew), per_b(gates8)]
        + [const(a) for a in consts],
        out_specs=(pl.BlockSpec((None, N_HEADS, LANES), lambda b, pt: (b, 0, 0)),
                   pl.BlockSpec((win_rows, LANES), lambda b, pt: (b, 0))),
    )
    o8, win_out = pl.pallas_call(
        functools.partial(_attn_decode_kernel, n_pages=n_pages, page_size=page_size, wbuf=wbuf),
        grid_spec=grid_spec,
        out_shape=(jax.ShapeDtypeStruct((batch, N_HEADS, LANES), F32),
                   jax.ShapeDtypeStruct((batch * win_rows, LANES), F32)),
        compiler_params=pltpu.CompilerParams(dimension_semantics=("arbitrary",), vmem_limit_bytes=VMEM_LIMIT),
        name="attn_decode",
    )(page_table, *([cache2d] * n_pages), win2d, q8, kvnew, gates8, *consts)
    return o8, win_out


def _overlap_matrix(n_cmp_pad, n_cmp, n_selb):
    cs = jnp.arange(n_cmp_pad) * CMP_STRIDE
    ss = jnp.arange(LANES) * SEL_BLOCK
    ov = (cs[:, None] < ss[None, :] + SEL_BLOCK) & (cs[:, None] + CMP_BLOCK > ss[None, :])
    ov = ov & (jnp.arange(n_cmp_pad) < n_cmp)[:, None] & (jnp.arange(LANES) < n_selb)[None, :]
    return ov.astype(BF16)


def _prompt_layer(x, mod, wts):
    batch, seq, _ = x.shape
    x2d = x.reshape(batch * seq, D_MODEL)
    shift, scale, gate = mod[:, 0:D_MODEL], mod[:, D_MODEL:2 * D_MODEL], mod[:, 2 * D_MODEL:]
    tabs = _rope_tables(jnp.arange(seq))
    tm = min(256, seq)
    (qpad, kvrows, kwin, ksel, kwinb, gates, sa, ga, pb, utail) = _proj_prompt(
        x2d, shift, scale, wts["g_pre"], wts["w_in"], tabs, wts["conv_w"], wts["w_br_b"], batch, seq, tm)
    kvcmp = _compress_prompt(kvrows, wts["cmp"], batch, seq)
    n_chunk = seq // CMP_STRIDE
    ovl = _overlap_matrix(n_chunk, n_chunk - 1, -(-seq // SEL_BLOCK))
    o_attn = _attn_prompt(qpad, ksel, kwinb, kvcmp, gates, ovl, batch, seq)
    y = _finish(x2d, o_attn, sa, ga, pb, gate, wts["g_post"], wts["w_br_a"], wts["w_out"], tm, seq)
    n_keep = min(WINDOW, seq)
    return (y.reshape(batch, seq, D_MODEL),
            kvrows.reshape(batch, seq, 4, KV_HEADS, HEAD_DIM),
            kwin.reshape(batch, seq, 2, KV_HEADS, HEAD_DIM)[:, seq - n_keep:],
            utail[:, SUBLANES - (CONV_K - 1):])


def _sample_layer(x, mod, cache, page_table, win, conv_state, wts):
    batch, dec_seq, _ = x.shape
    assert dec_seq == 1
    n_pages, page_size, wbuf = page_table.shape[1], cache.shape[1], win.shape[1]
    past_len = n_pages * page_size
    assert past_len % SEL_BLOCK == 0 and wbuf == WINDOW and past_len // SEL_BLOCK < LANES
    x2d = x.reshape(batch, D_MODEL)
    shift, scale, gate = mod[:, 0:D_MODEL], mod[:, D_MODEL:2 * D_MODEL], mod[:, 2 * D_MODEL:]
    tabs = _rope_tables(jnp.full((1,), past_len, jnp.int32))
    cbuf = conv_state.reshape(batch, (CONV_K - 1) * CONV_WIDTH)
    qpad, kvnew, gates, sa, ga, pb, u = _proj_sample(
        x2d, shift, scale, wts["g_pre"], wts["w_in"], tabs, wts["conv_w"], wts["w_br_b"], cbuf)
    n_gate = N_HEADS * 3
    gates8 = jnp.pad(gates[:, :n_gate].reshape(batch, N_HEADS, 3), ((0, 0), (0, 0), (0, LANES - 3)))
    n_chunk = past_len // CMP_STRIDE
    ovl = _overlap_matrix(n_chunk, n_chunk - 1, past_len // SEL_BLOCK + 1)
    expand = (jnp.arange(LANES)[:, None] == (jnp.arange(past_len) // SEL_BLOCK)[None, :]).astype(BF16)
    o8, win_out = _attn_decode(cache, page_table, win, qpad.reshape(batch, N_HEADS, LANES).astype(F32),
                               kvnew.reshape(batch, 6, LANES), gates8, wts["cmp"], ovl, expand)
    y = _finish(x2d, o8[:, :, :HEAD_DIM].reshape(batch, ATTN_WIDTH), sa, ga, pb, gate, wts["g_post"], wts["w_br_a"],
                wts["w_out"], batch, 1)
    return (y.reshape(batch, 1, D_MODEL),
            kvnew[:, :4 * KV_WIDTH].reshape(batch, 1, 4, KV_HEADS, HEAD_DIM),
            win_out.reshape(batch, wbuf, 2, KV_HEADS, HEAD_DIM),
            jnp.stack([conv_state[:, CONV_K - 2], u], axis=1))


def _prep_weights(w_ada, b_ada, g_pre, g_post, w_in, pe_cmp, w_cmp1, w_cmp2, conv_w, w_br_a, w_br_b, w_out):
    n_unpadded_gate = HEADS_PER_GROUP * KV_HEADS * 3
    w_pad = jnp.concatenate(
        [w_in[:, :C_G + n_unpadded_gate], jnp.zeros((D_MODEL, LANES - n_unpadded_gate), w_in.dtype),
         w_in[:, C_G + n_unpadded_gate:]], axis=1).astype(BF16)
    half = CMP_STRIDE * HEAD_DIM
    return dict(
        w_ada=w_ada, b_ada=b_ada, g_pre=g_pre.reshape(1, -1), g_post=g_post.reshape(1, -1), w_in=w_pad,
        cmp=_compress_weights(pe_cmp, w_cmp1, w_cmp2), conv_w=conv_w,
        w_br_a=w_br_a.astype(BF16), w_br_b=w_br_b.astype(BF16), w_out=w_out.astype(BF16))


def kernel(x_prompt, x_sample, cache_kv_pages, state_win_kv, state_conv, page_table, c_prompt, c_sample, w_ada, b_ada, g_pre, g_post, w_in, pe_cmp, w_cmp1, w_cmp2, conv_w, w_br_a, w_br_b, w_out):
    depth = w_in.shape[0]
    assert depth == 1
    wts = _prep_weights(w_ada[0], b_ada[0], g_pre[0], g_post[0], w_in[0], pe_cmp[0], w_cmp1[0], w_cmp2[0],
                        conv_w[0], w_br_a[0], w_br_b[0], w_out[0])
    n_prompt = c_prompt.shape[0]
    mod = _ada(jnp.concatenate([c_prompt, c_sample], axis=0), wts["w_ada"], wts["b_ada"])
    yp, kvp, wp, cp = _prompt_layer(x_prompt, mod[:n_prompt], wts)
    ys, kvs, ws, cs = _sample_layer(x_sample, mod[n_prompt:], cache_kv_pages[0], page_table, state_win_kv[0],
                                    state_conv[0], wts)
    return (yp, ys, kvp[None], wp[None], cp[None], kvs[None], ws[None], cs[None])
```

```python
import functools

import jax
import jax.numpy as jnp
from jax import lax
from jax.experimental import pallas as pl
from jax.experimental.pallas import tpu as pltpu

F32 = jnp.float32
BF16 = jnp.bfloat16

D_MODEL = 1024
N_HEADS = 8
KV_HEADS = 2
HEADS_PER_GROUP = N_HEADS // KV_HEADS
HEAD_DIM = 64
ROPE_DIM = HEAD_DIM // 4
ROPE_THETA = 500000.0
CMP_BLOCK = 32
CMP_STRIDE = 16
CMP_HIDDEN = 256
SEL_BLOCK = 64
N_SEL = 16
WINDOW = 512
Q_BLOCK = 128
FORCED_SCORE = 1e6
CONV_WIDTH = D_MODEL // 2
CONV_K = 3
ATTN_WIDTH = N_HEADS * HEAD_DIM
KV_WIDTH = KV_HEADS * HEAD_DIM
RMS_EPS = 1e-6

LANES = 128
SUBLANES = 8
VMEM_LIMIT = 56 * 1024 * 1024

C_Q = 0
C_KV = C_Q + ATTN_WIDTH
C_G = C_KV + 6 * KV_WIDTH
C_A = C_G + LANES
C_CB = C_A + ATTN_WIDTH
C_CC = C_CB + CONV_WIDTH
C_CX = C_CC + CONV_WIDTH
C_CG = C_CX + CONV_WIDTH
C_MA = C_CG + CONV_WIDTH
C_MB = C_MA + D_MODEL
IN_PAD = C_MB + D_MODEL

PAGE_ROWS = 4 * KV_HEADS
WIN_ROWS = 2 * KV_HEADS

NEG_BIG = -1e30
KV_TILE = 512
ROW_CHUNK = 256


def _sigmoid(x):
    return 1.0 / (1.0 + jnp.exp(-x))


def _silu(x):
    return x * _sigmoid(x)


def _dot(a, b):
    return jnp.dot(a, b, preferred_element_type=F32)


def _dot_nt(a, b):
    return lax.dot_general(a, b, (((1,), (1,)), ((), ())), preferred_element_type=F32)


def _rope_tables(pos):
    half = ROPE_DIM // 2
    inv = ROPE_THETA ** (-jnp.arange(half, dtype=F32) / half)
    ang = pos.astype(F32)[:, None] * inv[None, :]
    cos, sin = jnp.cos(ang), jnp.sin(ang)
    n = pos.shape[0]
    a = jnp.concatenate([cos, cos, jnp.ones((n, HEAD_DIM - ROPE_DIM), F32)], axis=1)
    p = jnp.concatenate([jnp.zeros((n, half), F32), sin, jnp.zeros((n, HEAD_DIM - ROPE_DIM), F32)], axis=1)
    m = jnp.concatenate([-sin, jnp.zeros((n, HEAD_DIM - half), F32)], axis=1)
    tile = lambda t: jnp.concatenate([t, t], axis=1)
    return tile(a), tile(p), tile(m)


def _rope(x, ra, rp, rm):
    half = ROPE_DIM // 2
    return x * ra + pltpu.roll(x, half, 1) * rp + pltpu.roll(x, LANES - half, 1) * rm


def _ada_kernel(c_ref, w_ref, b_ref, o_ref):
    c = _silu(c_ref[...]).astype(BF16)
    o_ref[...] = _dot(c, w_ref[...].astype(BF16)) + b_ref[...]


def _ada(c_all, w_ada, b_ada):
    n = c_all.shape[0]
    tn = 512
    return pl.pallas_call(
        _ada_kernel,
        grid=(3 * D_MODEL // tn,),
        in_specs=[
            pl.BlockSpec((n, D_MODEL), lambda j: (0, 0)),
            pl.BlockSpec((D_MODEL, tn), lambda j: (0, j)),
            pl.BlockSpec((1, tn), lambda j: (0, j)),
        ],
        out_specs=pl.BlockSpec((n, tn), lambda j: (0, j)),
        out_shape=jax.ShapeDtypeStruct((n, 3 * D_MODEL), F32),
        compiler_params=pltpu.CompilerParams(dimension_semantics=("arbitrary",), vmem_limit_bytes=VMEM_LIMIT),
        name="ada",
    )(c_all, w_ada, b_ada.reshape(1, -1))


def _proj_common(x_ref, shift_ref, scale_ref, gpre_ref, w_ref, ra_ref, rp_ref, rm_ref):
    x = x_ref[...]
    ms = jnp.mean(x * x, axis=-1, keepdims=True)
    xn = x * lax.rsqrt(ms + RMS_EPS) * gpre_ref[...]
    h = xn * (1.0 + scale_ref[...]) + shift_ref[...]
    hb = h.astype(BF16)
    ra, rp, rm = ra_ref[...], rp_ref[...], rm_ref[...]

    def seg(lo, hi):
        return _dot(hb, w_ref[:, lo:hi])

    return seg, (ra, rp, rm)


def _emit_q(seg, rope, qpad_ref):
    zq = seg(C_Q, C_Q + ATTN_WIDTH)
    lane = lax.broadcasted_iota(jnp.int32, (zq.shape[0], LANES), 1)
    lower = lane < HEAD_DIM
    for j in range(ATTN_WIDTH // LANES):
        c = _rope(zq[:, j * LANES:(j + 1) * LANES], *rope) * (HEAD_DIM ** -0.5)
        r = pltpu.roll(c, HEAD_DIM, 1)
        if (2 * j) // HEADS_PER_GROUP == 0:
            even, odd = jnp.where(lower, c, 0.0), jnp.where(lower, r, 0.0)
        else:
            even, odd = jnp.where(lower, 0.0, r), jnp.where(lower, 0.0, c)
        qpad_ref[:, (2 * j) * LANES:(2 * j + 1) * LANES] = even.astype(BF16)
        qpad_ref[:, (2 * j + 1) * LANES:(2 * j + 2) * LANES] = odd.astype(BF16)


def _kv_pieces(seg, rope):
    zkv = seg(C_KV, C_KV + 6 * KV_WIDTH)
    pieces = []
    for p in range(6):
        c = zkv[:, p * LANES:(p + 1) * LANES]
        pieces.append(_rope(c, *rope) if p % 2 == 0 else c)
    return pieces


def _branch_b(seg, um2, um1, u, convw_ref, wbrb_ref):
    cb = seg(C_CB, C_CB + CONV_WIDTH)
    conv = convw_ref[0:1, :] * um2
    conv = conv + convw_ref[1:2, :] * um1
    conv = conv + convw_ref[2:3, :] * u
    ybin = cb * conv * _silu(seg(C_CG, C_CG + CONV_WIDTH))
    yb = _dot(ybin.astype(BF16), wbrb_ref[...])
    gb = _sigmoid(seg(C_MB, C_MB + D_MODEL))
    return gb * yb


def _proj_prompt_kernel(x_ref, shift_ref, scale_ref, gpre_ref, w_ref, ra_ref, rp_ref, rm_ref, convw_ref, wbrb_ref,
                        qpad_ref, kvrows_ref, kvcmp_ref, kwin_ref, ksel_ref, kwinb_ref, gates_ref, sa_ref, ga_ref,
                        pb_ref, utail_ref, carry_ref):
    ti = pl.program_id(1)
    tm = x_ref.shape[0]
    seg, rope = _proj_common(x_ref, shift_ref, scale_ref, gpre_ref, w_ref, ra_ref, rp_ref, rm_ref)
    _emit_q(seg, rope, qpad_ref)

    pieces = _kv_pieces(seg, rope)
    for p in range(4):
        for g in range(KV_HEADS):
            kvrows_ref[pl.ds(p * KV_HEADS + g, tm, stride=PAGE_ROWS), :] = pieces[p][:, g * HEAD_DIM:(g + 1) * HEAD_DIM]
    kvcmp_ref[:, 0:LANES] = pieces[0]
    kvcmp_ref[:, LANES:2 * LANES] = pieces[1]
    kwin_ref[:, 0:LANES] = pieces[4]
    kwin_ref[:, LANES:2 * LANES] = pieces[5]
    kwinb_ref[:, 0:LANES] = pieces[4].astype(BF16)
    kwinb_ref[:, LANES:2 * LANES] = pieces[5].astype(BF16)
    row = ti * tm + lax.broadcasted_iota(jnp.int32, (tm, LANES), 0)
    lane = lax.broadcasted_iota(jnp.int32, (tm, LANES), 1)
    onehot = jnp.where(lane == row // SEL_BLOCK, 1.0, 0.0)
    ksel_ref[:, 0:LANES] = pieces[2].astype(BF16)
    ksel_ref[:, LANES:2 * LANES] = onehot.astype(BF16)
    ksel_ref[:, 2 * LANES:3 * LANES] = pieces[3].astype(BF16)

    gates_ref[...] = _sigmoid(seg(C_G, C_G + LANES))
    sa_ref[...] = _silu(seg(C_A, C_A + ATTN_WIDTH))
    ga_ref[...] = _sigmoid(seg(C_MA, C_MA + D_MODEL))

    @pl.when(ti == 0)
    def _():
        carry_ref[...] = jnp.zeros_like(carry_ref)

    u = seg(C_CC, C_CC + CONV_WIDTH) * seg(C_CX, C_CX + CONV_WIDTH)
    r = lax.broadcasted_iota(jnp.int32, u.shape, 0)
    c7 = carry_ref[SUBLANES - 1:SUBLANES, :]
    c6 = carry_ref[SUBLANES - 2:SUBLANES - 1, :]
    um1 = jnp.where(r == 0, c7, pltpu.roll(u, 1, 0))
    um2 = jnp.where(r == 0, c6, jnp.where(r == 1, c7, pltpu.roll(u, 2, 0)))
    pb_ref[...] = _branch_b(seg, um2, um1, u, convw_ref, wbrb_ref)
    tail = u[tm - SUBLANES:tm, :]
    carry_ref[...] = tail
    utail_ref[0] = tail


def _proj_prompt(x2d, shift, scale, g_pre, w_pad, rope_tabs, conv_w, w_br_b, batch, seq, tm):
    n = batch * seq
    nt = seq // tm
    row = lambda w: pl.BlockSpec((tm, w), lambda b, t: (b * nt + t, 0))
    per_b = lambda w: pl.BlockSpec((None, 1, w), lambda b, t: (b, 0, 0))
    const = lambda shp: pl.BlockSpec(shp, lambda b, t: (0,) * len(shp))
    tab = pl.BlockSpec((tm, LANES), lambda b, t: (t, 0))
    out_shapes = (
        jax.ShapeDtypeStruct((n, N_HEADS * LANES), BF16),
        jax.ShapeDtypeStruct((n * PAGE_ROWS, HEAD_DIM), F32),
        jax.ShapeDtypeStruct((n, 2 * KV_WIDTH), F32),
        jax.ShapeDtypeStruct((n, 2 * KV_WIDTH), F32),
        jax.ShapeDtypeStruct((n, 3 * LANES), BF16),
        jax.ShapeDtypeStruct((n, 2 * KV_WIDTH), BF16),
        jax.ShapeDtypeStruct((n, LANES), F32),
        jax.ShapeDtypeStruct((n, ATTN_WIDTH), F32),
        jax.ShapeDtypeStruct((n, D_MODEL), F32),
        jax.ShapeDtypeStruct((n, D_MODEL), F32),
        jax.ShapeDtypeStruct((batch, SUBLANES, CONV_WIDTH), F32),
    )
    out_specs = (
        row(N_HEADS * LANES), pl.BlockSpec((tm * PAGE_ROWS, HEAD_DIM), lambda b, t: (b * nt + t, 0)),
        row(2 * KV_WIDTH), row(2 * KV_WIDTH), row(3 * LANES), row(2 * KV_WIDTH),
        row(LANES), row(ATTN_WIDTH), row(D_MODEL), row(D_MODEL),
        pl.BlockSpec((1, SUBLANES, CONV_WIDTH), lambda b, t: (b, 0, 0)),
    )
    return pl.pallas_call(
        _proj_prompt_kernel,
        grid=(batch, nt),
        in_specs=[row(D_MODEL), per_b(D_MODEL), per_b(D_MODEL), const((1, D_MODEL)), const((D_MODEL, IN_PAD)),
                  tab, tab, tab, const((CONV_K, CONV_WIDTH)), const((CONV_WIDTH, D_MODEL))],
        out_specs=out_specs,
        out_shape=out_shapes,
        scratch_shapes=[pltpu.VMEM((SUBLANES, CONV_WIDTH), F32)],
        compiler_params=pltpu.CompilerParams(dimension_semantics=("arbitrary", "arbitrary"),
                                             vmem_limit_bytes=VMEM_LIMIT),
        name="proj_prompt",
    )(x2d, shift[:, None, :], scale[:, None, :], g_pre, w_pad, *rope_tabs, conv_w, w_br_b)


def _proj_sample_kernel(x_ref, shift_ref, scale_ref, gpre_ref, w_ref, ra_ref, rp_ref, rm_ref, convw_ref, wbrb_ref,
                        cbuf_ref, qpad_ref, kvnew_ref, gates_ref, sa_ref, ga_ref, pb_ref, u_ref):
    seg, rope = _proj_common(x_ref, shift_ref, scale_ref, gpre_ref, w_ref, ra_ref, rp_ref, rm_ref)
    _emit_q(seg, rope, qpad_ref)
    pieces = _kv_pieces(seg, rope)
    for p in range(6):
        kvnew_ref[:, p * LANES:(p + 1) * LANES] = pieces[p]
    gates_ref[...] = _sigmoid(seg(C_G, C_G + LANES))
    sa_ref[...] = _silu(seg(C_A, C_A + ATTN_WIDTH))
    ga_ref[...] = _sigmoid(seg(C_MA, C_MA + D_MODEL))
    u = seg(C_CC, C_CC + CONV_WIDTH) * seg(C_CX, C_CX + CONV_WIDTH)
    um2 = cbuf_ref[:, 0:CONV_WIDTH]
    um1 = cbuf_ref[:, CONV_WIDTH:2 * CONV_WIDTH]
    pb_ref[...] = _branch_b(seg, um2, um1, u, convw_ref, wbrb_ref)
    u_ref[...] = u


def _proj_sample(x2d, shift, scale, g_pre, w_pad, rope_tabs, conv_w, w_br_b, cbuf):
    n = x2d.shape[0]
    full = lambda shp: pl.BlockSpec(shp, lambda i: (0,) * len(shp))
    out_shapes = (
        jax.ShapeDtypeStruct((n, N_HEADS * LANES), BF16),
        jax.ShapeDtypeStruct((n, 6 * KV_WIDTH), F32),
        jax.ShapeDtypeStruct((n, LANES), F32),
        jax.ShapeDtypeStruct((n, ATTN_WIDTH), F32),
        jax.ShapeDtypeStruct((n, D_MODEL), F32),
        jax.ShapeDtypeStruct((n, D_MODEL), F32),
        jax.ShapeDtypeStruct((n, CONV_WIDTH), F32),
    )
    return pl.pallas_call(
        _proj_sample_kernel,
        grid=(1,),
        in_specs=[full((n, D_MODEL)), full((n, D_MODEL)), full((n, D_MODEL)), full((1, D_MODEL)),
                  full((D_MODEL, IN_PAD)), full((1, LANES)), full((1, LANES)), full((1, LANES)),
                  full((CONV_K, CONV_WIDTH)), full((CONV_WIDTH, D_MODEL)), full((n, 2 * CONV_WIDTH))],
        out_specs=tuple(full(s.shape) for s in out_shapes),
        out_shape=out_shapes,
        compiler_params=pltpu.CompilerParams(dimension_semantics=("arbitrary",), vmem_limit_bytes=VMEM_LIMIT),
        name="proj_sample",
    )(x2d, shift, scale, g_pre, w_pad, *rope_tabs, conv_w, w_br_b, cbuf)


CHUNK_LANES = CMP_STRIDE * KV_WIDTH


def _compress_weights(pe_cmp, w_cmp1, w_cmp2):
    zeros = jnp.zeros((2, CMP_STRIDE, HEAD_DIM, CMP_HIDDEN), w_cmp1.dtype)

    def both_groups(w_half):
        w = w_half.reshape(2, CMP_STRIDE, HEAD_DIM, CMP_HIDDEN)
        g0 = jnp.concatenate([w, zeros], axis=2)
        g1 = jnp.concatenate([zeros, w], axis=2)
        return jnp.concatenate([g0, g1], axis=3).reshape(2, CHUNK_LANES, KV_HEADS * CMP_HIDDEN).astype(BF16)

    half = CMP_STRIDE * HEAD_DIM
    z2 = jnp.zeros_like(w_cmp2)
    w2 = jnp.concatenate([jnp.concatenate([w_cmp2, z2], axis=2), jnp.concatenate([z2, w_cmp2], axis=2)], axis=1)
    pe = jnp.concatenate([pe_cmp, pe_cmp], axis=2)
    return dict(w1_lo=both_groups(w_cmp1[:, :half]), w1_hi=both_groups(w_cmp1[:, half:]), w2=w2.astype(BF16),
                pe_lo=pe[:, :CMP_STRIDE].reshape(2, 1, CHUNK_LANES), pe_hi=pe[:, CMP_STRIDE:].reshape(2, 1, CHUNK_LANES))


def _compress_chunks(c, kind, pelo_ref, pehi_ref, w1lo_ref, w1hi_ref):
    lo = _dot((c + pelo_ref[kind]).astype(BF16), w1lo_ref[kind])
    hi = _dot((c + pehi_ref[kind]).astype(BF16), w1hi_ref[kind])
    return lo, hi


def _compress_prompt_kernel(kc_ref, vc_ref, pelo_ref, pehi_ref, w1lo_ref, w1hi_ref, w2_ref, o_ref):
    n_chunk = kc_ref.shape[0] // CMP_STRIDE
    for kind, src_ref in enumerate((kc_ref, vc_ref)):
        c = jnp.concatenate([src_ref[pl.ds(r, n_chunk, stride=CMP_STRIDE), :] for r in range(CMP_STRIDE)], axis=1)
        lo, hi = _compress_chunks(c, kind, pelo_ref, pehi_ref, w1lo_ref, w1hi_ref)
        hid = lo + pltpu.roll(hi, n_chunk - 1, 0)
        out = _dot(_silu(hid).astype(BF16), w2_ref[kind])
        o_ref[0, :, kind * KV_WIDTH:(kind + 1) * KV_WIDTH] = out.astype(BF16)


def _compress_prompt(kvrows, cw, batch, seq):
    n_chunk = seq // CMP_STRIDE
    const = lambda a: pl.BlockSpec(a.shape, lambda b: (0,) * a.ndim)
    return pl.pallas_call(
        _compress_prompt_kernel,
        grid=(batch,),
        in_specs=[
            pl.BlockSpec((seq, KV_WIDTH), lambda b: (b, 0)),
            pl.BlockSpec((seq, KV_WIDTH), lambda b: (b, 1)),
            const(cw["pe_lo"]), const(cw["pe_hi"]), const(cw["w1_lo"]), const(cw["w1_hi"]), const(cw["w2"]),
        ],
        out_specs=pl.BlockSpec((1, n_chunk, 2 * KV_WIDTH), lambda b: (b, 0, 0)),
        out_shape=jax.ShapeDtypeStruct((batch, n_chunk, 2 * KV_WIDTH), BF16),
        compiler_params=pltpu.CompilerParams(dimension_semantics=("arbitrary",), vmem_limit_bytes=VMEM_LIMIT),
        name="compress_prompt",
    )(kvrows, kvrows, cw["pe_lo"], cw["pe_hi"], cw["w1_lo"], cw["w1_hi"], cw["w2"])


def _top_k_bias(score):
    lane = lax.broadcasted_iota(jnp.int32, score.shape, 1).astype(F32)
    bias = jnp.full(score.shape, NEG_BIG, F32)
    for _ in range(N_SEL):
        m = jnp.max(score, axis=1, keepdims=True)
        idx = jnp.min(jnp.where(score == m, lane, float(LANES)), axis=1, keepdims=True)
        hit = lane == idx
        bias = jnp.where(hit, 0.0, bias)
        score = jnp.where(hit, -jnp.inf, score)
    return bias


def _split_bf16(x):
    hi = x.astype(BF16)
    lo = (x - hi.astype(F32)).astype(BF16)
    return hi, lo


def _masked_softmax_parts(s, valid):
    s = jnp.where(valid, s, -jnp.inf)
    m = jnp.max(s, axis=-1, keepdims=True)
    m = jnp.where(m == -jnp.inf, 0.0, m)
    e = jnp.exp(s - m)
    den = jnp.maximum(jnp.sum(e, axis=-1, keepdims=True), 1e-30)
    return e, den


def _attn_prompt_kernel(qpad_ref, kaug_ref, vsel_ref, kwin_ref, vwin_ref, kvcmp_ref, gates_ref, ovl_ref,
                        o_ref, qaug_ref, m_ref, l_ref, acc_ref, oc_ref, ow_ref):
    qb = pl.program_id(1)
    q0 = qb * Q_BLOCK
    seq = kaug_ref.shape[0]
    n_cmp = kvcmp_ref.shape[1]
    n_rows = N_HEADS * Q_BLOCK
    n_chunks = n_rows // ROW_CHUNK
    heads_per_chunk = ROW_CHUNK // Q_BLOCK

    for n in range(N_HEADS):
        qaug_ref[n * Q_BLOCK:(n + 1) * Q_BLOCK, 0:LANES] = qpad_ref[:, n * LANES:(n + 1) * LANES]

    def qpos_rows(rows):
        r = lax.broadcasted_iota(jnp.int32, (rows, 1), 0)
        return q0 + (r & (Q_BLOCK - 1))

    kc = kvcmp_ref[0, :, 0:LANES]
    vc = kvcmp_ref[0, :, LANES:2 * LANES]
    c_end = lax.broadcasted_iota(jnp.int32, (1, n_cmp), 1) * CMP_STRIDE + (CMP_BLOCK - 1)
    psum = [None] * KV_HEADS
    for rc in range(n_chunks):
        rows = slice(rc * ROW_CHUNK, (rc + 1) * ROW_CHUNK)
        s = _dot_nt(qaug_ref[rows, 0:LANES], kc)
        e, den = _masked_softmax_parts(s, c_end <= qpos_rows(ROW_CHUNK))
        p = e * (1.0 / den)
        oc_ref[rows, :] = _dot(p.astype(BF16), vc)
        g = (rc * heads_per_chunk) // HEADS_PER_GROUP
        part = p[0:Q_BLOCK]
        for h in range(1, heads_per_chunk):
            part = part + p[h * Q_BLOCK:(h + 1) * Q_BLOCK]
        psum[g] = part if psum[g] is None else psum[g] + part

    qpos = q0 + lax.broadcasted_iota(jnp.int32, (Q_BLOCK, 1), 0)
    cur = qpos // SEL_BLOCK
    j = lax.broadcasted_iota(jnp.int32, (Q_BLOCK, LANES), 1)
    forced = (j == 0) | (j == cur) | (j == cur - 1)
    for g in range(KV_HEADS):
        hi, lo = _split_bf16(psum[g])
        imp = _dot(hi, ovl_ref[...]) + _dot(lo, ovl_ref[...])
        score = jnp.where(forced, FORCED_SCORE, jnp.where(j <= cur, imp, -1.0))
        bias = _top_k_bias(score).astype(BF16)
        for h in range(HEADS_PER_GROUP):
            n = g * HEADS_PER_GROUP + h
            qaug_ref[n * Q_BLOCK:(n + 1) * Q_BLOCK, LANES:2 * LANES] = bias

    m_ref[...] = jnp.full(m_ref.shape, NEG_BIG, F32)
    l_ref[...] = jnp.zeros(l_ref.shape, F32)
    acc_ref[...] = jnp.zeros(acc_ref.shape, F32)
    kt_last = (q0 + Q_BLOCK - 1) // KV_TILE

    def sel_tile(kt, causal):
        k0 = pl.multiple_of(kt * KV_TILE, KV_TILE)
        k = kaug_ref[pl.ds(k0, KV_TILE), :]
        v = vsel_ref[pl.ds(k0, KV_TILE), :]
        for rc in range(n_chunks):
            rows = slice(rc * ROW_CHUNK, (rc + 1) * ROW_CHUNK)
            s = _dot_nt(qaug_ref[rows, :], k)
            if causal:
                kpos = k0 + lax.broadcasted_iota(jnp.int32, (1, KV_TILE), 1)
                s = jnp.where(kpos <= qpos_rows(ROW_CHUNK), s, NEG_BIG)
            m_old = m_ref[rows, :]
            m_new = jnp.maximum(m_old, jnp.max(s, axis=-1, keepdims=True))
            alpha = jnp.exp(m_old - m_new)
            p = jnp.exp(s - m_new)
            l_ref[rows, :] = alpha * l_ref[rows, :] + jnp.sum(p, axis=-1, keepdims=True)
            acc_ref[rows, :] = alpha * acc_ref[rows, :] + _dot(p.astype(BF16), v)
            m_ref[rows, :] = m_new

    def body(kt, carry):
        sel_tile(kt, False)
        return carry

    lax.fori_loop(0, kt_last, body, 0)
    sel_tile(kt_last, True)

    wk = kwin_ref.shape[0] if kwin_ref.shape[0] < WINDOW + Q_BLOCK else WINDOW + Q_BLOCK
    start = pl.multiple_of(jnp.maximum(q0 - WINDOW, 0), Q_BLOCK)
    kw = kwin_ref[pl.ds(start, wk), :]
    vw = vwin_ref[pl.ds(start, wk), :]
    kwpos = start + lax.broadcasted_iota(jnp.int32, (1, wk), 1)
    for rc in range(n_chunks):
        rows = slice(rc * ROW_CHUNK, (rc + 1) * ROW_CHUNK)
        s = _dot_nt(qaug_ref[rows, 0:LANES], kw)
        dt = qpos_rows(ROW_CHUNK) - kwpos
        e, den = _masked_softmax_parts(s, (dt >= 0) & (dt < WINDOW))
        ow_ref[rows, :] = _dot(e.astype(BF16), vw) * (1.0 / den)

    lane = lax.broadcasted_iota(jnp.int32, (Q_BLOCK, LANES), 1)
    lower = lane < HEAD_DIM
    gates = gates_ref[...]
    for pair in range(N_HEADS // 2):
        halves = []
        for n in (2 * pair, 2 * pair + 1):
            rows = slice(n * Q_BLOCK, (n + 1) * Q_BLOCK)
            o_sel = acc_ref[rows, :] * (1.0 / l_ref[rows, :])
            o = (gates[:, 3 * n:3 * n + 1] * oc_ref[rows, :] + gates[:, 3 * n + 1:3 * n + 2] * o_sel
                 + gates[:, 3 * n + 2:3 * n + 3] * ow_ref[rows, :])
            halves.append(o)
        g = (2 * pair) // HEADS_PER_GROUP
        if g == 0:
            merged = jnp.where(lower, halves[0], pltpu.roll(halves[1], HEAD_DIM, 1))
        else:
            merged = jnp.where(lower, pltpu.roll(halves[0], HEAD_DIM, 1), halves[1])
        o_ref[:, pair * LANES:(pair + 1) * LANES] = merged


def _attn_prompt(qpad, ksel, kwinb, kvcmp, gates, ovl, batch, seq):
    nq = seq // Q_BLOCK
    n_cmp = kvcmp.shape[1]
    n_rows = N_HEADS * Q_BLOCK
    rowq = lambda w: pl.BlockSpec((Q_BLOCK, w), lambda b, i: (b * nq + i, 0))
    return pl.pallas_call(
        _attn_prompt_kernel,
        grid=(batch, nq),
        in_specs=[
            rowq(N_HEADS * LANES),
            pl.BlockSpec((seq, 2 * LANES), lambda b, i: (b, 0)),
            pl.BlockSpec((seq, LANES), lambda b, i: (b, 2)),
            pl.BlockSpec((seq, LANES), lambda b, i: (b, 0)),
            pl.BlockSpec((seq, LANES), lambda b, i: (b, 1)),
            pl.BlockSpec((1, n_cmp, 2 * LANES), lambda b, i: (b, 0, 0)),
            rowq(LANES),
            pl.BlockSpec((n_cmp, LANES), lambda b, i: (0, 0)),
        ],
        out_specs=rowq(ATTN_WIDTH),
        out_shape=jax.ShapeDtypeStruct((batch * seq, ATTN_WIDTH), F32),
        scratch_shapes=[
            pltpu.VMEM((n_rows, 2 * LANES), BF16),
            pltpu.VMEM((n_rows, 1), F32),
            pltpu.VMEM((n_rows, 1), F32),
            pltpu.VMEM((n_rows, LANES), F32),
            pltpu.VMEM((n_rows, LANES), F32),
            pltpu.VMEM((n_rows, LANES), F32),
        ],
        compiler_params=pltpu.CompilerParams(dimension_semantics=("arbitrary", "arbitrary"),
                                             vmem_limit_bytes=VMEM_LIMIT),
        name="attn_prompt",
    )(qpad, ksel, ksel, kwinb, kwinb, kvcmp, gates, ovl)


def _finish_kernel(x_ref, o_ref, sa_ref, ga_ref, pb_ref, gate_ref, gpost_ref, wbra_ref, wout_ref, y_ref):
    ya = _dot((o_ref[...] * sa_ref[...]).astype(BF16), wbra_ref[...])
    mix = ga_ref[...] * ya + pb_ref[...]
    o = _dot(mix.astype(BF16), wout_ref[...])
    ms = jnp.mean(o * o, axis=-1, keepdims=True)
    on = o * lax.rsqrt(ms + RMS_EPS) * gpost_ref[...]
    y_ref[...] = x_ref[...] + gate_ref[...] * on


def _finish(x2d, o_attn, sa, ga, pb, gate, g_post, w_br_a, w_out, tm, rows_per_gate):
    n = x2d.shape[0]
    row = lambda w: pl.BlockSpec((tm, w), lambda i: (i, 0))
    const = lambda shp: pl.BlockSpec(shp, lambda i: (0,) * len(shp))
    if rows_per_gate == 1:
        gate_spec = row(D_MODEL)
    else:
        tiles_per_gate = rows_per_gate // tm
        gate = gate[:, None, :]
        gate_spec = pl.BlockSpec((None, 1, D_MODEL), lambda i: (i // tiles_per_gate, 0, 0))
    return pl.pallas_call(
        _finish_kernel,
        grid=(n // tm,),
        in_specs=[row(D_MODEL), row(ATTN_WIDTH), row(ATTN_WIDTH), row(D_MODEL), row(D_MODEL), gate_spec,
                  const((1, D_MODEL)), const((ATTN_WIDTH, D_MODEL)), const((D_MODEL, D_MODEL))],
        out_specs=row(D_MODEL),
        out_shape=jax.ShapeDtypeStruct((n, D_MODEL), F32),
        compiler_params=pltpu.CompilerParams(dimension_semantics=("arbitrary",), vmem_limit_bytes=VMEM_LIMIT),
        name="finish",
    )(x2d, o_attn, sa, ga, pb, gate, g_post, w_br_a, w_out)


def _attn_decode_kernel(pt_ref, *refs, n_pages, page_size, wbuf):
    del pt_ref
    page_refs = refs[:n_pages]
    (win_ref, q_ref, kvnew_ref, gates_ref, pelo_ref, pehi_ref, w1lo_ref, w1hi_ref, w2_ref, ovl_ref, expand_ref,
     o_ref, winout_ref, kcmp_ref, vcmp_ref, ksel_ref, vsel_ref, kwin_ref, vwin_ref) = refs[n_pages:]
    past_len = n_pages * page_size
    n_chunk = past_len // CMP_STRIDE
    q8f = q_ref[...]
    q8 = q8f.astype(BF16)

    def both_groups(ref, first_row, n_tokens, rows_per_token):
        return jnp.concatenate([ref[pl.ds(first_row + g, n_tokens, stride=rows_per_token), :]
                                for g in range(KV_HEADS)], axis=1)

    for i, pr in enumerate(page_refs):
        rows = slice(i * page_size, (i + 1) * page_size)
        kcmp_ref[rows, :] = both_groups(pr, 0 * KV_HEADS, page_size, PAGE_ROWS)
        vcmp_ref[rows, :] = both_groups(pr, 1 * KV_HEADS, page_size, PAGE_ROWS)
        ksel_ref[rows, :] = both_groups(pr, 2 * KV_HEADS, page_size, PAGE_ROWS).astype(BF16)
        vsel_ref[rows, :] = both_groups(pr, 3 * KV_HEADS, page_size, PAGE_ROWS).astype(BF16)
    kwin_ref[...] = both_groups(win_ref, 0 * KV_HEADS, wbuf, WIN_ROWS).astype(BF16)
    vwin_ref[...] = both_groups(win_ref, 1 * KV_HEADS, wbuf, WIN_ROWS).astype(BF16)

    kvc = []
    for kind, src_ref in enumerate((kcmp_ref, vcmp_ref)):
        c = jnp.concatenate([src_ref[pl.ds(r, n_chunk, stride=CMP_STRIDE), :] for r in range(CMP_STRIDE)], axis=1)
        lo, hi = _compress_chunks(c, kind, pelo_ref, pehi_ref, w1lo_ref, w1hi_ref)
        hid = lo + pltpu.roll(hi, n_chunk - 1, 0)
        kvc.append(_dot(_silu(hid).astype(BF16), w2_ref[kind]).astype(BF16))
    kc, vc = kvc

    c_end = lax.broadcasted_iota(jnp.int32, (1, n_chunk), 1) * CMP_STRIDE + (CMP_BLOCK - 1)
    e, den = _masked_softmax_parts(_dot_nt(q8, kc), c_end <= past_len)
    p = e * (1.0 / den)
    o_cmp = _dot(p.astype(BF16), vc)

    row = lax.broadcasted_iota(jnp.int32, p.shape, 0)
    in_g0 = row < HEADS_PER_GROUP
    g0 = jnp.sum(jnp.where(in_g0, p, 0.0), axis=0, keepdims=True)
    g1 = jnp.sum(jnp.where(in_g0, 0.0, p), axis=0, keepdims=True)
    hi, lo = _split_bf16(jnp.where(in_g0, g0, g1))
    imp = _dot(hi, ovl_ref[...]) + _dot(lo, ovl_ref[...])
    cur = past_len // SEL_BLOCK
    j = lax.broadcasted_iota(jnp.int32, imp.shape, 1)
    forced = (j == 0) | (j == cur) | (j == cur - 1)
    score = jnp.where(j > cur, -jnp.inf, jnp.where(forced, FORCED_SCORE, imp))
    bias_keys = _dot(_top_k_bias(score).astype(BF16), expand_ref[...])

    s = _dot_nt(q8, ksel_ref[...]) + bias_keys
    s_new = jnp.sum(q8f * kvnew_ref[2:3, :], axis=1, keepdims=True)
    m = jnp.maximum(jnp.max(s, axis=1, keepdims=True), s_new)
    e = jnp.exp(s - m)
    e_new = jnp.exp(s_new - m)
    den = jnp.sum(e, axis=1, keepdims=True) + e_new
    o_sel = (_dot(e.astype(BF16), vsel_ref[...]) + e_new * kvnew_ref[3:4, :]) * (1.0 / den)

    kwpos = past_len - wbuf + lax.broadcasted_iota(jnp.int32, (1, wbuf), 1)
    dt = past_len - kwpos
    valid = (dt >= 0) & (dt < WINDOW) & (kwpos >= 0)
    s = jnp.where(valid, _dot_nt(q8, kwin_ref[...]), -jnp.inf)
    s_new = jnp.sum(q8f * kvnew_ref[4:5, :], axis=1, keepdims=True)
    m = jnp.maximum(jnp.max(s, axis=1, keepdims=True), s_new)
    e = jnp.exp(s - m)
    e_new = jnp.exp(s_new - m)
    den = jnp.sum(e, axis=1, keepdims=True) + e_new
    o_win = (_dot(e.astype(BF16), vwin_ref[...]) + e_new * kvnew_ref[5:6, :]) * (1.0 / den)

    gates = gates_ref[...]
    o = gates[:, 0:1] * o_cmp + gates[:, 1:2] * o_sel + gates[:, 2:3] * o_win
    o_ref[...] = jnp.where(in_g0, o, pltpu.roll(o, HEAD_DIM, 1))

    n_win = WIN_ROWS * wbuf
    winout_ref[0:n_win - WIN_ROWS, :] = win_ref[WIN_ROWS:n_win, :]
    for kind in range(2):
        for g in range(KV_HEADS):
            r = n_win - WIN_ROWS + kind * KV_HEADS + g
            winout_ref[r:r + 1, :] = kvnew_ref[4 + kind:5 + kind, g * HEAD_DIM:(g + 1) * HEAD_DIM]


def _attn_decode(cache, page_table, win, q8, kvnew, gates8, cw, ovl, expand):
    n_phys, page_size = cache.shape[0], cache.shape[1]
    batch, n_pages = page_table.shape
    wbuf = win.shape[1]
    past_len = n_pages * page_size
    cache_rows = cache.reshape(n_phys * page_size * PAGE_ROWS, HEAD_DIM)
    win_rows = win.reshape(batch * wbuf * WIN_ROWS, HEAD_DIM)

    def page_spec(k):
        return pl.BlockSpec((page_size * PAGE_ROWS, HEAD_DIM), lambda b, pt: (pt[b, k], 0))

    per_b = lambda a: pl.BlockSpec((None,) + a.shape[1:], lambda b, pt: (b,) + (0,) * (a.ndim - 1))
    const = lambda a: pl.BlockSpec(a.shape, lambda b, pt: (0,) * a.ndim, pipeline_mode=pl.Buffered(1))
    win_spec = pl.BlockSpec((wbuf * WIN_ROWS, HEAD_DIM), lambda b, pt: (b, 0))
    consts = [cw["pe_lo"], cw["pe_hi"], cw["w1_lo"], cw["w1_hi"], cw["w2"], ovl, expand]
    grid_spec = pltpu.PrefetchScalarGridSpec(
        num_scalar_prefetch=1,
        grid=(batch,),
        in_specs=[page_spec(k) for k in range(n_pages)]
        + [win_spec, per_b(q8), per_b(kvnew), per_b(gates8)]
        + [const(a) for a in consts],
        out_specs=(pl.BlockSpec((None, N_HEADS, LANES), lambda b, pt: (b, 0, 0)), win_spec),
        scratch_shapes=[pltpu.VMEM((past_len, LANES), F32), pltpu.VMEM((past_len, LANES), F32),
                        pltpu.VMEM((past_len, LANES), BF16), pltpu.VMEM((past_len, LANES), BF16),
                        pltpu.VMEM((wbuf, LANES), BF16), pltpu.VMEM((wbuf, LANES), BF16)],
    )
    o8, win_out = pl.pallas_call(
        functools.partial(_attn_decode_kernel, n_pages=n_pages, page_size=page_size, wbuf=wbuf),
        grid_spec=grid_spec,
        out_shape=(jax.ShapeDtypeStruct((batch, N_HEADS, LANES), F32),
                   jax.ShapeDtypeStruct((batch * wbuf * WIN_ROWS, HEAD_DIM), F32)),
        compiler_params=pltpu.CompilerParams(dimension_semantics=("arbitrary",), vmem_limit_bytes=VMEM_LIMIT),
        name="attn_decode",
    )(page_table, *([cache_rows] * n_pages), win_rows, q8, kvnew, gates8, *consts)
    return o8, win_out


def _overlap_matrix(n_cmp_pad, n_cmp, n_selb):
    cs = jnp.arange(n_cmp_pad) * CMP_STRIDE
    ss = jnp.arange(LANES) * SEL_BLOCK
    ov = (cs[:, None] < ss[None, :] + SEL_BLOCK) & (cs[:, None] + CMP_BLOCK > ss[None, :])
    ov = ov & (jnp.arange(n_cmp_pad) < n_cmp)[:, None] & (jnp.arange(LANES) < n_selb)[None, :]
    return ov.astype(BF16)


def _prompt_layer(x, mod, wts):
    batch, seq, _ = x.shape
    x2d = x.reshape(batch * seq, D_MODEL)
    shift, scale, gate = mod[:, 0:D_MODEL], mod[:, D_MODEL:2 * D_MODEL], mod[:, 2 * D_MODEL:]
    tabs = _rope_tables(jnp.arange(seq))
    tm = min(256, seq)
    (qpad, kvrows, kvc, kwin, ksel, kwinb, gates, sa, ga, pb, utail) = _proj_prompt(
        x2d, shift, scale, wts["g_pre"], wts["w_in"], tabs, wts["conv_w"], wts["w_br_b"], batch, seq, tm)
    kvcmp = _compress_prompt(kvc, wts["cmp"], batch, seq)
    n_chunk = seq // CMP_STRIDE
    ovl = _overlap_matrix(n_chunk, n_chunk - 1, -(-seq // SEL_BLOCK))
    o_attn = _attn_prompt(qpad, ksel, kwinb, kvcmp, gates, ovl, batch, seq)
    y = _finish(x2d, o_attn, sa, ga, pb, gate, wts["g_post"], wts["w_br_a"], wts["w_out"], tm, seq)
    n_keep = min(WINDOW, seq)
    return (y.reshape(batch, seq, D_MODEL),
            kvrows.reshape(batch, seq, 4, KV_HEADS, HEAD_DIM),
            kwin.reshape(batch, seq, 2, KV_HEADS, HEAD_DIM)[:, seq - n_keep:],
            utail[:, SUBLANES - (CONV_K - 1):])


def _sample_layer(x, mod, cache, page_table, win, conv_state, wts):
    batch, dec_seq, _ = x.shape
    assert dec_seq == 1
    n_pages, page_size, wbuf = page_table.shape[1], cache.shape[1], win.shape[1]
    past_len = n_pages * page_size
    assert past_len % SEL_BLOCK == 0 and wbuf == WINDOW and past_len // SEL_BLOCK < LANES
    x2d = x.reshape(batch, D_MODEL)
    shift, scale, gate = mod[:, 0:D_MODEL], mod[:, D_MODEL:2 * D_MODEL], mod[:, 2 * D_MODEL:]
    tabs = _rope_tables(jnp.full((1,), past_len, jnp.int32))
    cbuf = conv_state.reshape(batch, (CONV_K - 1) * CONV_WIDTH)
    qpad, kvnew, gates, sa, ga, pb, u = _proj_sample(
        x2d, shift, scale, wts["g_pre"], wts["w_in"], tabs, wts["conv_w"], wts["w_br_b"], cbuf)
    n_gate = N_HEADS * 3
    gates8 = jnp.pad(gates[:, :n_gate].reshape(batch, N_HEADS, 3), ((0, 0), (0, 0), (0, LANES - 3)))
    n_chunk = past_len // CMP_STRIDE
    ovl = _overlap_matrix(n_chunk, n_chunk - 1, past_len // SEL_BLOCK + 1)
    expand = (jnp.arange(LANES)[:, None] == (jnp.arange(past_len) // SEL_BLOCK)[None, :]).astype(BF16)
    o8, win_out = _attn_decode(cache, page_table, win, qpad.reshape(batch, N_HEADS, LANES).astype(F32),
                               kvnew.reshape(batch, 6, LANES), gates8, wts["cmp"], ovl, expand)
    y = _finish(x2d, o8[:, :, :HEAD_DIM].reshape(batch, ATTN_WIDTH), sa, ga, pb, gate, wts["g_post"], wts["w_br_a"],
                wts["w_out"], batch, 1)
    return (y.reshape(batch, 1, D_MODEL),
            kvnew[:, :4 * KV_WIDTH].reshape(batch, 1, 4, KV_HEADS, HEAD_DIM),
            win_out.reshape(batch, wbuf, 2, KV_HEADS, HEAD_DIM),
            jnp.stack([conv_state[:, CONV_K - 2], u], axis=1))


def _prep_weights(w_ada, b_ada, g_pre, g_post, w_in, pe_cmp, w_cmp1, w_cmp2, conv_w, w_br_a, w_br_b, w_out):
    n_unpadded_gate = HEADS_PER_GROUP * KV_HEADS * 3
    w_pad = jnp.concatenate(
        [w_in[:, :C_G + n_unpadded_gate], jnp.zeros((D_MODEL, LANES - n_unpadded_gate), w_in.dtype),
         w_in[:, C_G + n_unpadded_gate:]], axis=1).astype(BF16)
    half = CMP_STRIDE * HEAD_DIM
    return dict(
        w_ada=w_ada, b_ada=b_ada, g_pre=g_pre.reshape(1, -1), g_post=g_post.reshape(1, -1), w_in=w_pad,
        cmp=_compress_weights(pe_cmp, w_cmp1, w_cmp2), conv_w=conv_w,
        w_br_a=w_br_a.astype(BF16), w_br_b=w_br_b.astype(BF16), w_out=w_out.astype(BF16))


def kernel(x_prompt, x_sample, cache_kv_pages, state_win_kv, state_conv, page_table, c_prompt, c_sample, w_ada, b_ada, g_pre, g_post, w_in, pe_cmp, w_cmp1, w_cmp2, conv_w, w_br_a, w_br_b, w_out):
    depth = w_in.shape[0]
    assert depth == 1
    wts = _prep_weights(w_ada[0], b_ada[0], g_pre[0], g_post[0], w_in[0], pe_cmp[0], w_cmp1[0], w_cmp2[0],
                        conv_w[0], w_br_a[0], w_br_b[0], w_out[0])
    n_prompt = c_prompt.shape[0]
    mod = _ada(jnp.concatenate([c_prompt, c_sample], axis=0), wts["w_ada"], wts["b_ada"])
    yp, kvp, wp, cp = _prompt_layer(x_prompt, mod[:n_prompt], wts)
    ys, kvs, ws, cs = _sample_layer(x_sample, mod[n_prompt:], cache_kv_pages[0], page_table, state_win_kv[0],
                                    state_conv[0], wts)
    return (yp, ys, kvp[None], wp[None], cp[None], kvs[None], ws[None], cs[None])
```

```python
import functools

import jax
import jax.numpy as jnp
from jax import lax
from jax.experimental import pallas as pl
from jax.experimental.pallas import tpu as pltpu

F32 = jnp.float32
BF16 = jnp.bfloat16

D_MODEL = 1024
N_HEADS = 8
KV_HEADS = 2
HEADS_PER_GROUP = N_HEADS // KV_HEADS
HEAD_DIM = 64
ROPE_DIM = HEAD_DIM // 4
ROPE_THETA = 500000.0
CMP_BLOCK = 32
CMP_STRIDE = 16
CMP_HIDDEN = 256
SEL_BLOCK = 64
N_SEL = 16
WINDOW = 512
Q_BLOCK = 128
FORCED_SCORE = 1e6
CONV_WIDTH = D_MODEL // 2
CONV_K = 3
ATTN_WIDTH = N_HEADS * HEAD_DIM
KV_WIDTH = KV_HEADS * HEAD_DIM
RMS_EPS = 1e-6

LANES = 128
SUBLANES = 8
VMEM_LIMIT = 56 * 1024 * 1024

C_Q = 0
C_KV = C_Q + ATTN_WIDTH
C_G = C_KV + 6 * KV_WIDTH
C_A = C_G + LANES
C_CB = C_A + ATTN_WIDTH
C_CC = C_CB + CONV_WIDTH
C_CX = C_CC + CONV_WIDTH
C_CG = C_CX + CONV_WIDTH
C_MA = C_CG + CONV_WIDTH
C_MB = C_MA + D_MODEL
IN_PAD = C_MB + D_MODEL

PAGE_ROWS = 4 * KV_HEADS
WIN_ROWS = 2 * KV_HEADS

NEG_BIG = -1e30
KV_TILE = 512
ROW_CHUNK = 256


def _sigmoid(x):
    return 1.0 / (1.0 + jnp.exp(-x))


def _silu(x):
    return x * _sigmoid(x)


def _dot(a, b):
    return jnp.dot(a, b, preferred_element_type=F32)


def _dot_nt(a, b):
    return lax.dot_general(a, b, (((1,), (1,)), ((), ())), preferred_element_type=F32)


def _rope_tables(pos):
    half = ROPE_DIM // 2
    inv = ROPE_THETA ** (-jnp.arange(half, dtype=F32) / half)
    ang = pos.astype(F32)[:, None] * inv[None, :]
    cos, sin = jnp.cos(ang), jnp.sin(ang)
    n = pos.shape[0]
    a = jnp.concatenate([cos, cos, jnp.ones((n, HEAD_DIM - ROPE_DIM), F32)], axis=1)
    p = jnp.concatenate([jnp.zeros((n, half), F32), sin, jnp.zeros((n, HEAD_DIM - ROPE_DIM), F32)], axis=1)
    m = jnp.concatenate([-sin, jnp.zeros((n, HEAD_DIM - half), F32)], axis=1)
    tile = lambda t: jnp.concatenate([t, t], axis=1)
    return tile(a), tile(p), tile(m)


def _rope(x, ra, rp, rm):
    half = ROPE_DIM // 2
    return x * ra + pltpu.roll(x, half, 1) * rp + pltpu.roll(x, LANES - half, 1) * rm


def _ada_kernel(c_ref, w_ref, b_ref, o_ref):
    c = _silu(c_ref[...]).astype(BF16)
    o_ref[...] = _dot(c, w_ref[...].astype(BF16)) + b_ref[...]


def _ada(c_all, w_ada, b_ada):
    n = c_all.shape[0]
    tn = 512
    return pl.pallas_call(
        _ada_kernel,
        grid=(3 * D_MODEL // tn,),
        in_specs=[
            pl.BlockSpec((n, D_MODEL), lambda j: (0, 0)),
            pl.BlockSpec((D_MODEL, tn), lambda j: (0, j)),
            pl.BlockSpec((1, tn), lambda j: (0, j)),
        ],
        out_specs=pl.BlockSpec((n, tn), lambda j: (0, j)),
        out_shape=jax.ShapeDtypeStruct((n, 3 * D_MODEL), F32),
        compiler_params=pltpu.CompilerParams(dimension_semantics=("arbitrary",), vmem_limit_bytes=VMEM_LIMIT),
        name="ada",
    )(c_all, w_ada, b_ada.reshape(1, -1))


def _proj_common(x_ref, shift_ref, scale_ref, gpre_ref, w_ref, ra_ref, rp_ref, rm_ref):
    x = x_ref[...]
    ms = jnp.mean(x * x, axis=-1, keepdims=True)
    xn = x * lax.rsqrt(ms + RMS_EPS) * gpre_ref[...]
    h = xn * (1.0 + scale_ref[...]) + shift_ref[...]
    hb = h.astype(BF16)
    ra, rp, rm = ra_ref[...], rp_ref[...], rm_ref[...]

    def seg(lo, hi):
        return _dot(hb, w_ref[:, lo:hi])

    return seg, (ra, rp, rm)


def _emit_q(seg, rope, qpad_ref):
    zq = seg(C_Q, C_Q + ATTN_WIDTH)
    lane = lax.broadcasted_iota(jnp.int32, (zq.shape[0], LANES), 1)
    lower = lane < HEAD_DIM
    for j in range(ATTN_WIDTH // LANES):
        c = _rope(zq[:, j * LANES:(j + 1) * LANES], *rope) * (HEAD_DIM ** -0.5)
        r = pltpu.roll(c, HEAD_DIM, 1)
        if (2 * j) // HEADS_PER_GROUP == 0:
            even, odd = jnp.where(lower, c, 0.0), jnp.where(lower, r, 0.0)
        else:
            even, odd = jnp.where(lower, 0.0, r), jnp.where(lower, 0.0, c)
        qpad_ref[:, (2 * j) * LANES:(2 * j + 1) * LANES] = even.astype(BF16)
        qpad_ref[:, (2 * j + 1) * LANES:(2 * j + 2) * LANES] = odd.astype(BF16)


def _kv_pieces(seg, rope):
    zkv = seg(C_KV, C_KV + 6 * KV_WIDTH)
    pieces = []
    for p in range(6):
        c = zkv[:, p * LANES:(p + 1) * LANES]
        pieces.append(_rope(c, *rope) if p % 2 == 0 else c)
    return pieces


def _branch_b(seg, um2, um1, u, convw_ref, wbrb_ref):
    cb = seg(C_CB, C_CB + CONV_WIDTH)
    conv = convw_ref[0:1, :] * um2
    conv = conv + convw_ref[1:2, :] * um1
    conv = conv + convw_ref[2:3, :] * u
    ybin = cb * conv * _silu(seg(C_CG, C_CG + CONV_WIDTH))
    yb = _dot(ybin.astype(BF16), wbrb_ref[...])
    gb = _sigmoid(seg(C_MB, C_MB + D_MODEL))
    return gb * yb


def _proj_prompt_kernel(x_ref, shift_ref, scale_ref, gpre_ref, w_ref, ra_ref, rp_ref, rm_ref, convw_ref, wbrb_ref,
                        qpad_ref, kvrows_ref, kvcmp_ref, kwin_ref, ksel_ref, kwinb_ref, gates_ref, sa_ref, ga_ref,
                        pb_ref, utail_ref, carry_ref):
    ti = pl.program_id(1)
    tm = x_ref.shape[0]
    seg, rope = _proj_common(x_ref, shift_ref, scale_ref, gpre_ref, w_ref, ra_ref, rp_ref, rm_ref)
    _emit_q(seg, rope, qpad_ref)

    pieces = _kv_pieces(seg, rope)
    for p in range(4):
        for g in range(KV_HEADS):
            kvrows_ref[pl.ds(p * KV_HEADS + g, tm, stride=PAGE_ROWS), :] = pieces[p][:, g * HEAD_DIM:(g + 1) * HEAD_DIM]
    kvcmp_ref[:, 0:LANES] = pieces[0]
    kvcmp_ref[:, LANES:2 * LANES] = pieces[1]
    kwin_ref[:, 0:LANES] = pieces[4]
    kwin_ref[:, LANES:2 * LANES] = pieces[5]
    kwinb_ref[:, 0:LANES] = pieces[4].astype(BF16)
    kwinb_ref[:, LANES:2 * LANES] = pieces[5].astype(BF16)
    row = ti * tm + lax.broadcasted_iota(jnp.int32, (tm, LANES), 0)
    lane = lax.broadcasted_iota(jnp.int32, (tm, LANES), 1)
    onehot = jnp.where(lane == row // SEL_BLOCK, 1.0, 0.0)
    ksel_ref[:, 0:LANES] = pieces[2].astype(BF16)
    ksel_ref[:, LANES:2 * LANES] = onehot.astype(BF16)
    ksel_ref[:, 2 * LANES:3 * LANES] = pieces[3].astype(BF16)

    gates_ref[...] = _sigmoid(seg(C_G, C_G + LANES))
    sa_ref[...] = _silu(seg(C_A, C_A + ATTN_WIDTH))
    ga_ref[...] = _sigmoid(seg(C_MA, C_MA + D_MODEL))

    @pl.when(ti == 0)
    def _():
        carry_ref[...] = jnp.zeros_like(carry_ref)

    u = seg(C_CC, C_CC + CONV_WIDTH) * seg(C_CX, C_CX + CONV_WIDTH)
    r = lax.broadcasted_iota(jnp.int32, u.shape, 0)
    c7 = carry_ref[SUBLANES - 1:SUBLANES, :]
    c6 = carry_ref[SUBLANES - 2:SUBLANES - 1, :]
    um1 = jnp.where(r == 0, c7, pltpu.roll(u, 1, 0))
    um2 = jnp.where(r == 0, c6, jnp.where(r == 1, c7, pltpu.roll(u, 2, 0)))
    pb_ref[...] = _branch_b(seg, um2, um1, u, convw_ref, wbrb_ref)
    tail = u[tm - SUBLANES:tm, :]
    carry_ref[...] = tail
    utail_ref[0] = tail


def _proj_prompt(x2d, shift, scale, g_pre, w_pad, rope_tabs, conv_w, w_br_b, batch, seq, tm):
    n = batch * seq
    nt = seq // tm
    row = lambda w: pl.BlockSpec((tm, w), lambda b, t: (b * nt + t, 0))
    per_b = lambda w: pl.BlockSpec((None, 1, w), lambda b, t: (b, 0, 0))
    const = lambda shp: pl.BlockSpec(shp, lambda b, t: (0,) * len(shp))
    tab = pl.BlockSpec((tm, LANES), lambda b, t: (t, 0))
    out_shapes = (
        jax.ShapeDtypeStruct((n, N_HEADS * LANES), BF16),
        jax.ShapeDtypeStruct((n * PAGE_ROWS, HEAD_DIM), F32),
        jax.ShapeDtypeStruct((n, 2 * KV_WIDTH), F32),
        jax.ShapeDtypeStruct((n, 2 * KV_WIDTH), F32),
        jax.ShapeDtypeStruct((n, 3 * LANES), BF16),
        jax.ShapeDtypeStruct((n, 2 * KV_WIDTH), BF16),
        jax.ShapeDtypeStruct((n, LANES), F32),
        jax.ShapeDtypeStruct((n, ATTN_WIDTH), F32),
        jax.ShapeDtypeStruct((n, D_MODEL), F32),
        jax.ShapeDtypeStruct((n, D_MODEL), F32),
        jax.ShapeDtypeStruct((batch, SUBLANES, CONV_WIDTH), F32),
    )
    out_specs = (
        row(N_HEADS * LANES), pl.BlockSpec((tm * PAGE_ROWS, HEAD_DIM), lambda b, t: (b * nt + t, 0)),
        row(2 * KV_WIDTH), row(2 * KV_WIDTH), row(3 * LANES), row(2 * KV_WIDTH),
        row(LANES), row(ATTN_WIDTH), row(D_MODEL), row(D_MODEL),
        pl.BlockSpec((1, SUBLANES, CONV_WIDTH), lambda b, t: (b, 0, 0)),
    )
    return pl.pallas_call(
        _proj_prompt_kernel,
        grid=(batch, nt),
        in_specs=[row(D_MODEL), per_b(D_MODEL), per_b(D_MODEL), const((1, D_MODEL)), const((D_MODEL, IN_PAD)),
                  tab, tab, tab, const((CONV_K, CONV_WIDTH)), const((CONV_WIDTH, D_MODEL))],
        out_specs=out_specs,
        out_shape=out_shapes,
        scratch_shapes=[pltpu.VMEM((SUBLANES, CONV_WIDTH), F32)],
        compiler_params=pltpu.CompilerParams(dimension_semantics=("arbitrary", "arbitrary"),
                                             vmem_limit_bytes=VMEM_LIMIT),
        name="proj_prompt",
    )(x2d, shift[:, None, :], scale[:, None, :], g_pre, w_pad, *rope_tabs, conv_w, w_br_b)


def _proj_sample_kernel(x_ref, shift_ref, scale_ref, gpre_ref, w_ref, ra_ref, rp_ref, rm_ref, convw_ref, wbrb_ref,
                        cbuf_ref, qpad_ref, kvnew_ref, gates_ref, sa_ref, ga_ref, pb_ref, u_ref):
    seg, rope = _proj_common(x_ref, shift_ref, scale_ref, gpre_ref, w_ref, ra_ref, rp_ref, rm_ref)
    _emit_q(seg, rope, qpad_ref)
    pieces = _kv_pieces(seg, rope)
    for p in range(6):
        kvnew_ref[:, p * LANES:(p + 1) * LANES] = pieces[p]
    gates_ref[...] = _sigmoid(seg(C_G, C_G + LANES))
    sa_ref[...] = _silu(seg(C_A, C_A + ATTN_WIDTH))
    ga_ref[...] = _sigmoid(seg(C_MA, C_MA + D_MODEL))
    u = seg(C_CC, C_CC + CONV_WIDTH) * seg(C_CX, C_CX + CONV_WIDTH)
    um2 = cbuf_ref[:, 0:CONV_WIDTH]
    um1 = cbuf_ref[:, CONV_WIDTH:2 * CONV_WIDTH]
    pb_ref[...] = _branch_b(seg, um2, um1, u, convw_ref, wbrb_ref)
    u_ref[...] = u


def _proj_sample(x2d, shift, scale, g_pre, w_pad, rope_tabs, conv_w, w_br_b, cbuf):
    n = x2d.shape[0]
    full = lambda shp: pl.BlockSpec(shp, lambda i: (0,) * len(shp))
    out_shapes = (
        jax.ShapeDtypeStruct((n, N_HEADS * LANES), BF16),
        jax.ShapeDtypeStruct((n, 6 * KV_WIDTH), F32),
        jax.ShapeDtypeStruct((n, LANES), F32),
        jax.ShapeDtypeStruct((n, ATTN_WIDTH), F32),
        jax.ShapeDtypeStruct((n, D_MODEL), F32),
        jax.ShapeDtypeStruct((n, D_MODEL), F32),
        jax.ShapeDtypeStruct((n, CONV_WIDTH), F32),
    )
    return pl.pallas_call(
        _proj_sample_kernel,
        grid=(1,),
        in_specs=[full((n, D_MODEL)), full((n, D_MODEL)), full((n, D_MODEL)), full((1, D_MODEL)),
                  full((D_MODEL, IN_PAD)), full((1, LANES)), full((1, LANES)), full((1, LANES)),
                  full((CONV_K, CONV_WIDTH)), full((CONV_WIDTH, D_MODEL)), full((n, 2 * CONV_WIDTH))],
        out_specs=tuple(full(s.shape) for s in out_shapes),
        out_shape=out_shapes,
        compiler_params=pltpu.CompilerParams(dimension_semantics=("arbitrary",), vmem_limit_bytes=VMEM_LIMIT),
        name="proj_sample",
    )(x2d, shift, scale, g_pre, w_pad, *rope_tabs, conv_w, w_br_b, cbuf)


CHUNK_LANES = CMP_STRIDE * KV_WIDTH


def _compress_weights(pe_cmp, w_cmp1, w_cmp2):
    zeros = jnp.zeros((2, CMP_STRIDE, HEAD_DIM, CMP_HIDDEN), w_cmp1.dtype)

    def both_groups(w_half):
        w = w_half.reshape(2, CMP_STRIDE, HEAD_DIM, CMP_HIDDEN)
        g0 = jnp.concatenate([w, zeros], axis=2)
        g1 = jnp.concatenate([zeros, w], axis=2)
        return jnp.concatenate([g0, g1], axis=3).reshape(2, CHUNK_LANES, KV_HEADS * CMP_HIDDEN).astype(BF16)

    half = CMP_STRIDE * HEAD_DIM
    z2 = jnp.zeros_like(w_cmp2)
    w2 = jnp.concatenate([jnp.concatenate([w_cmp2, z2], axis=2), jnp.concatenate([z2, w_cmp2], axis=2)], axis=1)
    pe = jnp.concatenate([pe_cmp, pe_cmp], axis=2)
    return dict(w1_lo=both_groups(w_cmp1[:, :half]), w1_hi=both_groups(w_cmp1[:, half:]), w2=w2.astype(BF16),
                pe_lo=pe[:, :CMP_STRIDE].reshape(2, 1, CHUNK_LANES), pe_hi=pe[:, CMP_STRIDE:].reshape(2, 1, CHUNK_LANES))


def _compress_chunks(c, kind, pelo_ref, pehi_ref, w1lo_ref, w1hi_ref):
    lo = _dot((c + pelo_ref[kind]).astype(BF16), w1lo_ref[kind])
    hi = _dot((c + pehi_ref[kind]).astype(BF16), w1hi_ref[kind])
    return lo, hi


def _compress_prompt_kernel(kc_ref, vc_ref, pelo_ref, pehi_ref, w1lo_ref, w1hi_ref, w2_ref, o_ref):
    n_chunk = kc_ref.shape[0] // CMP_STRIDE
    for kind, src_ref in enumerate((kc_ref, vc_ref)):
        c = jnp.concatenate([src_ref[pl.ds(r, n_chunk, stride=CMP_STRIDE), :] for r in range(CMP_STRIDE)], axis=1)
        lo, hi = _compress_chunks(c, kind, pelo_ref, pehi_ref, w1lo_ref, w1hi_ref)
        hid = lo + pltpu.roll(hi, n_chunk - 1, 0)
        out = _dot(_silu(hid).astype(BF16), w2_ref[kind])
        o_ref[0, :, kind * KV_WIDTH:(kind + 1) * KV_WIDTH] = out.astype(BF16)


def _compress_prompt(kvrows, cw, batch, seq):
    n_chunk = seq // CMP_STRIDE
    const = lambda a: pl.BlockSpec(a.shape, lambda b: (0,) * a.ndim)
    return pl.pallas_call(
        _compress_prompt_kernel,
        grid=(batch,),
        in_specs=[
            pl.BlockSpec((seq, KV_WIDTH), lambda b: (b, 0)),
            pl.BlockSpec((seq, KV_WIDTH), lambda b: (b, 1)),
            const(cw["pe_lo"]), const(cw["pe_hi"]), const(cw["w1_lo"]), const(cw["w1_hi"]), const(cw["w2"]),
        ],
        out_specs=pl.BlockSpec((1, n_chunk, 2 * KV_WIDTH), lambda b: (b, 0, 0)),
        out_shape=jax.ShapeDtypeStruct((batch, n_chunk, 2 * KV_WIDTH), BF16),
        compiler_params=pltpu.CompilerParams(dimension_semantics=("arbitrary",), vmem_limit_bytes=VMEM_LIMIT),
        name="compress_prompt",
    )(kvrows, kvrows, cw["pe_lo"], cw["pe_hi"], cw["w1_lo"], cw["w1_hi"], cw["w2"])


def _top_k_bias(score):
    lane = lax.broadcasted_iota(jnp.int32, score.shape, 1).astype(F32)
    bias = jnp.full(score.shape, NEG_BIG, F32)
    for _ in range(N_SEL):
        m = jnp.max(score, axis=1, keepdims=True)
        idx = jnp.min(jnp.where(score == m, lane, float(LANES)), axis=1, keepdims=True)
        hit = lane == idx
        bias = jnp.where(hit, 0.0, bias)
        score = jnp.where(hit, -jnp.inf, score)
    return bias


def _split_bf16(x):
    hi = x.astype(BF16)
    lo = (x - hi.astype(F32)).astype(BF16)
    return hi, lo


def _masked_softmax_parts(s, valid):
    s = jnp.where(valid, s, -jnp.inf)
    m = jnp.max(s, axis=-1, keepdims=True)
    m = jnp.where(m == -jnp.inf, 0.0, m)
    e = jnp.exp(s - m)
    den = jnp.maximum(jnp.sum(e, axis=-1, keepdims=True), 1e-30)
    return e, den


def _attn_prompt_kernel(qpad_ref, kaug_ref, vsel_ref, kwin_ref, vwin_ref, kvcmp_ref, gates_ref, ovl_ref,
                        o_ref, qaug_ref, m_ref, l_ref, acc_ref, oc_ref, ow_ref):
    qb = pl.program_id(1)
    q0 = qb * Q_BLOCK
    seq = kaug_ref.shape[0]
    n_cmp = kvcmp_ref.shape[1]
    n_rows = N_HEADS * Q_BLOCK
    n_chunks = n_rows // ROW_CHUNK
    heads_per_chunk = ROW_CHUNK // Q_BLOCK

    for n in range(N_HEADS):
        qaug_ref[n * Q_BLOCK:(n + 1) * Q_BLOCK, 0:LANES] = qpad_ref[:, n * LANES:(n + 1) * LANES]

    def qpos_rows(rows):
        r = lax.broadcasted_iota(jnp.int32, (rows, 1), 0)
        return q0 + (r & (Q_BLOCK - 1))

    kc = kvcmp_ref[0, :, 0:LANES]
    vc = kvcmp_ref[0, :, LANES:2 * LANES]
    c_end = lax.broadcasted_iota(jnp.int32, (1, n_cmp), 1) * CMP_STRIDE + (CMP_BLOCK - 1)
    psum = [None] * KV_HEADS
    for rc in range(n_chunks):
        rows = slice(rc * ROW_CHUNK, (rc + 1) * ROW_CHUNK)
        s = _dot_nt(qaug_ref[rows, 0:LANES], kc)
        e, den = _masked_softmax_parts(s, c_end <= qpos_rows(ROW_CHUNK))
        p = e * (1.0 / den)
        oc_ref[rows, :] = _dot(p.astype(BF16), vc)
        g = (rc * heads_per_chunk) // HEADS_PER_GROUP
        part = p[0:Q_BLOCK]
        for h in range(1, heads_per_chunk):
            part = part + p[h * Q_BLOCK:(h + 1) * Q_BLOCK]
        psum[g] = part if psum[g] is None else psum[g] + part

    qpos = q0 + lax.broadcasted_iota(jnp.int32, (Q_BLOCK, 1), 0)
    cur = qpos // SEL_BLOCK
    j = lax.broadcasted_iota(jnp.int32, (Q_BLOCK, LANES), 1)
    forced = (j == 0) | (j == cur) | (j == cur - 1)
    for g in range(KV_HEADS):
        hi, lo = _split_bf16(psum[g])
        imp = _dot(hi, ovl_ref[...]) + _dot(lo, ovl_ref[...])
        score = jnp.where(forced, FORCED_SCORE, jnp.where(j <= cur, imp, -1.0))
        bias = _top_k_bias(score).astype(BF16)
        for h in range(HEADS_PER_GROUP):
            n = g * HEADS_PER_GROUP + h
            qaug_ref[n * Q_BLOCK:(n + 1) * Q_BLOCK, LANES:2 * LANES] = bias

    m_ref[...] = jnp.full(m_ref.shape, NEG_BIG, F32)
    l_ref[...] = jnp.zeros(l_ref.shape, F32)
    acc_ref[...] = jnp.zeros(acc_ref.shape, F32)
    kt_last = (q0 + Q_BLOCK - 1) // KV_TILE

    def sel_tile(kt, causal):
        k0 = pl.multiple_of(kt * KV_TILE, KV_TILE)
        k = kaug_ref[pl.ds(k0, KV_TILE), :]
        v = vsel_ref[pl.ds(k0, KV_TILE), :]
        for rc in range(n_chunks):
            rows = slice(rc * ROW_CHUNK, (rc + 1) * ROW_CHUNK)
            s = _dot_nt(qaug_ref[rows, :], k)
            if causal:
                kpos = k0 + lax.broadcasted_iota(jnp.int32, (1, KV_TILE), 1)
                s = jnp.where(kpos <= qpos_rows(ROW_CHUNK), s, NEG_BIG)
            m_old = m_ref[rows, :]
            m_new = jnp.maximum(m_old, jnp.max(s, axis=-1, keepdims=True))
            alpha = jnp.exp(m_old - m_new)
            p = jnp.exp(s - m_new)
            l_ref[rows, :] = alpha * l_ref[rows, :] + jnp.sum(p, axis=-1, keepdims=True)
            acc_ref[rows, :] = alpha * acc_ref[rows, :] + _dot(p.astype(BF16), v)
            m_ref[rows, :] = m_new

    def body(kt, carry):
        sel_tile(kt, False)
        return carry

    lax.fori_loop(0, kt_last, body, 0)
    sel_tile(kt_last, True)

    wk = kwin_ref.shape[0] if kwin_ref.shape[0] < WINDOW + Q_BLOCK else WINDOW + Q_BLOCK
    start = pl.multiple_of(jnp.maximum(q0 - WINDOW, 0), Q_BLOCK)
    kw = kwin_ref[pl.ds(start, wk), :]
    vw = vwin_ref[pl.ds(start, wk), :]
    kwpos = start + lax.broadcasted_iota(jnp.int32, (1, wk), 1)
    for rc in range(n_chunks):
        rows = slice(rc * ROW_CHUNK, (rc + 1) * ROW_CHUNK)
        s = _dot_nt(qaug_ref[rows, 0:LANES], kw)
        dt = qpos_rows(ROW_CHUNK) - kwpos
        e, den = _masked_softmax_parts(s, (dt >= 0) & (dt < WINDOW))
        ow_ref[rows, :] = _dot(e.astype(BF16), vw) * (1.0 / den)

    lane = lax.broadcasted_iota(jnp.int32, (Q_BLOCK, LANES), 1)
    lower = lane < HEAD_DIM
    gates = gates_ref[...]
    for pair in range(N_HEADS // 2):
        halves = []
        for n in (2 * pair, 2 * pair + 1):
            rows = slice(n * Q_BLOCK, (n + 1) * Q_BLOCK)
            o_sel = acc_ref[rows, :] * (1.0 / l_ref[rows, :])
            o = (gates[:, 3 * n:3 * n + 1] * oc_ref[rows, :] + gates[:, 3 * n + 1:3 * n + 2] * o_sel
                 + gates[:, 3 * n + 2:3 * n + 3] * ow_ref[rows, :])
            halves.append(o)
        g = (2 * pair) // HEADS_PER_GROUP
        if g == 0:
            merged = jnp.where(lower, halves[0], pltpu.roll(halves[1], HEAD_DIM, 1))
        else:
            merged = jnp.where(lower, pltpu.roll(halves[0], HEAD_DIM, 1), halves[1])
        o_ref[:, pair * LANES:(pair + 1) * LANES] = merged


def _attn_prompt(qpad, ksel, kwinb, kvcmp, gates, ovl, batch, seq):
    nq = seq // Q_BLOCK
    n_cmp = kvcmp.shape[1]
    n_rows = N_HEADS * Q_BLOCK
    rowq = lambda w: pl.BlockSpec((Q_BLOCK, w), lambda b, i: (b * nq + i, 0))
    return pl.pallas_call(
        _attn_prompt_kernel,
        grid=(batch, nq),
        in_specs=[
            rowq(N_HEADS * LANES),
            pl.BlockSpec((seq, 2 * LANES), lambda b, i: (b, 0)),
            pl.BlockSpec((seq, LANES), lambda b, i: (b, 2)),
            pl.BlockSpec((seq, LANES), lambda b, i: (b, 0)),
            pl.BlockSpec((seq, LANES), lambda b, i: (b, 1)),
            pl.BlockSpec((1, n_cmp, 2 * LANES), lambda b, i: (b, 0, 0)),
            rowq(LANES),
            pl.BlockSpec((n_cmp, LANES), lambda b, i: (0, 0)),
        ],
        out_specs=rowq(ATTN_WIDTH),
        out_shape=jax.ShapeDtypeStruct((batch * seq, ATTN_WIDTH), F32),
        scratch_shapes=[
            pltpu.VMEM((n_rows, 2 * LANES), BF16),
            pltpu.VMEM((n_rows, 1), F32),
            pltpu.VMEM((n_rows, 1), F32),
            pltpu.VMEM((n_rows, LANES), F32),
            pltpu.VMEM((n_rows, LANES), F32),
            pltpu.VMEM((n_rows, LANES), F32),
        ],
        compiler_params=pltpu.CompilerParams(dimension_semantics=("arbitrary", "arbitrary"),
                                             vmem_limit_bytes=VMEM_LIMIT),
        name="attn_prompt",
    )(qpad, ksel, ksel, kwinb, kwinb, kvcmp, gates, ovl)


def _finish_kernel(x_ref, o_ref, sa_ref, ga_ref, pb_ref, gate_ref, gpost_ref, wbra_ref, wout_ref, y_ref):
    ya = _dot((o_ref[...] * sa_ref[...]).astype(BF16), wbra_ref[...])
    mix = ga_ref[...] * ya + pb_ref[...]
    o = _dot(mix.astype(BF16), wout_ref[...])
    ms = jnp.mean(o * o, axis=-1, keepdims=True)
    on = o * lax.rsqrt(ms + RMS_EPS) * gpost_ref[...]
    y_ref[...] = x_ref[...] + gate_ref[...] * on


def _finish(x2d, o_attn, sa, ga, pb, gate, g_post, w_br_a, w_out, tm, rows_per_gate):
    n = x2d.shape[0]
    row = lambda w: pl.BlockSpec((tm, w), lambda i: (i, 0))
    const = lambda shp: pl.BlockSpec(shp, lambda i: (0,) * len(shp))
    if rows_per_gate == 1:
        gate_spec = row(D_MODEL)
    else:
        tiles_per_gate = rows_per_gate // tm
        gate = gate[:, None, :]
        gate_spec = pl.BlockSpec((None, 1, D_MODEL), lambda i: (i // tiles_per_gate, 0, 0))
    return pl.pallas_call(
        _finish_kernel,
        grid=(n // tm,),
        in_specs=[row(D_MODEL), row(ATTN_WIDTH), row(ATTN_WIDTH), row(D_MODEL), row(D_MODEL), gate_spec,
                  const((1, D_MODEL)), const((ATTN_WIDTH, D_MODEL)), const((D_MODEL, D_MODEL))],
        out_specs=row(D_MODEL),
        out_shape=jax.ShapeDtypeStruct((n, D_MODEL), F32),
        compiler_params=pltpu.CompilerParams(dimension_semantics=("arbitrary",), vmem_limit_bytes=VMEM_LIMIT),
        name="finish",
    )(x2d, o_attn, sa, ga, pb, gate, g_post, w_br_a, w_out)


def _attn_decode_kernel(pt_ref, *refs, n_pages, page_size, wbuf):
    del pt_ref
    page_refs = refs[:n_pages]
    (win_ref, q_ref, kvnew_ref, gates_ref, pelo_ref, pehi_ref, w1lo_ref, w1hi_ref, w2_ref, ovl_ref, expand_ref,
     o_ref, winout_ref, kcmp_ref, vcmp_ref) = refs[n_pages:]
    past_len = n_pages * page_size
    n_chunk = past_len // CMP_STRIDE
    q8f = q_ref[...]
    q8 = q8f.astype(BF16)

    def stream(ref, which):
        slab = ref[which * KV_HEADS:(which + 1) * KV_HEADS]
        return slab.reshape(KV_WIDTH, slab.shape[-1])

    for i, pr in enumerate(page_refs):
        rows = slice(i * page_size, (i + 1) * page_size)
        kcmp_ref[rows, :] = stream(pr, 0).T
        vcmp_ref[rows, :] = stream(pr, 1).T

    kvc = []
    for kind, src_ref in enumerate((kcmp_ref, vcmp_ref)):
        c = jnp.concatenate([src_ref[pl.ds(r, n_chunk, stride=CMP_STRIDE), :] for r in range(CMP_STRIDE)], axis=1)
        lo, hi = _compress_chunks(c, kind, pelo_ref, pehi_ref, w1lo_ref, w1hi_ref)
        hid = lo + pltpu.roll(hi, n_chunk - 1, 0)
        kvc.append(_dot(_silu(hid).astype(BF16), w2_ref[kind]).astype(BF16))
    kc, vc = kvc

    c_end = lax.broadcasted_iota(jnp.int32, (1, n_chunk), 1) * CMP_STRIDE + (CMP_BLOCK - 1)
    e, den = _masked_softmax_parts(_dot_nt(q8, kc), c_end <= past_len)
    p = e * (1.0 / den)
    o_cmp = _dot(p.astype(BF16), vc)

    row = lax.broadcasted_iota(jnp.int32, p.shape, 0)
    in_g0 = row < HEADS_PER_GROUP
    g0 = jnp.sum(jnp.where(in_g0, p, 0.0), axis=0, keepdims=True)
    g1 = jnp.sum(jnp.where(in_g0, 0.0, p), axis=0, keepdims=True)
    hi, lo = _split_bf16(jnp.where(in_g0, g0, g1))
    imp = _dot(hi, ovl_ref[...]) + _dot(lo, ovl_ref[...])
    cur = past_len // SEL_BLOCK
    j = lax.broadcasted_iota(jnp.int32, imp.shape, 1)
    forced = (j == 0) | (j == cur) | (j == cur - 1)
    score = jnp.where(j > cur, -jnp.inf, jnp.where(forced, FORCED_SCORE, imp))
    bias_keys = _dot(_top_k_bias(score).astype(BF16), expand_ref[...])

    s = jnp.concatenate([_dot(q8, stream(pr, 2).astype(BF16)) for pr in page_refs], axis=1) + bias_keys
    s_new = jnp.sum(q8f * kvnew_ref[2:3, :], axis=1, keepdims=True)
    m = jnp.maximum(jnp.max(s, axis=1, keepdims=True), s_new)
    e = jnp.exp(s - m)
    e_new = jnp.exp(s_new - m)
    den = jnp.sum(e, axis=1, keepdims=True) + e_new
    acc = e_new * kvnew_ref[3:4, :]
    for i, pr in enumerate(page_refs):
        acc = acc + _dot_nt(e[:, i * page_size:(i + 1) * page_size].astype(BF16), stream(pr, 3).astype(BF16))
    o_sel = acc * (1.0 / den)

    kwpos = past_len - wbuf + lax.broadcasted_iota(jnp.int32, (1, wbuf), 1)
    dt = past_len - kwpos
    valid = (dt >= 0) & (dt < WINDOW) & (kwpos >= 0)
    kw_t = stream(win_ref, 0)
    vw_t = stream(win_ref, 1)
    s = jnp.where(valid, _dot(q8, kw_t.astype(BF16)), -jnp.inf)
    s_new = jnp.sum(q8f * kvnew_ref[4:5, :], axis=1, keepdims=True)
    m = jnp.maximum(jnp.max(s, axis=1, keepdims=True), s_new)
    e = jnp.exp(s - m)
    e_new = jnp.exp(s_new - m)
    den = jnp.sum(e, axis=1, keepdims=True) + e_new
    o_win = (_dot_nt(e.astype(BF16), vw_t.astype(BF16)) + e_new * kvnew_ref[5:6, :]) * (1.0 / den)

    gates = gates_ref[...]
    o = gates[:, 0:1] * o_cmp + gates[:, 1:2] * o_sel + gates[:, 2:3] * o_win
    o_ref[...] = jnp.where(in_g0, o, pltpu.roll(o, HEAD_DIM, 1))

    new_cols = jnp.concatenate([kvnew_ref[...], jnp.zeros((SUBLANES - 6, LANES), F32)], axis=0).T
    lane = lax.broadcasted_iota(jnp.int32, (KV_WIDTH, wbuf), 1)
    for kind, old in enumerate((kw_t, vw_t)):
        shifted = jnp.where(lane == wbuf - 1, new_cols[:, 4 + kind:5 + kind], pltpu.roll(old, wbuf - 1, 1))
        winout_ref[kind * KV_HEADS:(kind + 1) * KV_HEADS] = shifted.reshape(KV_HEADS, HEAD_DIM, wbuf)


def _attn_decode(cache_t, page_table, win_t, q8, kvnew, gates8, cw, ovl, expand):
    page_size = cache_t.shape[-1]
    batch, n_pages = page_table.shape
    wbuf = win_t.shape[-1]
    past_len = n_pages * page_size

    def page_spec(k):
        return pl.BlockSpec((None, PAGE_ROWS, HEAD_DIM, page_size), lambda b, pt: (pt[b, k], 0, 0, 0))

    per_b = lambda a: pl.BlockSpec((None,) + a.shape[1:], lambda b, pt: (b,) + (0,) * (a.ndim - 1))
    const = lambda a: pl.BlockSpec(a.shape, lambda b, pt: (0,) * a.ndim, pipeline_mode=pl.Buffered(1))
    consts = [cw["pe_lo"], cw["pe_hi"], cw["w1_lo"], cw["w1_hi"], cw["w2"], ovl, expand]
    grid_spec = pltpu.PrefetchScalarGridSpec(
        num_scalar_prefetch=1,
        grid=(batch,),
        in_specs=[page_spec(k) for k in range(n_pages)]
        + [per_b(win_t), per_b(q8), per_b(kvnew), per_b(gates8)]
        + [const(a) for a in consts],
        out_specs=(pl.BlockSpec((None, N_HEADS, LANES), lambda b, pt: (b, 0, 0)), per_b(win_t)),
        scratch_shapes=[pltpu.VMEM((past_len, LANES), F32), pltpu.VMEM((past_len, LANES), F32)],
    )
    o8, win_out = pl.pallas_call(
        functools.partial(_attn_decode_kernel, n_pages=n_pages, page_size=page_size, wbuf=wbuf),
        grid_spec=grid_spec,
        out_shape=(jax.ShapeDtypeStruct((batch, N_HEADS, LANES), F32),
                   jax.ShapeDtypeStruct(win_t.shape, F32)),
        compiler_params=pltpu.CompilerParams(dimension_semantics=("arbitrary",), vmem_limit_bytes=VMEM_LIMIT),
        name="attn_decode",
    )(page_table, *([cache_t] * n_pages), win_t, q8, kvnew, gates8, *consts)
    return o8, win_out


def _overlap_matrix(n_cmp_pad, n_cmp, n_selb):
    cs = jnp.arange(n_cmp_pad) * CMP_STRIDE
    ss = jnp.arange(LANES) * SEL_BLOCK
    ov = (cs[:, None] < ss[None, :] + SEL_BLOCK) & (cs[:, None] + CMP_BLOCK > ss[None, :])
    ov = ov & (jnp.arange(n_cmp_pad) < n_cmp)[:, None] & (jnp.arange(LANES) < n_selb)[None, :]
    return ov.astype(BF16)


def _prompt_layer(x, mod, wts):
    batch, seq, _ = x.shape
    x2d = x.reshape(batch * seq, D_MODEL)
    shift, scale, gate = mod[:, 0:D_MODEL], mod[:, D_MODEL:2 * D_MODEL], mod[:, 2 * D_MODEL:]
    tabs = _rope_tables(jnp.arange(seq))
    tm = min(256, seq)
    (qpad, kvrows, kvc, kwin, ksel, kwinb, gates, sa, ga, pb, utail) = _proj_prompt(
        x2d, shift, scale, wts["g_pre"], wts["w_in"], tabs, wts["conv_w"], wts["w_br_b"], batch, seq, tm)
    kvcmp = _compress_prompt(kvc, wts["cmp"], batch, seq)
    n_chunk = seq // CMP_STRIDE
    ovl = _overlap_matrix(n_chunk, n_chunk - 1, -(-seq // SEL_BLOCK))
    o_attn = _attn_prompt(qpad, ksel, kwinb, kvcmp, gates, ovl, batch, seq)
    y = _finish(x2d, o_attn, sa, ga, pb, gate, wts["g_post"], wts["w_br_a"], wts["w_out"], tm, seq)
    n_keep = min(WINDOW, seq)
    return (y.reshape(batch, seq, D_MODEL),
            kvrows.reshape(batch, seq, 4, KV_HEADS, HEAD_DIM),
            kwin.reshape(batch, seq, 2, KV_HEADS, HEAD_DIM)[:, seq - n_keep:],
            utail[:, SUBLANES - (CONV_K - 1):])


def _sample_layer(x, mod, cache, page_table, win, conv_state, wts):
    batch, dec_seq, _ = x.shape
    assert dec_seq == 1
    n_pages, page_size, wbuf = page_table.shape[1], cache.shape[1], win.shape[1]
    past_len = n_pages * page_size
    assert past_len % SEL_BLOCK == 0 and wbuf == WINDOW and past_len // SEL_BLOCK < LANES
    x2d = x.reshape(batch, D_MODEL)
    shift, scale, gate = mod[:, 0:D_MODEL], mod[:, D_MODEL:2 * D_MODEL], mod[:, 2 * D_MODEL:]
    tabs = _rope_tables(jnp.full((1,), past_len, jnp.int32))
    cbuf = conv_state.reshape(batch, (CONV_K - 1) * CONV_WIDTH)
    qpad, kvnew, gates, sa, ga, pb, u = _proj_sample(
        x2d, shift, scale, wts["g_pre"], wts["w_in"], tabs, wts["conv_w"], wts["w_br_b"], cbuf)
    n_gate = N_HEADS * 3
    gates8 = jnp.pad(gates[:, :n_gate].reshape(batch, N_HEADS, 3), ((0, 0), (0, 0), (0, LANES - 3)))
    n_chunk = past_len // CMP_STRIDE
    ovl = _overlap_matrix(n_chunk, n_chunk - 1, past_len // SEL_BLOCK + 1)
    expand = (jnp.arange(LANES)[:, None] == (jnp.arange(past_len) // SEL_BLOCK)[None, :]).astype(BF16)
    cache_t = jnp.transpose(cache, (0, 2, 3, 4, 1)).reshape(cache.shape[0], PAGE_ROWS, HEAD_DIM, page_size)
    win_t = jnp.transpose(win, (0, 2, 3, 4, 1)).reshape(batch, WIN_ROWS, HEAD_DIM, wbuf)
    o8, win_out = _attn_decode(cache_t, page_table, win_t, qpad.reshape(batch, N_HEADS, LANES).astype(F32),
                               kvnew.reshape(batch, 6, LANES), gates8, wts["cmp"], ovl, expand)
    win_out = jnp.transpose(win_out.reshape(batch, 2, KV_HEADS, HEAD_DIM, wbuf), (0, 4, 1, 2, 3))
    y = _finish(x2d, o8[:, :, :HEAD_DIM].reshape(batch, ATTN_WIDTH), sa, ga, pb, gate, wts["g_post"], wts["w_br_a"],
                wts["w_out"], batch, 1)
    return (y.reshape(batch, 1, D_MODEL),
            kvnew[:, :4 * KV_WIDTH].reshape(batch, 1, 4, KV_HEADS, HEAD_DIM),
            win_out,
            jnp.stack([conv_state[:, CONV_K - 2], u], axis=1))


def _prep_weights(w_ada, b_ada, g_pre, g_post, w_in, pe_cmp, w_cmp1, w_cmp2, conv_w, w_br_a, w_br_b, w_out):
    n_unpadded_gate = HEADS_PER_GROUP * KV_HEADS * 3
    w_pad = jnp.concatenate(
        [w_in[:, :C_G + n_unpadded_gate], jnp.zeros((D_MODEL, LANES - n_unpadded_gate), w_in.dtype),
         w_in[:, C_G + n_unpadded_gate:]], axis=1).astype(BF16)
    half = CMP_STRIDE * HEAD_DIM
    return dict(
        w_ada=w_ada, b_ada=b_ada, g_pre=g_pre.reshape(1, -1), g_post=g_post.reshape(1, -1), w_in=w_pad,
        cmp=_compress_weights(pe_cmp, w_cmp1, w_cmp2), conv_w=conv_w,
        w_br_a=w_br_a.astype(BF16), w_br_b=w_br_b.astype(BF16), w_out=w_out.astype(BF16))


def kernel(x_prompt, x_sample, cache_kv_pages, state_win_kv, state_conv, page_table, c_prompt, c_sample, w_ada, b_ada, g_pre, g_post, w_in, pe_cmp, w_cmp1, w_cmp2, conv_w, w_br_a, w_br_b, w_out):
    depth = w_in.shape[0]
    assert depth == 1
    wts = _prep_weights(w_ada[0], b_ada[0], g_pre[0], g_post[0], w_in[0], pe_cmp[0], w_cmp1[0], w_cmp2[0],
                        conv_w[0], w_br_a[0], w_br_b[0], w_out[0])
    n_prompt = c_prompt.shape[0]
    mod = _ada(jnp.concatenate([c_prompt, c_sample], axis=0), wts["w_ada"], wts["b_ada"])
    yp, kvp, wp, cp = _prompt_layer(x_prompt, mod[:n_prompt], wts)
    ys, kvs, ws, cs = _sample_layer(x_sample, mod[n_prompt:], cache_kv_pages[0], page_table, state_win_kv[0],
                                    state_conv[0], wts)
    return (yp, ys, kvp[None], wp[None], cp[None], kvs[None], ws[None], cs[None])
```

```python
import functools

import jax
import jax.numpy as jnp
from jax import lax
from jax.experimental import pallas as pl
from jax.experimental.pallas import tpu as pltpu

F32 = jnp.float32
BF16 = jnp.bfloat16

D_MODEL = 1024
N_HEADS = 8
KV_HEADS = 2
HEADS_PER_GROUP = N_HEADS // KV_HEADS
HEAD_DIM = 64
ROPE_DIM = HEAD_DIM // 4
ROPE_THETA = 500000.0
CMP_BLOCK = 32
CMP_STRIDE = 16
CMP_HIDDEN = 256
SEL_BLOCK = 64
N_SEL = 16
WINDOW = 512
Q_BLOCK = 128
FORCED_SCORE = 1e6
CONV_WIDTH = D_MODEL // 2
CONV_K = 3
ATTN_WIDTH = N_HEADS * HEAD_DIM
KV_WIDTH = KV_HEADS * HEAD_DIM
RMS_EPS = 1e-6

LANES = 128
SUBLANES = 8
VMEM_LIMIT = 56 * 1024 * 1024

C_Q = 0
C_KV = C_Q + ATTN_WIDTH
C_G = C_KV + 6 * KV_WIDTH
C_A = C_G + LANES
C_CB = C_A + ATTN_WIDTH
C_CC = C_CB + CONV_WIDTH
C_CX = C_CC + CONV_WIDTH
C_CG = C_CX + CONV_WIDTH
C_MA = C_CG + CONV_WIDTH
C_MB = C_MA + D_MODEL
IN_PAD = C_MB + D_MODEL

PAGE_ROWS = 4 * KV_HEADS
WIN_ROWS = 2 * KV_HEADS

GATE_ROWS = 32
LOG2_E = 1.4426950408889634

NEG_BIG = -1e30
KV_TILE = 512
ROW_CHUNK = 256


def _sigmoid(x):
    return 1.0 / (1.0 + jnp.exp(-x))


def _silu(x):
    return x * _sigmoid(x)


def _dot(a, b):
    return jnp.dot(a, b, preferred_element_type=F32)


def _dot_nt(a, b):
    return lax.dot_general(a, b, (((1,), (1,)), ((), ())), preferred_element_type=F32)


def _rope_tables(pos):
    half = ROPE_DIM // 2
    inv = ROPE_THETA ** (-jnp.arange(half, dtype=F32) / half)
    ang = pos.astype(F32)[:, None] * inv[None, :]
    cos, sin = jnp.cos(ang), jnp.sin(ang)
    n = pos.shape[0]
    a = jnp.concatenate([cos, cos, jnp.ones((n, HEAD_DIM - ROPE_DIM), F32)], axis=1)
    p = jnp.concatenate([jnp.zeros((n, half), F32), sin, jnp.zeros((n, HEAD_DIM - ROPE_DIM), F32)], axis=1)
    m = jnp.concatenate([-sin, jnp.zeros((n, HEAD_DIM - half), F32)], axis=1)
    tile = lambda t: jnp.concatenate([t, t], axis=1)
    return tile(a), tile(p), tile(m)


def _rope(x, ra, rp, rm):
    half = ROPE_DIM // 2
    return x * ra + pltpu.roll(x, half, 1) * rp + pltpu.roll(x, LANES - half, 1) * rm


def _ada_kernel(c_ref, w_ref, b_ref, o_ref):
    c = _silu(c_ref[...]).astype(BF16)
    o_ref[...] = _dot(c, w_ref[...].astype(BF16)) + b_ref[...]


def _ada(c_all, w_ada, b_ada):
    n = c_all.shape[0]
    tn = 512
    return pl.pallas_call(
        _ada_kernel,
        grid=(3 * D_MODEL // tn,),
        in_specs=[
            pl.BlockSpec((n, D_MODEL), lambda j: (0, 0)),
            pl.BlockSpec((D_MODEL, tn), lambda j: (0, j)),
            pl.BlockSpec((1, tn), lambda j: (0, j)),
        ],
        out_specs=pl.BlockSpec((n, tn), lambda j: (0, j)),
        out_shape=jax.ShapeDtypeStruct((n, 3 * D_MODEL), F32),
        compiler_params=pltpu.CompilerParams(dimension_semantics=("arbitrary",), vmem_limit_bytes=VMEM_LIMIT),
        name="ada",
    )(c_all, w_ada, b_ada.reshape(1, -1))


def _proj_common(x_ref, shift_ref, scale_ref, gpre_ref, w_ref, ra_ref, rp_ref, rm_ref):
    x = x_ref[...]
    ms = jnp.mean(x * x, axis=-1, keepdims=True)
    xn = x * lax.rsqrt(ms + RMS_EPS) * gpre_ref[...]
    h = xn * (1.0 + scale_ref[...]) + shift_ref[...]
    hb = h.astype(BF16)
    ra, rp, rm = ra_ref[...], rp_ref[...], rm_ref[...]

    def seg(lo, hi):
        return _dot(hb, w_ref[:, lo:hi])

    return seg, (ra, rp, rm)


def _padded_q_heads(seg, rope, scale):
    zq = seg(C_Q, C_Q + ATTN_WIDTH)
    lane = lax.broadcasted_iota(jnp.int32, (zq.shape[0], LANES), 1)
    lower = lane < HEAD_DIM
    heads = []
    for j in range(ATTN_WIDTH // LANES):
        c = _rope(zq[:, j * LANES:(j + 1) * LANES], *rope) * scale
        r = pltpu.roll(c, HEAD_DIM, 1)
        if (2 * j) // HEADS_PER_GROUP == 0:
            heads += [jnp.where(lower, c, 0.0), jnp.where(lower, r, 0.0)]
        else:
            heads += [jnp.where(lower, 0.0, r), jnp.where(lower, 0.0, c)]
    return heads


def _kv_pieces(seg, rope):
    zkv = seg(C_KV, C_KV + 6 * KV_WIDTH)
    pieces = []
    for p in range(6):
        c = zkv[:, p * LANES:(p + 1) * LANES]
        pieces.append(_rope(c, *rope) if p % 2 == 0 else c)
    return pieces


def _branch_b(seg, um2, um1, u, convw_ref, wbrb_ref):
    cb = seg(C_CB, C_CB + CONV_WIDTH)
    conv = convw_ref[0:1, :] * um2
    conv = conv + convw_ref[1:2, :] * um1
    conv = conv + convw_ref[2:3, :] * u
    ybin = cb * conv * _silu(seg(C_CG, C_CG + CONV_WIDTH))
    yb = _dot(ybin.astype(BF16), wbrb_ref[...])
    gb = _sigmoid(seg(C_MB, C_MB + D_MODEL))
    return gb * yb


def _proj_prompt_kernel(x_ref, shift_ref, scale_ref, gpre_ref, w_ref, ra_ref, rp_ref, rm_ref, convw_ref, wbrb_ref,
                        qt_ref, kvt_ref, kvcmp_ref, kwin_ref, ksel_ref, kwinb_ref, vselt_ref, vwint_ref, gatest_ref,
                        sa_ref, ga_ref, pb_ref, utail_ref, carry_ref):
    ti = pl.program_id(1)
    tm = x_ref.shape[0]
    seg, rope = _proj_common(x_ref, shift_ref, scale_ref, gpre_ref, w_ref, ra_ref, rp_ref, rm_ref)
    for n, head in enumerate(_padded_q_heads(seg, rope, HEAD_DIM ** -0.5 * LOG2_E)):
        qt_ref[n] = head.T.astype(BF16)

    pieces = _kv_pieces(seg, rope)
    pieces_t = [p.T for p in pieces]
    for p in range(4):
        kvt_ref[p * KV_WIDTH:(p + 1) * KV_WIDTH, :] = pieces_t[p]
    kvcmp_ref[:, 0:LANES] = pieces[0]
    kvcmp_ref[:, LANES:2 * LANES] = pieces[1]
    kwin_ref[:, 0:LANES] = pieces[4]
    kwin_ref[:, LANES:2 * LANES] = pieces[5]
    kwinb_ref[...] = pieces[4].astype(BF16)
    vselt_ref[...] = pieces_t[3].astype(BF16)
    vwint_ref[...] = pieces_t[5].astype(BF16)
    row = ti * tm + lax.broadcasted_iota(jnp.int32, (tm, LANES), 0)
    lane = lax.broadcasted_iota(jnp.int32, (tm, LANES), 1)
    onehot = jnp.where(lane == row // SEL_BLOCK, 1.0, 0.0)
    ksel_ref[:, 0:LANES] = pieces[2].astype(BF16)
    ksel_ref[:, LANES:2 * LANES] = onehot.astype(BF16)

    gatest_ref[...] = _sigmoid(seg(C_G, C_G + LANES)).T[0:GATE_ROWS, :]
    sa_ref[...] = _silu(seg(C_A, C_A + ATTN_WIDTH))
    ga_ref[...] = _sigmoid(seg(C_MA, C_MA + D_MODEL))

    @pl.when(ti == 0)
    def _():
        carry_ref[...] = jnp.zeros_like(carry_ref)

    u = seg(C_CC, C_CC + CONV_WIDTH) * seg(C_CX, C_CX + CONV_WIDTH)
    r = lax.broadcasted_iota(jnp.int32, u.shape, 0)
    c7 = carry_ref[SUBLANES - 1:SUBLANES, :]
    c6 = carry_ref[SUBLANES - 2:SUBLANES - 1, :]
    um1 = jnp.where(r == 0, c7, pltpu.roll(u, 1, 0))
    um2 = jnp.where(r == 0, c6, jnp.where(r == 1, c7, pltpu.roll(u, 2, 0)))
    pb_ref[...] = _branch_b(seg, um2, um1, u, convw_ref, wbrb_ref)
    tail = u[tm - SUBLANES:tm, :]
    carry_ref[...] = tail
    utail_ref[0] = tail


def _proj_prompt(x2d, shift, scale, g_pre, w_pad, rope_tabs, conv_w, w_br_b, batch, seq, tm):
    n = batch * seq
    nt = seq // tm
    row = lambda w: pl.BlockSpec((tm, w), lambda b, t: (b * nt + t, 0))
    per_b = lambda w: pl.BlockSpec((None, 1, w), lambda b, t: (b, 0, 0))
    const = lambda shp: pl.BlockSpec(shp, lambda b, t: (0,) * len(shp))
    tab = pl.BlockSpec((tm, LANES), lambda b, t: (t, 0))
    feat = lambda rows: pl.BlockSpec((None, rows, tm), lambda b, t: (b, 0, t))
    out_shapes = (
        jax.ShapeDtypeStruct((batch, N_HEADS, LANES, seq), BF16),
        jax.ShapeDtypeStruct((batch, 4 * KV_WIDTH, seq), F32),
        jax.ShapeDtypeStruct((n, 2 * KV_WIDTH), F32),
        jax.ShapeDtypeStruct((n, 2 * KV_WIDTH), F32),
        jax.ShapeDtypeStruct((n, 2 * LANES), BF16),
        jax.ShapeDtypeStruct((n, KV_WIDTH), BF16),
        jax.ShapeDtypeStruct((batch, KV_WIDTH, seq), BF16),
        jax.ShapeDtypeStruct((batch, KV_WIDTH, seq), BF16),
        jax.ShapeDtypeStruct((batch, GATE_ROWS, seq), F32),
        jax.ShapeDtypeStruct((n, ATTN_WIDTH), F32),
        jax.ShapeDtypeStruct((n, D_MODEL), F32),
        jax.ShapeDtypeStruct((n, D_MODEL), F32),
        jax.ShapeDtypeStruct((batch, SUBLANES, CONV_WIDTH), F32),
    )
    out_specs = (
        pl.BlockSpec((None, N_HEADS, LANES, tm), lambda b, t: (b, 0, 0, t)), feat(4 * KV_WIDTH),
        row(2 * KV_WIDTH), row(2 * KV_WIDTH), row(2 * LANES), row(KV_WIDTH), feat(KV_WIDTH), feat(KV_WIDTH),
        feat(GATE_ROWS), row(ATTN_WIDTH), row(D_MODEL), row(D_MODEL),
        pl.BlockSpec((1, SUBLANES, CONV_WIDTH), lambda b, t: (b, 0, 0)),
    )
    return pl.pallas_call(
        _proj_prompt_kernel,
        grid=(batch, nt),
        in_specs=[row(D_MODEL), per_b(D_MODEL), per_b(D_MODEL), const((1, D_MODEL)), const((D_MODEL, IN_PAD)),
                  tab, tab, tab, const((CONV_K, CONV_WIDTH)), const((CONV_WIDTH, D_MODEL))],
        out_specs=out_specs,
        out_shape=out_shapes,
        scratch_shapes=[pltpu.VMEM((SUBLANES, CONV_WIDTH), F32)],
        compiler_params=pltpu.CompilerParams(dimension_semantics=("arbitrary", "arbitrary"),
                                             vmem_limit_bytes=VMEM_LIMIT),
        name="proj_prompt",
    )(x2d, shift[:, None, :], scale[:, None, :], g_pre, w_pad, *rope_tabs, conv_w, w_br_b)


def _proj_sample_kernel(x_ref, shift_ref, scale_ref, gpre_ref, w_ref, ra_ref, rp_ref, rm_ref, convw_ref, wbrb_ref,
                        cbuf_ref, qpad_ref, kvnew_ref, gates_ref, sa_ref, ga_ref, pb_ref, u_ref):
    seg, rope = _proj_common(x_ref, shift_ref, scale_ref, gpre_ref, w_ref, ra_ref, rp_ref, rm_ref)
    for n, head in enumerate(_padded_q_heads(seg, rope, HEAD_DIM ** -0.5)):
        qpad_ref[:, n * LANES:(n + 1) * LANES] = head.astype(BF16)
    pieces = _kv_pieces(seg, rope)
    for p in range(6):
        kvnew_ref[:, p * LANES:(p + 1) * LANES] = pieces[p]
    gates_ref[...] = _sigmoid(seg(C_G, C_G + LANES))
    sa_ref[...] = _silu(seg(C_A, C_A + ATTN_WIDTH))
    ga_ref[...] = _sigmoid(seg(C_MA, C_MA + D_MODEL))
    u = seg(C_CC, C_CC + CONV_WIDTH) * seg(C_CX, C_CX + CONV_WIDTH)
    um2 = cbuf_ref[:, 0:CONV_WIDTH]
    um1 = cbuf_ref[:, CONV_WIDTH:2 * CONV_WIDTH]
    pb_ref[...] = _branch_b(seg, um2, um1, u, convw_ref, wbrb_ref)
    u_ref[...] = u


def _proj_sample(x2d, shift, scale, g_pre, w_pad, rope_tabs, conv_w, w_br_b, cbuf):
    n = x2d.shape[0]
    full = lambda shp: pl.BlockSpec(shp, lambda i: (0,) * len(shp))
    out_shapes = (
        jax.ShapeDtypeStruct((n, N_HEADS * LANES), BF16),
        jax.ShapeDtypeStruct((n, 6 * KV_WIDTH), F32),
        jax.ShapeDtypeStruct((n, LANES), F32),
        jax.ShapeDtypeStruct((n, ATTN_WIDTH), F32),
        jax.ShapeDtypeStruct((n, D_MODEL), F32),
        jax.ShapeDtypeStruct((n, D_MODEL), F32),
        jax.ShapeDtypeStruct((n, CONV_WIDTH), F32),
    )
    return pl.pallas_call(
        _proj_sample_kernel,
        grid=(1,),
        in_specs=[full((n, D_MODEL)), full((n, D_MODEL)), full((n, D_MODEL)), full((1, D_MODEL)),
                  full((D_MODEL, IN_PAD)), full((1, LANES)), full((1, LANES)), full((1, LANES)),
                  full((CONV_K, CONV_WIDTH)), full((CONV_WIDTH, D_MODEL)), full((n, 2 * CONV_WIDTH))],
        out_specs=tuple(full(s.shape) for s in out_shapes),
        out_shape=out_shapes,
        compiler_params=pltpu.CompilerParams(dimension_semantics=("arbitrary",), vmem_limit_bytes=VMEM_LIMIT),
        name="proj_sample",
    )(x2d, shift, scale, g_pre, w_pad, *rope_tabs, conv_w, w_br_b, cbuf)


CHUNK_LANES = CMP_STRIDE * KV_WIDTH


def _compress_weights(pe_cmp, w_cmp1, w_cmp2):
    zeros = jnp.zeros((2, CMP_STRIDE, HEAD_DIM, CMP_HIDDEN), w_cmp1.dtype)

    def both_groups(w_half):
        w = w_half.reshape(2, CMP_STRIDE, HEAD_DIM, CMP_HIDDEN)
        g0 = jnp.concatenate([w, zeros], axis=2)
        g1 = jnp.concatenate([zeros, w], axis=2)
        return jnp.concatenate([g0, g1], axis=3).reshape(2, CHUNK_LANES, KV_HEADS * CMP_HIDDEN).astype(BF16)

    half = CMP_STRIDE * HEAD_DIM
    z2 = jnp.zeros_like(w_cmp2)
    w2 = jnp.concatenate([jnp.concatenate([w_cmp2, z2], axis=2), jnp.concatenate([z2, w_cmp2], axis=2)], axis=1)
    pe = jnp.concatenate([pe_cmp, pe_cmp], axis=2)
    return dict(w1_lo=both_groups(w_cmp1[:, :half]), w1_hi=both_groups(w_cmp1[:, half:]), w2=w2.astype(BF16),
                pe_lo=pe[:, :CMP_STRIDE].reshape(2, 1, CHUNK_LANES), pe_hi=pe[:, CMP_STRIDE:].reshape(2, 1, CHUNK_LANES))


def _compress_chunks(c, kind, pelo_ref, pehi_ref, w1lo_ref, w1hi_ref):
    lo = _dot((c + pelo_ref[kind]).astype(BF16), w1lo_ref[kind])
    hi = _dot((c + pehi_ref[kind]).astype(BF16), w1hi_ref[kind])
    return lo, hi


def _compress_prompt_kernel(kc_ref, vc_ref, pelo_ref, pehi_ref, w1lo_ref, w1hi_ref, w2_ref, kc_out_ref, vct_out_ref):
    n_chunk = kc_ref.shape[0] // CMP_STRIDE
    outs = []
    for kind, src_ref in enumerate((kc_ref, vc_ref)):
        c = jnp.concatenate([src_ref[pl.ds(r, n_chunk, stride=CMP_STRIDE), :] for r in range(CMP_STRIDE)], axis=1)
        lo, hi = _compress_chunks(c, kind, pelo_ref, pehi_ref, w1lo_ref, w1hi_ref)
        hid = lo + pltpu.roll(hi, n_chunk - 1, 0)
        outs.append(_dot(_silu(hid).astype(BF16), w2_ref[kind]))
    kc_out_ref[0] = outs[0].astype(BF16)
    vct_out_ref[0] = outs[1].T.astype(BF16)


def _compress_prompt(kvc, cw, batch, seq):
    n_chunk = seq // CMP_STRIDE
    const = lambda a: pl.BlockSpec(a.shape, lambda b: (0,) * a.ndim)
    return pl.pallas_call(
        _compress_prompt_kernel,
        grid=(batch,),
        in_specs=[
            pl.BlockSpec((seq, KV_WIDTH), lambda b: (b, 0)),
            pl.BlockSpec((seq, KV_WIDTH), lambda b: (b, 1)),
            const(cw["pe_lo"]), const(cw["pe_hi"]), const(cw["w1_lo"]), const(cw["w1_hi"]), const(cw["w2"]),
        ],
        out_specs=(pl.BlockSpec((1, n_chunk, KV_WIDTH), lambda b: (b, 0, 0)),
                   pl.BlockSpec((1, KV_WIDTH, n_chunk), lambda b: (b, 0, 0))),
        out_shape=(jax.ShapeDtypeStruct((batch, n_chunk, KV_WIDTH), BF16),
                   jax.ShapeDtypeStruct((batch, KV_WIDTH, n_chunk), BF16)),
        compiler_params=pltpu.CompilerParams(dimension_semantics=("arbitrary",), vmem_limit_bytes=VMEM_LIMIT),
        name="compress_prompt",
    )(kvc, kvc, cw["pe_lo"], cw["pe_hi"], cw["w1_lo"], cw["w1_hi"], cw["w2"])


def _top_k_bias(score):
    lane = lax.broadcasted_iota(jnp.int32, score.shape, 1).astype(F32)
    bias = jnp.full(score.shape, NEG_BIG, F32)
    for _ in range(N_SEL):
        m = jnp.max(score, axis=1, keepdims=True)
        idx = jnp.min(jnp.where(score == m, lane, float(LANES)), axis=1, keepdims=True)
        hit = lane == idx
        bias = jnp.where(hit, 0.0, bias)
        score = jnp.where(hit, -jnp.inf, score)
    return bias


def _split_bf16(x):
    hi = x.astype(BF16)
    lo = (x - hi.astype(F32)).astype(BF16)
    return hi, lo


def _masked_softmax_parts(s, valid):
    s = jnp.where(valid, s, -jnp.inf)
    m = jnp.max(s, axis=-1, keepdims=True)
    m = jnp.where(m == -jnp.inf, 0.0, m)
    e = jnp.exp(s - m)
    den = jnp.maximum(jnp.sum(e, axis=-1, keepdims=True), 1e-30)
    return e, den


def _attn_prompt_kernel(qpad_ref, kaug_ref, vsel_ref, kwin_ref, vwin_ref, kvcmp_ref, gates_ref, ovl_ref,
                        o_ref, qaug_ref, m_ref, l_ref, acc_ref, oc_ref, ow_ref):
    qb = pl.program_id(1)
    q0 = qb * Q_BLOCK
    seq = kaug_ref.shape[0]
    n_cmp = kvcmp_ref.shape[1]
    n_rows = N_HEADS * Q_BLOCK
    n_chunks = n_rows // ROW_CHUNK
    heads_per_chunk = ROW_CHUNK // Q_BLOCK

    for n in range(N_HEADS):
        qaug_ref[n * Q_BLOCK:(n + 1) * Q_BLOCK, 0:LANES] = qpad_ref[:, n * LANES:(n + 1) * LANES]

    def qpos_rows(rows):
        r = lax.broadcasted_iota(jnp.int32, (rows, 1), 0)
        return q0 + (r & (Q_BLOCK - 1))

    kc = kvcmp_ref[0, :, 0:LANES]
    vc = kvcmp_ref[0, :, LANES:2 * LANES]
    c_end = lax.broadcasted_iota(jnp.int32, (1, n_cmp), 1) * CMP_STRIDE + (CMP_BLOCK - 1)
    psum = [None] * KV_HEADS
    for rc in range(n_chunks):
        rows = slice(rc * ROW_CHUNK, (rc + 1) * ROW_CHUNK)
        s = _dot_nt(qaug_ref[rows, 0:LANES], kc)
        e, den = _masked_softmax_parts(s, c_end <= qpos_rows(ROW_CHUNK))
        p = e * (1.0 / den)
        oc_ref[rows, :] = _dot(p.astype(BF16), vc)
        g = (rc * heads_per_chunk) // HEADS_PER_GROUP
        part = p[0:Q_BLOCK]
        for h in range(1, heads_per_chunk):
            part = part + p[h * Q_BLOCK:(h + 1) * Q_BLOCK]
        psum[g] = part if psum[g] is None else psum[g] + part

    qpos = q0 + lax.broadcasted_iota(jnp.int32, (Q_BLOCK, 1), 0)
    cur = qpos // SEL_BLOCK
    j = lax.broadcasted_iota(jnp.int32, (Q_BLOCK, LANES), 1)
    forced = (j == 0) | (j == cur) | (j == cur - 1)
    for g in range(KV_HEADS):
        hi, lo = _split_bf16(psum[g])
        imp = _dot(hi, ovl_ref[...]) + _dot(lo, ovl_ref[...])
        score = jnp.where(forced, FORCED_SCORE, jnp.where(j <= cur, imp, -1.0))
        bias = _top_k_bias(score).astype(BF16)
        for h in range(HEADS_PER_GROUP):
            n = g * HEADS_PER_GROUP + h
            qaug_ref[n * Q_BLOCK:(n + 1) * Q_BLOCK, LANES:2 * LANES] = bias

    m_ref[...] = jnp.full(m_ref.shape, NEG_BIG, F32)
    l_ref[...] = jnp.zeros(l_ref.shape, F32)
    acc_ref[...] = jnp.zeros(acc_ref.shape, F32)
    kt_last = (q0 + Q_BLOCK - 1) // KV_TILE

    def sel_tile(kt, causal):
        k0 = pl.multiple_of(kt * KV_TILE, KV_TILE)
        k = kaug_ref[pl.ds(k0, KV_TILE), :]
        v = vsel_ref[pl.ds(k0, KV_TILE), :]
        for rc in range(n_chunks):
            rows = slice(rc * ROW_CHUNK, (rc + 1) * ROW_CHUNK)
            s = _dot_nt(qaug_ref[rows, :], k)
            if causal:
                kpos = k0 + lax.broadcasted_iota(jnp.int32, (1, KV_TILE), 1)
                s = jnp.where(kpos <= qpos_rows(ROW_CHUNK), s, NEG_BIG)
            m_old = m_ref[rows, :]
            m_new = jnp.maximum(m_old, jnp.max(s, axis=-1, keepdims=True))
            alpha = jnp.exp(m_old - m_new)
            p = jnp.exp(s - m_new)
            l_ref[rows, :] = alpha * l_ref[rows, :] + jnp.sum(p, axis=-1, keepdims=True)
            acc_ref[rows, :] = alpha * acc_ref[rows, :] + _dot(p.astype(BF16), v)
            m_ref[rows, :] = m_new

    def body(kt, carry):
        sel_tile(kt, False)
        return carry

    lax.fori_loop(0, kt_last, body, 0)
    sel_tile(kt_last, True)

    wk = kwin_ref.shape[0] if kwin_ref.shape[0] < WINDOW + Q_BLOCK else WINDOW + Q_BLOCK
    start = pl.multiple_of(jnp.maximum(q0 - WINDOW, 0), Q_BLOCK)
    kw = kwin_ref[pl.ds(start, wk), :]
    vw = vwin_ref[pl.ds(start, wk), :]
    kwpos = start + lax.broadcasted_iota(jnp.int32, (1, wk), 1)
    for rc in range(n_chunks):
        rows = slice(rc * ROW_CHUNK, (rc + 1) * ROW_CHUNK)
        s = _dot_nt(qaug_ref[rows, 0:LANES], kw)
        dt = qpos_rows(ROW_CHUNK) - kwpos
        e, den = _masked_softmax_parts(s, (dt >= 0) & (dt < WINDOW))
        ow_ref[rows, :] = _dot(e.astype(BF16), vw) * (1.0 / den)

    lane = lax.broadcasted_iota(jnp.int32, (Q_BLOCK, LANES), 1)
    lower = lane < HEAD_DIM
    gates = gates_ref[...]
    for pair in range(N_HEADS // 2):
        halves = []
        for n in (2 * pair, 2 * pair + 1):
            rows = slice(n * Q_BLOCK, (n + 1) * Q_BLOCK)
            o_sel = acc_ref[rows, :] * (1.0 / l_ref[rows, :])
            o = (gates[:, 3 * n:3 * n + 1] * oc_ref[rows, :] + gates[:, 3 * n + 1:3 * n + 2] * o_sel
                 + gates[:, 3 * n + 2:3 * n + 3] * ow_ref[rows, :])
            halves.append(o)
        g = (2 * pair) // HEADS_PER_GROUP
        if g == 0:
            merged = jnp.where(lower, halves[0], pltpu.roll(halves[1], HEAD_DIM, 1))
        else:
            merged = jnp.where(lower, pltpu.roll(halves[0], HEAD_DIM, 1), halves[1])
        o_ref[:, pair * LANES:(pair + 1) * LANES] = merged


def _attn_prompt(qpad, ksel, kwinb, kvcmp, gates, ovl, batch, seq):
    nq = seq // Q_BLOCK
    n_cmp = kvcmp.shape[1]
    n_rows = N_HEADS * Q_BLOCK
    rowq = lambda w: pl.BlockSpec((Q_BLOCK, w), lambda b, i: (b * nq + i, 0))
    return pl.pallas_call(
        _attn_prompt_kernel,
        grid=(batch, nq),
        in_specs=[
            rowq(N_HEADS * LANES),
            pl.BlockSpec((seq, 2 * LANES), lambda b, i: (b, 0)),
            pl.BlockSpec((seq, LANES), lambda b, i: (b, 2)),
            pl.BlockSpec((seq, LANES), lambda b, i: (b, 0)),
            pl.BlockSpec((seq, LANES), lambda b, i: (b, 1)),
            pl.BlockSpec((1, n_cmp, 2 * LANES), lambda b, i: (b, 0, 0)),
            rowq(LANES),
            pl.BlockSpec((n_cmp, LANES), lambda b, i: (0, 0)),
        ],
        out_specs=rowq(ATTN_WIDTH),
        out_shape=jax.ShapeDtypeStruct((batch * seq, ATTN_WIDTH), F32),
        scratch_shapes=[
            pltpu.VMEM((n_rows, 2 * LANES), BF16),
            pltpu.VMEM((n_rows, 1), F32),
            pltpu.VMEM((n_rows, 1), F32),
            pltpu.VMEM((n_rows, LANES), F32),
            pltpu.VMEM((n_rows, LANES), F32),
            pltpu.VMEM((n_rows, LANES), F32),
        ],
        compiler_params=pltpu.CompilerParams(dimension_semantics=("arbitrary", "arbitrary"),
                                             vmem_limit_bytes=VMEM_LIMIT),
        name="attn_prompt",
    )(qpad, ksel, ksel, kwinb, kwinb, kvcmp, gates, ovl)


COL_CHUNK = 256


def _skewed(n, stages):
    for step in range(n + len(stages) - 1):
        for si, stage in enumerate(stages):
            if 0 <= step - si < n:
                stage(step - si)


def _top_k_bias_t(score):
    blk = lax.broadcasted_iota(jnp.int32, score.shape, 0).astype(F32)
    bias = jnp.full(score.shape, NEG_BIG, F32)
    for _ in range(N_SEL):
        m = jnp.max(score, axis=0, keepdims=True)
        idx = jnp.min(jnp.where(score == m, blk, float(LANES)), axis=0, keepdims=True)
        hit = blk == idx
        bias = jnp.where(hit, 0.0, bias)
        score = jnp.where(hit, -jnp.inf, score)
    return bias


def _masked_softmax_parts_t(s, valid):
    s = jnp.where(valid, s, -jnp.inf)
    m = jnp.max(s, axis=0, keepdims=True)
    m = jnp.where(m == -jnp.inf, 0.0, m)
    e = jnp.exp2(s - m)
    den = jnp.maximum(jnp.sum(e, axis=0, keepdims=True), 1e-30)
    return e, den


def _attn_prompt_t_kernel(qt_ref, kaug_ref, vselt_ref, kwin_ref, vwint_ref, kc_ref, vct_ref, gatest_ref, ovlt_ref,
                          o_ref, qaugt_ref, m_ref, l_ref, acct_ref, oct_ref, owt_ref):
    qb = pl.program_id(1)
    q0 = qb * Q_BLOCK
    n_cmp = kc_ref.shape[1]
    n_cols = N_HEADS * Q_BLOCK
    n_chunks = n_cols // COL_CHUNK
    heads_per_chunk = COL_CHUNK // Q_BLOCK

    for n in range(N_HEADS):
        qaugt_ref[0:LANES, n * Q_BLOCK:(n + 1) * Q_BLOCK] = qt_ref[n]

    def qpos_cols(cols):
        c = lax.broadcasted_iota(jnp.int32, (1, cols), 1)
        return q0 + (c & (Q_BLOCK - 1))

    def chunk_cols(cc):
        return slice(cc * COL_CHUNK, (cc + 1) * COL_CHUNK)

    kc = kc_ref[0]
    vct = vct_ref[0]
    c_end = lax.broadcasted_iota(jnp.int32, (n_cmp, 1), 0) * CMP_STRIDE + (CMP_BLOCK - 1)
    psum = [None] * KV_HEADS
    cs, cp = {}, {}

    def cmp_scores(cc):
        cs[cc] = _dot(kc, qaugt_ref[0:LANES, chunk_cols(cc)])

    def cmp_softmax(cc):
        e, den = _masked_softmax_parts_t(cs.pop(cc), c_end <= qpos_cols(COL_CHUNK))
        p = e * (1.0 / den)
        cp[cc] = p.astype(BF16)
        g = (cc * heads_per_chunk) // HEADS_PER_GROUP
        part = p[:, 0:Q_BLOCK]
        for h in range(1, heads_per_chunk):
            part = part + p[:, h * Q_BLOCK:(h + 1) * Q_BLOCK]
        psum[g] = part if psum[g] is None else psum[g] + part

    def cmp_values(cc):
        oct_ref[:, chunk_cols(cc)] = _dot(vct, cp.pop(cc))

    _skewed(n_chunks, (cmp_scores, cmp_softmax, cmp_values))

    cur = qpos_cols(KV_HEADS * Q_BLOCK) // SEL_BLOCK
    j = lax.broadcasted_iota(jnp.int32, (LANES, KV_HEADS * Q_BLOCK), 0)
    forced = (j == 0) | (j == cur) | (j == cur - 1)
    hi, lo = _split_bf16(jnp.concatenate(psum, axis=1))
    imp = _dot(ovlt_ref[...], hi) + _dot(ovlt_ref[...], lo)
    score = jnp.where(forced, FORCED_SCORE, jnp.where(j <= cur, imp, -1.0))
    bias = _top_k_bias_t(score).astype(BF16)
    for n in range(N_HEADS):
        g = n // HEADS_PER_GROUP
        qaugt_ref[LANES:2 * LANES, n * Q_BLOCK:(n + 1) * Q_BLOCK] = bias[:, g * Q_BLOCK:(g + 1) * Q_BLOCK]

    m_ref[...] = jnp.full(m_ref.shape, NEG_BIG, F32)
    l_ref[...] = jnp.zeros(l_ref.shape, F32)
    acct_ref[...] = jnp.zeros(acct_ref.shape, F32)
    kt_last = (q0 + Q_BLOCK - 1) // KV_TILE

    def sel_tile(kt, causal):
        k0 = pl.multiple_of(kt * KV_TILE, KV_TILE)
        k = kaug_ref[pl.ds(k0, KV_TILE), :]
        vt = vselt_ref[:, pl.ds(k0, KV_TILE)]
        s, pb, alpha = {}, {}, {}

        def scores(cc):
            s[cc] = _dot(k, qaugt_ref[:, chunk_cols(cc)])
            if causal:
                kpos = k0 + lax.broadcasted_iota(jnp.int32, (KV_TILE, 1), 0)
                s[cc] = jnp.where(kpos <= qpos_cols(COL_CHUNK), s[cc], NEG_BIG)

        def softmax(cc):
            cols = chunk_cols(cc)
            m_old = m_ref[:, cols]
            m_new = jnp.maximum(m_old, jnp.max(s[cc], axis=0, keepdims=True))
            alpha[cc] = jnp.exp2(m_old - m_new)
            p = jnp.exp2(s.pop(cc) - m_new)
            l_ref[:, cols] = alpha[cc] * l_ref[:, cols] + jnp.sum(p, axis=0, keepdims=True)
            m_ref[:, cols] = m_new
            pb[cc] = p.astype(BF16)

        def values(cc):
            cols = chunk_cols(cc)
            acct_ref[:, cols] = alpha.pop(cc) * acct_ref[:, cols] + _dot(vt, pb.pop(cc))

        _skewed(n_chunks, (scores, softmax, values))

    def body(kt, carry):
        sel_tile(kt, False)
        return carry

    lax.fori_loop(0, kt_last, body, 0)
    sel_tile(kt_last, True)

    wk = kwin_ref.shape[0] if kwin_ref.shape[0] < WINDOW + Q_BLOCK else WINDOW + Q_BLOCK
    start = pl.multiple_of(jnp.maximum(q0 - WINDOW, 0), Q_BLOCK)
    kw = kwin_ref[pl.ds(start, wk), :]
    vwt = vwint_ref[:, pl.ds(start, wk)]
    kwpos = start + lax.broadcasted_iota(jnp.int32, (wk, 1), 0)
    ws, we, wden = {}, {}, {}

    def win_scores(cc):
        ws[cc] = _dot(kw, qaugt_ref[0:LANES, chunk_cols(cc)])

    def win_softmax(cc):
        dt = qpos_cols(COL_CHUNK) - kwpos
        e, wden[cc] = _masked_softmax_parts_t(ws.pop(cc), (dt >= 0) & (dt < WINDOW))
        we[cc] = e.astype(BF16)

    def win_values(cc):
        owt_ref[:, chunk_cols(cc)] = _dot(vwt, we.pop(cc)) * (1.0 / wden.pop(cc))

    _skewed(n_chunks, (win_scores, win_softmax, win_values))

    gates = gatest_ref[...]
    for pair in range(N_HEADS // 2):
        g = (2 * pair) // HEADS_PER_GROUP
        feat = slice(g * HEAD_DIM, (g + 1) * HEAD_DIM)
        halves = []
        for n in (2 * pair, 2 * pair + 1):
            cols = slice(n * Q_BLOCK, (n + 1) * Q_BLOCK)
            o_sel = acct_ref[feat, cols] * (1.0 / l_ref[:, cols])
            halves.append(gates[3 * n:3 * n + 1, :] * oct_ref[feat, cols] + gates[3 * n + 1:3 * n + 2, :] * o_sel
                          + gates[3 * n + 2:3 * n + 3, :] * owt_ref[feat, cols])
        o_ref[:, pair * LANES:(pair + 1) * LANES] = jnp.concatenate(halves, axis=0).T


def _attn_prompt_t(qt, ksel, kwinb, vselt, vwint, kc, vct, gatest, ovlt, batch, seq):
    nq = seq // Q_BLOCK
    n_cmp = kc.shape[1]
    n_cols = N_HEADS * Q_BLOCK
    per_b = lambda a: pl.BlockSpec((None,) + a.shape[1:], lambda b, i: (b,) + (0,) * (a.ndim - 1))
    return pl.pallas_call(
        _attn_prompt_t_kernel,
        grid=(batch, nq),
        in_specs=[
            pl.BlockSpec((None, N_HEADS, LANES, Q_BLOCK), lambda b, i: (b, 0, 0, i)),
            pl.BlockSpec((seq, 2 * LANES), lambda b, i: (b, 0)),
            per_b(vselt),
            pl.BlockSpec((seq, LANES), lambda b, i: (b, 0)),
            per_b(vwint),
            pl.BlockSpec((1, n_cmp, LANES), lambda b, i: (b, 0, 0)),
            pl.BlockSpec((1, LANES, n_cmp), lambda b, i: (b, 0, 0)),
            pl.BlockSpec((None, GATE_ROWS, Q_BLOCK), lambda b, i: (b, 0, i)),
            pl.BlockSpec((LANES, n_cmp), lambda b, i: (0, 0)),
        ],
        out_specs=pl.BlockSpec((Q_BLOCK, ATTN_WIDTH), lambda b, i: (b * nq + i, 0)),
        out_shape=jax.ShapeDtypeStruct((batch * seq, ATTN_WIDTH), F32),
        scratch_shapes=[
            pltpu.VMEM((2 * LANES, n_cols), BF16),
            pltpu.VMEM((1, n_cols), F32),
            pltpu.VMEM((1, n_cols), F32),
            pltpu.VMEM((LANES, n_cols), F32),
            pltpu.VMEM((LANES, n_cols), F32),
            pltpu.VMEM((LANES, n_cols), F32),
        ],
        compiler_params=pltpu.CompilerParams(dimension_semantics=("arbitrary", "arbitrary"),
                                             vmem_limit_bytes=VMEM_LIMIT),
        name="attn_prompt",
    )(qt, ksel, vselt, kwinb, vwint, kc, vct, gatest, ovlt)


def _finish_kernel(x_ref, o_ref, sa_ref, ga_ref, pb_ref, gate_ref, gpost_ref, wbra_ref, wout_ref, y_ref):
    ya = _dot((o_ref[...] * sa_ref[...]).astype(BF16), wbra_ref[...])
    mix = ga_ref[...] * ya + pb_ref[...]
    o = _dot(mix.astype(BF16), wout_ref[...])
    ms = jnp.mean(o * o, axis=-1, keepdims=True)
    on = o * lax.rsqrt(ms + RMS_EPS) * gpost_ref[...]
    y_ref[...] = x_ref[...] + gate_ref[...] * on


def _finish(x2d, o_attn, sa, ga, pb, gate, g_post, w_br_a, w_out, tm, rows_per_gate):
    n = x2d.shape[0]
    row = lambda w: pl.BlockSpec((tm, w), lambda i: (i, 0))
    const = lambda shp: pl.BlockSpec(shp, lambda i: (0,) * len(shp))
    if rows_per_gate == 1:
        gate_spec = row(D_MODEL)
    else:
        tiles_per_gate = rows_per_gate // tm
        gate = gate[:, None, :]
        gate_spec = pl.BlockSpec((None, 1, D_MODEL), lambda i: (i // tiles_per_gate, 0, 0))
    return pl.pallas_call(
        _finish_kernel,
        grid=(n // tm,),
        in_specs=[row(D_MODEL), row(ATTN_WIDTH), row(ATTN_WIDTH), row(D_MODEL), row(D_MODEL), gate_spec,
                  const((1, D_MODEL)), const((ATTN_WIDTH, D_MODEL)), const((D_MODEL, D_MODEL))],
        out_specs=row(D_MODEL),
        out_shape=jax.ShapeDtypeStruct((n, D_MODEL), F32),
        compiler_params=pltpu.CompilerParams(dimension_semantics=("arbitrary",), vmem_limit_bytes=VMEM_LIMIT),
        name="finish",
    )(x2d, o_attn, sa, ga, pb, gate, g_post, w_br_a, w_out)


def _attn_decode_kernel(pt_ref, *refs, n_pages, page_size, wbuf):
    del pt_ref
    page_refs = refs[:n_pages]
    (win_ref, q_ref, kvnew_ref, gates_ref, pelo_ref, pehi_ref, w1lo_ref, w1hi_ref, w2_ref, ovl_ref, expand_ref,
     o_ref, winout_ref, kcmp_ref, vcmp_ref) = refs[n_pages:]
    past_len = n_pages * page_size
    n_chunk = past_len // CMP_STRIDE
    q8f = q_ref[...]
    q8 = q8f.astype(BF16)

    def stream(ref, which):
        slab = ref[which * KV_HEADS:(which + 1) * KV_HEADS]
        return slab.reshape(KV_WIDTH, slab.shape[-1])

    for i, pr in enumerate(page_refs):
        rows = slice(i * page_size, (i + 1) * page_size)
        kcmp_ref[rows, :] = stream(pr, 0).T
        vcmp_ref[rows, :] = stream(pr, 1).T

    kvc = []
    for kind, src_ref in enumerate((kcmp_ref, vcmp_ref)):
        c = jnp.concatenate([src_ref[pl.ds(r, n_chunk, stride=CMP_STRIDE), :] for r in range(CMP_STRIDE)], axis=1)
        lo, hi = _compress_chunks(c, kind, pelo_ref, pehi_ref, w1lo_ref, w1hi_ref)
        hid = lo + pltpu.roll(hi, n_chunk - 1, 0)
        kvc.append(_dot(_silu(hid).astype(BF16), w2_ref[kind]).astype(BF16))
    kc, vc = kvc

    c_end = lax.broadcasted_iota(jnp.int32, (1, n_chunk), 1) * CMP_STRIDE + (CMP_BLOCK - 1)
    e, den = _masked_softmax_parts(_dot_nt(q8, kc), c_end <= past_len)
    p = e * (1.0 / den)
    o_cmp = _dot(p.astype(BF16), vc)

    row = lax.broadcasted_iota(jnp.int32, p.shape, 0)
    in_g0 = row < HEADS_PER_GROUP
    g0 = jnp.sum(jnp.where(in_g0, p, 0.0), axis=0, keepdims=True)
    g1 = jnp.sum(jnp.where(in_g0, 0.0, p), axis=0, keepdims=True)
    hi, lo = _split_bf16(jnp.where(in_g0, g0, g1))
    imp = _dot(hi, ovl_ref[...]) + _dot(lo, ovl_ref[...])
    cur = past_len // SEL_BLOCK
    j = lax.broadcasted_iota(jnp.int32, imp.shape, 1)
    forced = (j == 0) | (j == cur) | (j == cur - 1)
    score = jnp.where(j > cur, -jnp.inf, jnp.where(forced, FORCED_SCORE, imp))
    bias_keys = _dot(_top_k_bias(score).astype(BF16), expand_ref[...])

    s = jnp.concatenate([_dot(q8, stream(pr, 2).astype(BF16)) for pr in page_refs], axis=1) + bias_keys
    s_new = jnp.sum(q8f * kvnew_ref[2:3, :], axis=1, keepdims=True)
    m = jnp.maximum(jnp.max(s, axis=1, keepdims=True), s_new)
    e = jnp.exp(s - m)
    e_new = jnp.exp(s_new - m)
    den = jnp.sum(e, axis=1, keepdims=True) + e_new
    acc = e_new * kvnew_ref[3:4, :]
    for i, pr in enumerate(page_refs):
        acc = acc + _dot_nt(e[:, i * page_size:(i + 1) * page_size].astype(BF16), stream(pr, 3).astype(BF16))
    o_sel = acc * (1.0 / den)

    kwpos = past_len - wbuf + lax.broadcasted_iota(jnp.int32, (1, wbuf), 1)
    dt = past_len - kwpos
    valid = (dt >= 0) & (dt < WINDOW) & (kwpos >= 0)
    kw_t = stream(win_ref, 0)
    vw_t = stream(win_ref, 1)
    s = jnp.where(valid, _dot(q8, kw_t.astype(BF16)), -jnp.inf)
    s_new = jnp.sum(q8f * kvnew_ref[4:5, :], axis=1, keepdims=True)
    m = jnp.maximum(jnp.max(s, axis=1, keepdims=True), s_new)
    e = jnp.exp(s - m)
    e_new = jnp.exp(s_new - m)
    den = jnp.sum(e, axis=1, keepdims=True) + e_new
    o_win = (_dot_nt(e.astype(BF16), vw_t.astype(BF16)) + e_new * kvnew_ref[5:6, :]) * (1.0 / den)

    gates = gates_ref[...]
    o = gates[:, 0:1] * o_cmp + gates[:, 1:2] * o_sel + gates[:, 2:3] * o_win
    o_ref[...] = jnp.where(in_g0, o, pltpu.roll(o, HEAD_DIM, 1))

    new_cols = jnp.concatenate([kvnew_ref[...], jnp.zeros((SUBLANES - 6, LANES), F32)], axis=0).T
    lane = lax.broadcasted_iota(jnp.int32, (KV_WIDTH, wbuf), 1)
    for kind, old in enumerate((kw_t, vw_t)):
        shifted = jnp.where(lane == wbuf - 1, new_cols[:, 4 + kind:5 + kind], pltpu.roll(old, wbuf - 1, 1))
        winout_ref[kind * KV_HEADS:(kind + 1) * KV_HEADS] = shifted.reshape(KV_HEADS, HEAD_DIM, wbuf)


def _attn_decode(cache_t, page_table, win_t, q8, kvnew, gates8, cw, ovl, expand):
    page_size = cache_t.shape[-1]
    batch, n_pages = page_table.shape
    wbuf = win_t.shape[-1]
    past_len = n_pages * page_size

    def page_spec(k):
        return pl.BlockSpec((None, PAGE_ROWS, HEAD_DIM, page_size), lambda b, pt: (pt[b, k], 0, 0, 0))

    per_b = lambda a: pl.BlockSpec((None,) + a.shape[1:], lambda b, pt: (b,) + (0,) * (a.ndim - 1))
    const = lambda a: pl.BlockSpec(a.shape, lambda b, pt: (0,) * a.ndim, pipeline_mode=pl.Buffered(1))
    consts = [cw["pe_lo"], cw["pe_hi"], cw["w1_lo"], cw["w1_hi"], cw["w2"], ovl, expand]
    grid_spec = pltpu.PrefetchScalarGridSpec(
        num_scalar_prefetch=1,
        grid=(batch,),
        in_specs=[page_spec(k) for k in range(n_pages)]
        + [per_b(win_t), per_b(q8), per_b(kvnew), per_b(gates8)]
        + [const(a) for a in consts],
        out_specs=(pl.BlockSpec((None, N_HEADS, LANES), lambda b, pt: (b, 0, 0)), per_b(win_t)),
        scratch_shapes=[pltpu.VMEM((past_len, LANES), F32), pltpu.VMEM((past_len, LANES), F32)],
    )
    o8, win_out = pl.pallas_call(
        functools.partial(_attn_decode_kernel, n_pages=n_pages, page_size=page_size, wbuf=wbuf),
        grid_spec=grid_spec,
        out_shape=(jax.ShapeDtypeStruct((batch, N_HEADS, LANES), F32),
                   jax.ShapeDtypeStruct(win_t.shape, F32)),
        compiler_params=pltpu.CompilerParams(dimension_semantics=("arbitrary",), vmem_limit_bytes=VMEM_LIMIT),
        name="attn_decode",
    )(page_table, *([cache_t] * n_pages), win_t, q8, kvnew, gates8, *consts)
    return o8, win_out


def _overlap_matrix(n_cmp_pad, n_cmp, n_selb):
    cs = jnp.arange(n_cmp_pad) * CMP_STRIDE
    ss = jnp.arange(LANES) * SEL_BLOCK
    ov = (cs[:, None] < ss[None, :] + SEL_BLOCK) & (cs[:, None] + CMP_BLOCK > ss[None, :])
    ov = ov & (jnp.arange(n_cmp_pad) < n_cmp)[:, None] & (jnp.arange(LANES) < n_selb)[None, :]
    return ov.astype(BF16)


def _prompt_layer(x, mod, wts):
    batch, seq, _ = x.shape
    x2d = x.reshape(batch * seq, D_MODEL)
    shift, scale, gate = mod[:, 0:D_MODEL], mod[:, D_MODEL:2 * D_MODEL], mod[:, 2 * D_MODEL:]
    tabs = _rope_tables(jnp.arange(seq))
    tm = min(256, seq)
    (qt, kvt, kvc, kwin, ksel, kwinb, vselt, vwint, gatest, sa, ga, pb, utail) = _proj_prompt(
        x2d, shift, scale, wts["g_pre"], wts["w_in"], tabs, wts["conv_w"], wts["w_br_b"], batch, seq, tm)
    kc, vct = _compress_prompt(kvc, wts["cmp"], batch, seq)
    n_chunk = seq // CMP_STRIDE
    ovlt = _overlap_matrix(n_chunk, n_chunk - 1, -(-seq // SEL_BLOCK)).T
    o_attn = _attn_prompt_t(qt, ksel, kwinb, vselt, vwint, kc, vct, gatest, ovlt, batch, seq)
    y = _finish(x2d, o_attn, sa, ga, pb, gate, wts["g_post"], wts["w_br_a"], wts["w_out"], tm, seq)
    n_keep = min(WINDOW, seq)
    return (y.reshape(batch, seq, D_MODEL),
            jnp.transpose(kvt.reshape(batch, 4, KV_HEADS, HEAD_DIM, seq), (0, 4, 1, 2, 3)),
            kwin.reshape(batch, seq, 2, KV_HEADS, HEAD_DIM)[:, seq - n_keep:],
            utail[:, SUBLANES - (CONV_K - 1):])


def _sample_layer(x, mod, cache, page_table, win, conv_state, wts):
    batch, dec_seq, _ = x.shape
    assert dec_seq == 1
    n_pages, page_size, wbuf = page_table.shape[1], cache.shape[1], win.shape[1]
    past_len = n_pages * page_size
    assert past_len % SEL_BLOCK == 0 and wbuf == WINDOW and past_len // SEL_BLOCK < LANES
    x2d = x.reshape(batch, D_MODEL)
    shift, scale, gate = mod[:, 0:D_MODEL], mod[:, D_MODEL:2 * D_MODEL], mod[:, 2 * D_MODEL:]
    tabs = _rope_tables(jnp.full((1,), past_len, jnp.int32))
    cbuf = conv_state.reshape(batch, (CONV_K - 1) * CONV_WIDTH)
    qpad, kvnew, gates, sa, ga, pb, u = _proj_sample(
        x2d, shift, scale, wts["g_pre"], wts["w_in"], tabs, wts["conv_w"], wts["w_br_b"], cbuf)
    n_gate = N_HEADS * 3
    gates8 = jnp.pad(gates[:, :n_gate].reshape(batch, N_HEADS, 3), ((0, 0), (0, 0), (0, LANES - 3)))
    n_chunk = past_len // CMP_STRIDE
    ovl = _overlap_matrix(n_chunk, n_chunk - 1, past_len // SEL_BLOCK + 1)
    expand = (jnp.arange(LANES)[:, None] == (jnp.arange(past_len) // SEL_BLOCK)[None, :]).astype(BF16)
    cache_t = jnp.transpose(cache, (0, 2, 3, 4, 1)).reshape(cache.shape[0], PAGE_ROWS, HEAD_DIM, page_size)
    win_t = jnp.transpose(win, (0, 2, 3, 4, 1)).reshape(batch, WIN_ROWS, HEAD_DIM, wbuf)
    o8, win_out = _attn_decode(cache_t, page_table, win_t, qpad.reshape(batch, N_HEADS, LANES).astype(F32),
                               kvnew.reshape(batch, 6, LANES), gates8, wts["cmp"], ovl, expand)
    win_out = jnp.transpose(win_out.reshape(batch, 2, KV_HEADS, HEAD_DIM, wbuf), (0, 4, 1, 2, 3))
    y = _finish(x2d, o8[:, :, :HEAD_DIM].reshape(batch, ATTN_WIDTH), sa, ga, pb, gate, wts["g_post"], wts["w_br_a"],
                wts["w_out"], batch, 1)
    return (y.reshape(batch, 1, D_MODEL),
            kvnew[:, :4 * KV_WIDTH].reshape(batch, 1, 4, KV_HEADS, HEAD_DIM),
            win_out,
            jnp.stack([conv_state[:, CONV_K - 2], u], axis=1))


def _prep_weights(w_ada, b_ada, g_pre, g_post, w_in, pe_cmp, w_cmp1, w_cmp2, conv_w, w_br_a, w_br_b, w_out):
    n_unpadded_gate = HEADS_PER_GROUP * KV_HEADS * 3
    w_pad = jnp.concatenate(
        [w_in[:, :C_G + n_unpadded_gate], jnp.zeros((D_MODEL, LANES - n_unpadded_gate), w_in.dtype),
         w_in[:, C_G + n_unpadded_gate:]], axis=1).astype(BF16)
    half = CMP_STRIDE * HEAD_DIM
    return dict(
        w_ada=w_ada, b_ada=b_ada, g_pre=g_pre.reshape(1, -1), g_post=g_post.reshape(1, -1), w_in=w_pad,
        cmp=_compress_weights(pe_cmp, w_cmp1, w_cmp2), conv_w=conv_w,
        w_br_a=w_br_a.astype(BF16), w_br_b=w_br_b.astype(BF16), w_out=w_out.astype(BF16))


def kernel(x_prompt, x_sample, cache_kv_pages, state_win_kv, state_conv, page_table, c_prompt, c_sample, w_ada, b_ada, g_pre, g_post, w_in, pe_cmp, w_cmp1, w_cmp2, conv_w, w_br_a, w_br_b, w_out):
    depth = w_in.shape[0]
    assert depth == 1
    wts = _prep_weights(w_ada[0], b_ada[0], g_pre[0], g_post[0], w_in[0], pe_cmp[0], w_cmp1[0], w_cmp2[0],
                        conv_w[0], w_br_a[0], w_br_b[0], w_out[0])
    n_prompt = c_prompt.shape[0]
    mod = _ada(jnp.concatenate([c_prompt, c_sample], axis=0), wts["w_ada"], wts["b_ada"])
    yp, kvp, wp, cp = _prompt_layer(x_prompt, mod[:n_prompt], wts)
    ys, kvs, ws, cs = _sample_layer(x_sample, mod[n_prompt:], cache_kv_pages[0], page_table, state_win_kv[0],
                                    state_conv[0], wts)
    return (yp, ys, kvp[None], wp[None], cp[None], kvs[None], ws[None], cs[None])
```

```python
import functools

import jax
import jax.numpy as jnp
from jax import lax
from jax.experimental import pallas as pl
from jax.experimental.pallas import tpu as pltpu

F32 = jnp.float32
BF16 = jnp.bfloat16

D_MODEL = 1024
N_HEADS = 8
KV_HEADS = 2
HEADS_PER_GROUP = N_HEADS // KV_HEADS
HEAD_DIM = 64
ROPE_DIM = HEAD_DIM // 4
ROPE_THETA = 500000.0
CMP_BLOCK = 32
CMP_STRIDE = 16
CMP_HIDDEN = 256
SEL_BLOCK = 64
N_SEL = 16
WINDOW = 512
Q_BLOCK = 128
FORCED_SCORE = 1e6
CONV_WIDTH = D_MODEL // 2
CONV_K = 3
ATTN_WIDTH = N_HEADS * HEAD_DIM
KV_WIDTH = KV_HEADS * HEAD_DIM
RMS_EPS = 1e-6

LANES = 128
SUBLANES = 8
VMEM_LIMIT = 56 * 1024 * 1024

C_Q = 0
C_KV = C_Q + ATTN_WIDTH
C_G = C_KV + 6 * KV_WIDTH
C_A = C_G + LANES
C_CB = C_A + ATTN_WIDTH
C_CC = C_CB + CONV_WIDTH
C_CX = C_CC + CONV_WIDTH
C_CG = C_CX + CONV_WIDTH
C_MA = C_CG + CONV_WIDTH
C_MB = C_MA + D_MODEL
IN_PAD = C_MB + D_MODEL

PAGE_ROWS = 4 * KV_HEADS
WIN_ROWS = 2 * KV_HEADS

GATE_ROWS = 32
LOG2_E = 1.4426950408889634

NEG_BIG = -1e30
KV_TILE = 512
ROW_CHUNK = 256


def _sigmoid(x):
    return 1.0 / (1.0 + jnp.exp(-x))


def _silu(x):
    return x * _sigmoid(x)


def _dot(a, b):
    return jnp.dot(a, b, preferred_element_type=F32)


def _dot_nt(a, b):
    return lax.dot_general(a, b, (((1,), (1,)), ((), ())), preferred_element_type=F32)


def _rope_tables(pos):
    half = ROPE_DIM // 2
    inv = ROPE_THETA ** (-jnp.arange(half, dtype=F32) / half)
    ang = pos.astype(F32)[:, None] * inv[None, :]
    cos, sin = jnp.cos(ang), jnp.sin(ang)
    n = pos.shape[0]
    a = jnp.concatenate([cos, cos, jnp.ones((n, HEAD_DIM - ROPE_DIM), F32)], axis=1)
    p = jnp.concatenate([jnp.zeros((n, half), F32), sin, jnp.zeros((n, HEAD_DIM - ROPE_DIM), F32)], axis=1)
    m = jnp.concatenate([-sin, jnp.zeros((n, HEAD_DIM - half), F32)], axis=1)
    tile = lambda t: jnp.concatenate([t, t], axis=1)
    return tile(a), tile(p), tile(m)


def _rope(x, ra, rp, rm):
    half = ROPE_DIM // 2
    return x * ra + pltpu.roll(x, half, 1) * rp + pltpu.roll(x, LANES - half, 1) * rm


def _ada_kernel(c_ref, w_ref, b_ref, o_ref):
    c = _silu(c_ref[...]).astype(BF16)
    o_ref[...] = _dot(c, w_ref[...].astype(BF16)) + b_ref[...]


def _ada(c_all, w_ada, b_ada):
    n = c_all.shape[0]
    tn = 512
    return pl.pallas_call(
        _ada_kernel,
        grid=(3 * D_MODEL // tn,),
        in_specs=[
            pl.BlockSpec((n, D_MODEL), lambda j: (0, 0)),
            pl.BlockSpec((D_MODEL, tn), lambda j: (0, j)),
            pl.BlockSpec((1, tn), lambda j: (0, j)),
        ],
        out_specs=pl.BlockSpec((n, tn), lambda j: (0, j)),
        out_shape=jax.ShapeDtypeStruct((n, 3 * D_MODEL), F32),
        compiler_params=pltpu.CompilerParams(dimension_semantics=("arbitrary",), vmem_limit_bytes=VMEM_LIMIT),
        name="ada",
    )(c_all, w_ada, b_ada.reshape(1, -1))


def _proj_common(x_ref, shift_ref, scale_ref, gpre_ref, w_ref, ra_ref, rp_ref, rm_ref):
    x = x_ref[...]
    ms = jnp.mean(x * x, axis=-1, keepdims=True)
    xn = x * lax.rsqrt(ms + RMS_EPS) * gpre_ref[...]
    h = xn * (1.0 + scale_ref[...]) + shift_ref[...]
    hb = h.astype(BF16)
    ra, rp, rm = ra_ref[...], rp_ref[...], rm_ref[...]

    def seg(lo, hi):
        return _dot(hb, w_ref[:, lo:hi])

    return seg, (ra, rp, rm)


def _padded_q_heads(seg, rope, scale):
    zq = seg(C_Q, C_Q + ATTN_WIDTH)
    lane = lax.broadcasted_iota(jnp.int32, (zq.shape[0], LANES), 1)
    lower = lane < HEAD_DIM
    heads = []
    for j in range(ATTN_WIDTH // LANES):
        c = _rope(zq[:, j * LANES:(j + 1) * LANES], *rope) * scale
        r = pltpu.roll(c, HEAD_DIM, 1)
        if (2 * j) // HEADS_PER_GROUP == 0:
            heads += [jnp.where(lower, c, 0.0), jnp.where(lower, r, 0.0)]
        else:
            heads += [jnp.where(lower, 0.0, r), jnp.where(lower, 0.0, c)]
    return heads


def _kv_pieces(seg, rope):
    zkv = seg(C_KV, C_KV + 6 * KV_WIDTH)
    pieces = []
    for p in range(6):
        c = zkv[:, p * LANES:(p + 1) * LANES]
        pieces.append(_rope(c, *rope) if p % 2 == 0 else c)
    return pieces


def _branch_b(seg, um2, um1, u, convw_ref, wbrb_ref):
    cb = seg(C_CB, C_CB + CONV_WIDTH)
    conv = convw_ref[0:1, :] * um2
    conv = conv + convw_ref[1:2, :] * um1
    conv = conv + convw_ref[2:3, :] * u
    ybin = cb * conv * _silu(seg(C_CG, C_CG + CONV_WIDTH))
    yb = _dot(ybin.astype(BF16), wbrb_ref[...])
    gb = _sigmoid(seg(C_MB, C_MB + D_MODEL))
    return gb * yb


def _proj_prompt_kernel(x_ref, shift_ref, scale_ref, gpre_ref, w_ref, ra_ref, rp_ref, rm_ref, convw_ref, wbrb_ref,
                        qt_ref, kvt_ref, kvcmp_ref, kwin_ref, ksel_ref, kwinb_ref, vselt_ref, vwint_ref, gatest_ref,
                        sa_ref, ga_ref, pb_ref, utail_ref, carry_ref):
    ti = pl.program_id(1)
    tm = x_ref.shape[0]
    seg, rope = _proj_common(x_ref, shift_ref, scale_ref, gpre_ref, w_ref, ra_ref, rp_ref, rm_ref)
    for n, head in enumerate(_padded_q_heads(seg, rope, HEAD_DIM ** -0.5 * LOG2_E)):
        qt_ref[n] = head.T.astype(BF16)

    pieces = _kv_pieces(seg, rope)
    pieces_t = [p.T for p in pieces]
    for p in range(4):
        kvt_ref[p * KV_WIDTH:(p + 1) * KV_WIDTH, :] = pieces_t[p]
    kvcmp_ref[:, 0:LANES] = pieces[0]
    kvcmp_ref[:, LANES:2 * LANES] = pieces[1]
    kwin_ref[:, 0:LANES] = pieces[4]
    kwin_ref[:, LANES:2 * LANES] = pieces[5]
    kwinb_ref[...] = pieces[4].astype(BF16)
    vselt_ref[...] = pieces_t[3].astype(BF16)
    vwint_ref[...] = pieces_t[5].astype(BF16)
    row = ti * tm + lax.broadcasted_iota(jnp.int32, (tm, LANES), 0)
    lane = lax.broadcasted_iota(jnp.int32, (tm, LANES), 1)
    onehot = jnp.where(lane == row // SEL_BLOCK, 1.0, 0.0)
    ksel_ref[:, 0:LANES] = pieces[2].astype(BF16)
    ksel_ref[:, LANES:2 * LANES] = onehot.astype(BF16)

    gatest_ref[...] = _sigmoid(seg(C_G, C_G + LANES)).T[0:GATE_ROWS, :]
    sa_ref[...] = _silu(seg(C_A, C_A + ATTN_WIDTH))
    ga_ref[...] = _sigmoid(seg(C_MA, C_MA + D_MODEL))

    @pl.when(ti == 0)
    def _():
        carry_ref[...] = jnp.zeros_like(carry_ref)

    u = seg(C_CC, C_CC + CONV_WIDTH) * seg(C_CX, C_CX + CONV_WIDTH)
    r = lax.broadcasted_iota(jnp.int32, u.shape, 0)
    c7 = carry_ref[SUBLANES - 1:SUBLANES, :]
    c6 = carry_ref[SUBLANES - 2:SUBLANES - 1, :]
    um1 = jnp.where(r == 0, c7, pltpu.roll(u, 1, 0))
    um2 = jnp.where(r == 0, c6, jnp.where(r == 1, c7, pltpu.roll(u, 2, 0)))
    pb_ref[...] = _branch_b(seg, um2, um1, u, convw_ref, wbrb_ref)
    tail = u[tm - SUBLANES:tm, :]
    carry_ref[...] = tail
    utail_ref[0] = tail


def _proj_prompt(x2d, shift, scale, g_pre, w_pad, rope_tabs, conv_w, w_br_b, batch, seq, tm):
    n = batch * seq
    nt = seq // tm
    row = lambda w: pl.BlockSpec((tm, w), lambda b, t: (b * nt + t, 0))
    per_b = lambda w: pl.BlockSpec((None, 1, w), lambda b, t: (b, 0, 0))
    const = lambda shp: pl.BlockSpec(shp, lambda b, t: (0,) * len(shp))
    tab = pl.BlockSpec((tm, LANES), lambda b, t: (t, 0))
    feat = lambda rows: pl.BlockSpec((None, rows, tm), lambda b, t: (b, 0, t))
    out_shapes = (
        jax.ShapeDtypeStruct((batch, N_HEADS, LANES, seq), BF16),
        jax.ShapeDtypeStruct((batch, 4 * KV_WIDTH, seq), F32),
        jax.ShapeDtypeStruct((n, 2 * KV_WIDTH), F32),
        jax.ShapeDtypeStruct((n, 2 * KV_WIDTH), F32),
        jax.ShapeDtypeStruct((n, 2 * LANES), BF16),
        jax.ShapeDtypeStruct((n, KV_WIDTH), BF16),
        jax.ShapeDtypeStruct((batch, KV_WIDTH, seq), BF16),
        jax.ShapeDtypeStruct((batch, KV_WIDTH, seq), BF16),
        jax.ShapeDtypeStruct((batch, GATE_ROWS, seq), F32),
        jax.ShapeDtypeStruct((n, ATTN_WIDTH), F32),
        jax.ShapeDtypeStruct((n, D_MODEL), F32),
        jax.ShapeDtypeStruct((n, D_MODEL), F32),
        jax.ShapeDtypeStruct((batch, SUBLANES, CONV_WIDTH), F32),
    )
    out_specs = (
        pl.BlockSpec((None, N_HEADS, LANES, tm), lambda b, t: (b, 0, 0, t)), feat(4 * KV_WIDTH),
        row(2 * KV_WIDTH), row(2 * KV_WIDTH), row(2 * LANES), row(KV_WIDTH), feat(KV_WIDTH), feat(KV_WIDTH),
        feat(GATE_ROWS), row(ATTN_WIDTH), row(D_MODEL), row(D_MODEL),
        pl.BlockSpec((1, SUBLANES, CONV_WIDTH), lambda b, t: (b, 0, 0)),
    )
    return pl.pallas_call(
        _proj_prompt_kernel,
        grid=(batch, nt),
        in_specs=[row(D_MODEL), per_b(D_MODEL), per_b(D_MODEL), const((1, D_MODEL)), const((D_MODEL, IN_PAD)),
                  tab, tab, tab, const((CONV_K, CONV_WIDTH)), const((CONV_WIDTH, D_MODEL))],
        out_specs=out_specs,
        out_shape=out_shapes,
        scratch_shapes=[pltpu.VMEM((SUBLANES, CONV_WIDTH), F32)],
        compiler_params=pltpu.CompilerParams(dimension_semantics=("arbitrary", "arbitrary"),
                                             vmem_limit_bytes=VMEM_LIMIT),
        name="proj_prompt",
    )(x2d, shift[:, None, :], scale[:, None, :], g_pre, w_pad, *rope_tabs, conv_w, w_br_b)


def _proj_sample_kernel(x_ref, shift_ref, scale_ref, gpre_ref, w_ref, ra_ref, rp_ref, rm_ref, convw_ref, wbrb_ref,
                        cbuf_ref, qpad_ref, kvnew_ref, gates_ref, sa_ref, ga_ref, pb_ref, u_ref):
    seg, rope = _proj_common(x_ref, shift_ref, scale_ref, gpre_ref, w_ref, ra_ref, rp_ref, rm_ref)
    for n, head in enumerate(_padded_q_heads(seg, rope, HEAD_DIM ** -0.5)):
        qpad_ref[:, n * LANES:(n + 1) * LANES] = head.astype(BF16)
    pieces = _kv_pieces(seg, rope)
    for p in range(6):
        kvnew_ref[:, p * LANES:(p + 1) * LANES] = pieces[p]
    gates_ref[...] = _sigmoid(seg(C_G, C_G + LANES))
    sa_ref[...] = _silu(seg(C_A, C_A + ATTN_WIDTH))
    ga_ref[...] = _sigmoid(seg(C_MA, C_MA + D_MODEL))
    u = seg(C_CC, C_CC + CONV_WIDTH) * seg(C_CX, C_CX + CONV_WIDTH)
    um2 = cbuf_ref[:, 0:CONV_WIDTH]
    um1 = cbuf_ref[:, CONV_WIDTH:2 * CONV_WIDTH]
    pb_ref[...] = _branch_b(seg, um2, um1, u, convw_ref, wbrb_ref)
    u_ref[...] = u


def _proj_sample(x2d, shift, scale, g_pre, w_pad, rope_tabs, conv_w, w_br_b, cbuf):
    n = x2d.shape[0]
    full = lambda shp: pl.BlockSpec(shp, lambda i: (0,) * len(shp))
    out_shapes = (
        jax.ShapeDtypeStruct((n, N_HEADS * LANES), BF16),
        jax.ShapeDtypeStruct((n, 6 * KV_WIDTH), F32),
        jax.ShapeDtypeStruct((n, LANES), F32),
        jax.ShapeDtypeStruct((n, ATTN_WIDTH), F32),
        jax.ShapeDtypeStruct((n, D_MODEL), F32),
        jax.ShapeDtypeStruct((n, D_MODEL), F32),
        jax.ShapeDtypeStruct((n, CONV_WIDTH), F32),
    )
    return pl.pallas_call(
        _proj_sample_kernel,
        grid=(1,),
        in_specs=[full((n, D_MODEL)), full((n, D_MODEL)), full((n, D_MODEL)), full((1, D_MODEL)),
                  full((D_MODEL, IN_PAD)), full((1, LANES)), full((1, LANES)), full((1, LANES)),
                  full((CONV_K, CONV_WIDTH)), full((CONV_WIDTH, D_MODEL)), full((n, 2 * CONV_WIDTH))],
        out_specs=tuple(full(s.shape) for s in out_shapes),
        out_shape=out_shapes,
        compiler_params=pltpu.CompilerParams(dimension_semantics=("arbitrary",), vmem_limit_bytes=VMEM_LIMIT),
        name="proj_sample",
    )(x2d, shift, scale, g_pre, w_pad, *rope_tabs, conv_w, w_br_b, cbuf)


CHUNK_LANES = CMP_STRIDE * KV_WIDTH


def _compress_weights(pe_cmp, w_cmp1, w_cmp2):
    zeros = jnp.zeros((2, CMP_STRIDE, HEAD_DIM, CMP_HIDDEN), w_cmp1.dtype)

    def both_groups(w_half):
        w = w_half.reshape(2, CMP_STRIDE, HEAD_DIM, CMP_HIDDEN)
        g0 = jnp.concatenate([w, zeros], axis=2)
        g1 = jnp.concatenate([zeros, w], axis=2)
        return jnp.concatenate([g0, g1], axis=3).reshape(2, CHUNK_LANES, KV_HEADS * CMP_HIDDEN).astype(BF16)

    half = CMP_STRIDE * HEAD_DIM
    z2 = jnp.zeros_like(w_cmp2)
    w2 = jnp.concatenate([jnp.concatenate([w_cmp2, z2], axis=2), jnp.concatenate([z2, w_cmp2], axis=2)], axis=1)
    pe = jnp.concatenate([pe_cmp, pe_cmp], axis=2)
    return dict(w1_lo=both_groups(w_cmp1[:, :half]), w1_hi=both_groups(w_cmp1[:, half:]), w2=w2.astype(BF16),
                pe_lo=pe[:, :CMP_STRIDE].reshape(2, 1, CHUNK_LANES), pe_hi=pe[:, CMP_STRIDE:].reshape(2, 1, CHUNK_LANES))


def _compress_chunks(c, kind, pelo_ref, pehi_ref, w1lo_ref, w1hi_ref):
    lo = _dot((c + pelo_ref[kind]).astype(BF16), w1lo_ref[kind])
    hi = _dot((c + pehi_ref[kind]).astype(BF16), w1hi_ref[kind])
    return lo, hi


def _compress_prompt_kernel(kc_ref, vc_ref, pelo_ref, pehi_ref, w1lo_ref, w1hi_ref, w2_ref, kc_out_ref, vct_out_ref):
    n_chunk = kc_ref.shape[0] // CMP_STRIDE
    outs = []
    for kind, src_ref in enumerate((kc_ref, vc_ref)):
        c = jnp.concatenate([src_ref[pl.ds(r, n_chunk, stride=CMP_STRIDE), :] for r in range(CMP_STRIDE)], axis=1)
        lo, hi = _compress_chunks(c, kind, pelo_ref, pehi_ref, w1lo_ref, w1hi_ref)
        hid = lo + pltpu.roll(hi, n_chunk - 1, 0)
        outs.append(_dot(_silu(hid).astype(BF16), w2_ref[kind]))
    kc_out_ref[0] = outs[0].astype(BF16)
    vct_out_ref[0] = outs[1].T.astype(BF16)


def _compress_prompt(kvc, cw, batch, seq):
    n_chunk = seq // CMP_STRIDE
    const = lambda a: pl.BlockSpec(a.shape, lambda b: (0,) * a.ndim)
    return pl.pallas_call(
        _compress_prompt_kernel,
        grid=(batch,),
        in_specs=[
            pl.BlockSpec((seq, KV_WIDTH), lambda b: (b, 0)),
            pl.BlockSpec((seq, KV_WIDTH), lambda b: (b, 1)),
            const(cw["pe_lo"]), const(cw["pe_hi"]), const(cw["w1_lo"]), const(cw["w1_hi"]), const(cw["w2"]),
        ],
        out_specs=(pl.BlockSpec((1, n_chunk, KV_WIDTH), lambda b: (b, 0, 0)),
                   pl.BlockSpec((1, KV_WIDTH, n_chunk), lambda b: (b, 0, 0))),
        out_shape=(jax.ShapeDtypeStruct((batch, n_chunk, KV_WIDTH), BF16),
                   jax.ShapeDtypeStruct((batch, KV_WIDTH, n_chunk), BF16)),
        compiler_params=pltpu.CompilerParams(dimension_semantics=("arbitrary",), vmem_limit_bytes=VMEM_LIMIT),
        name="compress_prompt",
    )(kvc, kvc, cw["pe_lo"], cw["pe_hi"], cw["w1_lo"], cw["w1_hi"], cw["w2"])


def _top_k_bias(score):
    lane = lax.broadcasted_iota(jnp.int32, score.shape, 1).astype(F32)
    bias = jnp.full(score.shape, NEG_BIG, F32)
    for _ in range(N_SEL):
        m = jnp.max(score, axis=1, keepdims=True)
        idx = jnp.min(jnp.where(score == m, lane, float(LANES)), axis=1, keepdims=True)
        hit = lane == idx
        bias = jnp.where(hit, 0.0, bias)
        score = jnp.where(hit, -jnp.inf, score)
    return bias


def _split_bf16(x):
    hi = x.astype(BF16)
    lo = (x - hi.astype(F32)).astype(BF16)
    return hi, lo


def _masked_softmax_parts(s, valid):
    s = jnp.where(valid, s, -jnp.inf)
    m = jnp.max(s, axis=-1, keepdims=True)
    m = jnp.where(m == -jnp.inf, 0.0, m)
    e = jnp.exp(s - m)
    den = jnp.maximum(jnp.sum(e, axis=-1, keepdims=True), 1e-30)
    return e, den


def _attn_prompt_kernel(qpad_ref, kaug_ref, vsel_ref, kwin_ref, vwin_ref, kvcmp_ref, gates_ref, ovl_ref,
                        o_ref, qaug_ref, m_ref, l_ref, acc_ref, oc_ref, ow_ref):
    qb = pl.program_id(1)
    q0 = qb * Q_BLOCK
    seq = kaug_ref.shape[0]
    n_cmp = kvcmp_ref.shape[1]
    n_rows = N_HEADS * Q_BLOCK
    n_chunks = n_rows // ROW_CHUNK
    heads_per_chunk = ROW_CHUNK // Q_BLOCK

    for n in range(N_HEADS):
        qaug_ref[n * Q_BLOCK:(n + 1) * Q_BLOCK, 0:LANES] = qpad_ref[:, n * LANES:(n + 1) * LANES]

    def qpos_rows(rows):
        r = lax.broadcasted_iota(jnp.int32, (rows, 1), 0)
        return q0 + (r & (Q_BLOCK - 1))

    kc = kvcmp_ref[0, :, 0:LANES]
    vc = kvcmp_ref[0, :, LANES:2 * LANES]
    c_end = lax.broadcasted_iota(jnp.int32, (1, n_cmp), 1) * CMP_STRIDE + (CMP_BLOCK - 1)
    psum = [None] * KV_HEADS
    for rc in range(n_chunks):
        rows = slice(rc * ROW_CHUNK, (rc + 1) * ROW_CHUNK)
        s = _dot_nt(qaug_ref[rows, 0:LANES], kc)
        e, den = _masked_softmax_parts(s, c_end <= qpos_rows(ROW_CHUNK))
        p = e * (1.0 / den)
        oc_ref[rows, :] = _dot(p.astype(BF16), vc)
        g = (rc * heads_per_chunk) // HEADS_PER_GROUP
        part = p[0:Q_BLOCK]
        for h in range(1, heads_per_chunk):
            part = part + p[h * Q_BLOCK:(h + 1) * Q_BLOCK]
        psum[g] = part if psum[g] is None else psum[g] + part

    qpos = q0 + lax.broadcasted_iota(jnp.int32, (Q_BLOCK, 1), 0)
    cur = qpos // SEL_BLOCK
    j = lax.broadcasted_iota(jnp.int32, (Q_BLOCK, LANES), 1)
    forced = (j == 0) | (j == cur) | (j == cur - 1)
    for g in range(KV_HEADS):
        hi, lo = _split_bf16(psum[g])
        imp = _dot(hi, ovl_ref[...]) + _dot(lo, ovl_ref[...])
        score = jnp.where(forced, FORCED_SCORE, jnp.where(j <= cur, imp, -1.0))
        bias = _top_k_bias(score).astype(BF16)
        for h in range(HEADS_PER_GROUP):
            n = g * HEADS_PER_GROUP + h
            qaug_ref[n * Q_BLOCK:(n + 1) * Q_BLOCK, LANES:2 * LANES] = bias

    m_ref[...] = jnp.full(m_ref.shape, NEG_BIG, F32)
    l_ref[...] = jnp.zeros(l_ref.shape, F32)
    acc_ref[...] = jnp.zeros(acc_ref.shape, F32)
    kt_last = (q0 + Q_BLOCK - 1) // KV_TILE

    def sel_tile(kt, causal):
        k0 = pl.multiple_of(kt * KV_TILE, KV_TILE)
        k = kaug_ref[pl.ds(k0, KV_TILE), :]
        v = vsel_ref[pl.ds(k0, KV_TILE), :]
        for rc in range(n_chunks):
            rows = slice(rc * ROW_CHUNK, (rc + 1) * ROW_CHUNK)
            s = _dot_nt(qaug_ref[rows, :], k)
            if causal:
                kpos = k0 + lax.broadcasted_iota(jnp.int32, (1, KV_TILE), 1)
                s = jnp.where(kpos <= qpos_rows(ROW_CHUNK), s, NEG_BIG)
            m_old = m_ref[rows, :]
            m_new = jnp.maximum(m_old, jnp.max(s, axis=-1, keepdims=True))
            alpha = jnp.exp(m_old - m_new)
            p = jnp.exp(s - m_new)
            l_ref[rows, :] = alpha * l_ref[rows, :] + jnp.sum(p, axis=-1, keepdims=True)
            acc_ref[rows, :] = alpha * acc_ref[rows, :] + _dot(p.astype(BF16), v)
            m_ref[rows, :] = m_new

    def body(kt, carry):
        sel_tile(kt, False)
        return carry

    lax.fori_loop(0, kt_last, body, 0)
    sel_tile(kt_last, True)

    wk = kwin_ref.shape[0] if kwin_ref.shape[0] < WINDOW + Q_BLOCK else WINDOW + Q_BLOCK
    start = pl.multiple_of(jnp.maximum(q0 - WINDOW, 0), Q_BLOCK)
    kw = kwin_ref[pl.ds(start, wk), :]
    vw = vwin_ref[pl.ds(start, wk), :]
    kwpos = start + lax.broadcasted_iota(jnp.int32, (1, wk), 1)
    for rc in range(n_chunks):
        rows = slice(rc * ROW_CHUNK, (rc + 1) * ROW_CHUNK)
        s = _dot_nt(qaug_ref[rows, 0:LANES], kw)
        dt = qpos_rows(ROW_CHUNK) - kwpos
        e, den = _masked_softmax_parts(s, (dt >= 0) & (dt < WINDOW))
        ow_ref[rows, :] = _dot(e.astype(BF16), vw) * (1.0 / den)

    lane = lax.broadcasted_iota(jnp.int32, (Q_BLOCK, LANES), 1)
    lower = lane < HEAD_DIM
    gates = gates_ref[...]
    for pair in range(N_HEADS // 2):
        halves = []
        for n in (2 * pair, 2 * pair + 1):
            rows = slice(n * Q_BLOCK, (n + 1) * Q_BLOCK)
            o_sel = acc_ref[rows, :] * (1.0 / l_ref[rows, :])
            o = (gates[:, 3 * n:3 * n + 1] * oc_ref[rows, :] + gates[:, 3 * n + 1:3 * n + 2] * o_sel
                 + gates[:, 3 * n + 2:3 * n + 3] * ow_ref[rows, :])
            halves.append(o)
        g = (2 * pair) // HEADS_PER_GROUP
        if g == 0:
            merged = jnp.where(lower, halves[0], pltpu.roll(halves[1], HEAD_DIM, 1))
        else:
            merged = jnp.where(lower, pltpu.roll(halves[0], HEAD_DIM, 1), halves[1])
        o_ref[:, pair * LANES:(pair + 1) * LANES] = merged


def _attn_prompt(qpad, ksel, kwinb, kvcmp, gates, ovl, batch, seq):
    nq = seq // Q_BLOCK
    n_cmp = kvcmp.shape[1]
    n_rows = N_HEADS * Q_BLOCK
    rowq = lambda w: pl.BlockSpec((Q_BLOCK, w), lambda b, i: (b * nq + i, 0))
    return pl.pallas_call(
        _attn_prompt_kernel,
        grid=(batch, nq),
        in_specs=[
            rowq(N_HEADS * LANES),
            pl.BlockSpec((seq, 2 * LANES), lambda b, i: (b, 0)),
            pl.BlockSpec((seq, LANES), lambda b, i: (b, 2)),
            pl.BlockSpec((seq, LANES), lambda b, i: (b, 0)),
            pl.BlockSpec((seq, LANES), lambda b, i: (b, 1)),
            pl.BlockSpec((1, n_cmp, 2 * LANES), lambda b, i: (b, 0, 0)),
            rowq(LANES),
            pl.BlockSpec((n_cmp, LANES), lambda b, i: (0, 0)),
        ],
        out_specs=rowq(ATTN_WIDTH),
        out_shape=jax.ShapeDtypeStruct((batch * seq, ATTN_WIDTH), F32),
        scratch_shapes=[
            pltpu.VMEM((n_rows, 2 * LANES), BF16),
            pltpu.VMEM((n_rows, 1), F32),
            pltpu.VMEM((n_rows, 1), F32),
            pltpu.VMEM((n_rows, LANES), F32),
            pltpu.VMEM((n_rows, LANES), F32),
            pltpu.VMEM((n_rows, LANES), F32),
        ],
        compiler_params=pltpu.CompilerParams(dimension_semantics=("arbitrary", "arbitrary"),
                                             vmem_limit_bytes=VMEM_LIMIT),
        name="attn_prompt",
    )(qpad, ksel, ksel, kwinb, kwinb, kvcmp, gates, ovl)


COL_CHUNK = 256


def _skewed(n, stages):
    for step in range(n + len(stages) - 1):
        for si, stage in enumerate(stages):
            if 0 <= step - si < n:
                stage(step - si)


def _top_k_bias_t(score):
    blk = lax.broadcasted_iota(jnp.int32, score.shape, 0).astype(F32)
    bias = jnp.full(score.shape, NEG_BIG, F32)
    for _ in range(N_SEL):
        m = jnp.max(score, axis=0, keepdims=True)
        idx = jnp.min(jnp.where(score == m, blk, float(LANES)), axis=0, keepdims=True)
        hit = blk == idx
        bias = jnp.where(hit, 0.0, bias)
        score = jnp.where(hit, -jnp.inf, score)
    return bias


def _masked_softmax_parts_t(s, valid):
    s = jnp.where(valid, s, -jnp.inf)
    m = jnp.max(s, axis=0, keepdims=True)
    m = jnp.where(m == -jnp.inf, 0.0, m)
    e = jnp.exp2(s - m)
    den = jnp.maximum(jnp.sum(e, axis=0, keepdims=True), 1e-30)
    return e, den


def _attn_prompt_t_kernel(qt_ref, kaug_ref, vselt_ref, kwin_ref, vwint_ref, kc_ref, vct_ref, gatest_ref, ovlt_ref,
                          o_ref, qaugt_ref, m_ref, l_ref, acct_ref, oct_ref, owt_ref):
    qb = pl.program_id(1)
    q0 = qb * Q_BLOCK
    n_cmp = kc_ref.shape[1]
    n_cols = N_HEADS * Q_BLOCK
    n_chunks = n_cols // COL_CHUNK
    heads_per_chunk = COL_CHUNK // Q_BLOCK

    for n in range(N_HEADS):
        qaugt_ref[0:LANES, n * Q_BLOCK:(n + 1) * Q_BLOCK] = qt_ref[n]

    def qpos_cols(cols):
        c = lax.broadcasted_iota(jnp.int32, (1, cols), 1)
        return q0 + (c & (Q_BLOCK - 1))

    def chunk_cols(cc):
        return slice(cc * COL_CHUNK, (cc + 1) * COL_CHUNK)

    kc = kc_ref[0]
    vct = vct_ref[0]
    c_end = lax.broadcasted_iota(jnp.int32, (n_cmp, 1), 0) * CMP_STRIDE + (CMP_BLOCK - 1)
    psum = [None] * KV_HEADS
    cs, cp = {}, {}

    def cmp_scores(cc):
        cs[cc] = _dot(kc, qaugt_ref[0:LANES, chunk_cols(cc)])

    def cmp_softmax(cc):
        e, den = _masked_softmax_parts_t(cs.pop(cc), c_end <= qpos_cols(COL_CHUNK))
        p = e * (1.0 / den)
        cp[cc] = p.astype(BF16)
        g = (cc * heads_per_chunk) // HEADS_PER_GROUP
        part = p[:, 0:Q_BLOCK]
        for h in range(1, heads_per_chunk):
            part = part + p[:, h * Q_BLOCK:(h + 1) * Q_BLOCK]
        psum[g] = part if psum[g] is None else psum[g] + part

    def cmp_values(cc):
        oct_ref[:, chunk_cols(cc)] = _dot(vct, cp.pop(cc))

    _skewed(n_chunks, (cmp_scores, cmp_softmax, cmp_values))

    cur = qpos_cols(KV_HEADS * Q_BLOCK) // SEL_BLOCK
    j = lax.broadcasted_iota(jnp.int32, (LANES, KV_HEADS * Q_BLOCK), 0)
    forced = (j == 0) | (j == cur) | (j == cur - 1)
    hi, lo = _split_bf16(jnp.concatenate(psum, axis=1))
    imp = _dot(ovlt_ref[...], hi) + _dot(ovlt_ref[...], lo)
    score = jnp.where(forced, FORCED_SCORE, jnp.where(j <= cur, imp, -1.0))
    bias = _top_k_bias_t(score).astype(BF16)
    for n in range(N_HEADS):
        g = n // HEADS_PER_GROUP
        qaugt_ref[LANES:2 * LANES, n * Q_BLOCK:(n + 1) * Q_BLOCK] = bias[:, g * Q_BLOCK:(g + 1) * Q_BLOCK]

    m_ref[...] = jnp.full(m_ref.shape, NEG_BIG, F32)
    l_ref[...] = jnp.zeros(l_ref.shape, F32)
    acct_ref[...] = jnp.zeros(acct_ref.shape, F32)
    kt_last = (q0 + Q_BLOCK - 1) // KV_TILE

    def sel_tile(kt, causal):
        k0 = pl.multiple_of(kt * KV_TILE, KV_TILE)
        k = kaug_ref[pl.ds(k0, KV_TILE), :]
        vt = vselt_ref[:, pl.ds(k0, KV_TILE)]
        s, pb, alpha = {}, {}, {}

        def scores(cc):
            s[cc] = _dot(k, qaugt_ref[:, chunk_cols(cc)])
            if causal:
                kpos = k0 + lax.broadcasted_iota(jnp.int32, (KV_TILE, 1), 0)
                s[cc] = jnp.where(kpos <= qpos_cols(COL_CHUNK), s[cc], NEG_BIG)

        def softmax(cc):
            cols = chunk_cols(cc)
            m_old = m_ref[:, cols]
            m_new = jnp.maximum(m_old, jnp.max(s[cc], axis=0, keepdims=True))
            alpha[cc] = jnp.exp2(m_old - m_new)
            p = jnp.exp2(s.pop(cc) - m_new)
            l_ref[:, cols] = alpha[cc] * l_ref[:, cols] + jnp.sum(p, axis=0, keepdims=True)
            m_ref[:, cols] = m_new
            pb[cc] = p.astype(BF16)

        def values(cc):
            cols = chunk_cols(cc)
            acct_ref[:, cols] = alpha.pop(cc) * acct_ref[:, cols] + _dot(vt, pb.pop(cc))

        _skewed(n_chunks, (scores, softmax, values))

    def body(kt, carry):
        sel_tile(kt, False)
        return carry

    lax.fori_loop(0, kt_last, body, 0)
    sel_tile(kt_last, True)

    wk = kwin_ref.shape[0] if kwin_ref.shape[0] < WINDOW + Q_BLOCK else WINDOW + Q_BLOCK
    start = pl.multiple_of(jnp.maximum(q0 - WINDOW, 0), Q_BLOCK)
    kw = kwin_ref[pl.ds(start, wk), :]
    vwt = vwint_ref[:, pl.ds(start, wk)]
    kwpos = start + lax.broadcasted_iota(jnp.int32, (wk, 1), 0)
    ws, we, wden = {}, {}, {}

    def win_scores(cc):
        ws[cc] = _dot(kw, qaugt_ref[0:LANES, chunk_cols(cc)])

    def win_softmax(cc):
        dt = qpos_cols(COL_CHUNK) - kwpos
        e, wden[cc] = _masked_softmax_parts_t(ws.pop(cc), (dt >= 0) & (dt < WINDOW))
        we[cc] = e.astype(BF16)

    def win_values(cc):
        owt_ref[:, chunk_cols(cc)] = _dot(vwt, we.pop(cc)) * (1.0 / wden.pop(cc))

    _skewed(n_chunks, (win_scores, win_softmax, win_values))

    gates = gatest_ref[...]
    for pair in range(N_HEADS // 2):
        g = (2 * pair) // HEADS_PER_GROUP
        feat = slice(g * HEAD_DIM, (g + 1) * HEAD_DIM)
        halves = []
        for n in (2 * pair, 2 * pair + 1):
            cols = slice(n * Q_BLOCK, (n + 1) * Q_BLOCK)
            o_sel = acct_ref[feat, cols] * (1.0 / l_ref[:, cols])
            halves.append(gates[3 * n:3 * n + 1, :] * oct_ref[feat, cols] + gates[3 * n + 1:3 * n + 2, :] * o_sel
                          + gates[3 * n + 2:3 * n + 3, :] * owt_ref[feat, cols])
        o_ref[:, pair * LANES:(pair + 1) * LANES] = jnp.concatenate(halves, axis=0).T


def _attn_prompt_t(qt, ksel, kwinb, vselt, vwint, kc, vct, gatest, ovlt, batch, seq):
    nq = seq // Q_BLOCK
    n_cmp = kc.shape[1]
    n_cols = N_HEADS * Q_BLOCK
    per_b = lambda a: pl.BlockSpec((None,) + a.shape[1:], lambda b, i: (b,) + (0,) * (a.ndim - 1))
    return pl.pallas_call(
        _attn_prompt_t_kernel,
        grid=(batch, nq),
        in_specs=[
            pl.BlockSpec((None, N_HEADS, LANES, Q_BLOCK), lambda b, i: (b, 0, 0, i)),
            pl.BlockSpec((seq, 2 * LANES), lambda b, i: (b, 0)),
            per_b(vselt),
            pl.BlockSpec((seq, LANES), lambda b, i: (b, 0)),
            per_b(vwint),
            pl.BlockSpec((1, n_cmp, LANES), lambda b, i: (b, 0, 0)),
            pl.BlockSpec((1, LANES, n_cmp), lambda b, i: (b, 0, 0)),
            pl.BlockSpec((None, GATE_ROWS, Q_BLOCK), lambda b, i: (b, 0, i)),
            pl.BlockSpec((LANES, n_cmp), lambda b, i: (0, 0)),
        ],
        out_specs=pl.BlockSpec((Q_BLOCK, ATTN_WIDTH), lambda b, i: (b * nq + i, 0)),
        out_shape=jax.ShapeDtypeStruct((batch * seq, ATTN_WIDTH), F32),
        scratch_shapes=[
            pltpu.VMEM((2 * LANES, n_cols), BF16),
            pltpu.VMEM((1, n_cols), F32),
            pltpu.VMEM((1, n_cols), F32),
            pltpu.VMEM((LANES, n_cols), F32),
            pltpu.VMEM((LANES, n_cols), F32),
            pltpu.VMEM((LANES, n_cols), F32),
        ],
        compiler_params=pltpu.CompilerParams(dimension_semantics=("arbitrary", "arbitrary"),
                                             vmem_limit_bytes=VMEM_LIMIT),
        name="attn_prompt",
    )(qt, ksel, vselt, kwinb, vwint, kc, vct, gatest, ovlt)


def _finish_kernel(x_ref, o_ref, sa_ref, ga_ref, pb_ref, gate_ref, gpost_ref, wbra_ref, wout_ref, y_ref):
    ya = _dot((o_ref[...] * sa_ref[...]).astype(BF16), wbra_ref[...])
    mix = ga_ref[...] * ya + pb_ref[...]
    o = _dot(mix.astype(BF16), wout_ref[...])
    ms = jnp.mean(o * o, axis=-1, keepdims=True)
    on = o * lax.rsqrt(ms + RMS_EPS) * gpost_ref[...]
    y_ref[...] = x_ref[...] + gate_ref[...] * on


def _finish(x2d, o_attn, sa, ga, pb, gate, g_post, w_br_a, w_out, tm, rows_per_gate):
    n = x2d.shape[0]
    row = lambda w: pl.BlockSpec((tm, w), lambda i: (i, 0))
    const = lambda shp: pl.BlockSpec(shp, lambda i: (0,) * len(shp))
    if rows_per_gate == 1:
        gate_spec = row(D_MODEL)
    else:
        tiles_per_gate = rows_per_gate // tm
        gate = gate[:, None, :]
        gate_spec = pl.BlockSpec((None, 1, D_MODEL), lambda i: (i // tiles_per_gate, 0, 0))
    return pl.pallas_call(
        _finish_kernel,
        grid=(n // tm,),
        in_specs=[row(D_MODEL), row(ATTN_WIDTH), row(ATTN_WIDTH), row(D_MODEL), row(D_MODEL), gate_spec,
                  const((1, D_MODEL)), const((ATTN_WIDTH, D_MODEL)), const((D_MODEL, D_MODEL))],
        out_specs=row(D_MODEL),
        out_shape=jax.ShapeDtypeStruct((n, D_MODEL), F32),
        compiler_params=pltpu.CompilerParams(dimension_semantics=("arbitrary",), vmem_limit_bytes=VMEM_LIMIT),
        name="finish",
    )(x2d, o_attn, sa, ga, pb, gate, g_post, w_br_a, w_out)


def _attn_decode_kernel(pt_ref, *refs, n_req, n_pages, page_size, wbuf):
    del pt_ref
    page_refs = refs[:n_req * n_pages]
    (win_ref, q_ref, kvnew_ref, gates_ref, pelo_ref, pehi_ref, w1lo_ref, w1hi_ref, w2_ref, ovl_ref, expand_ref,
     o_ref, winout_ref, kcmp_ref, vcmp_ref) = refs[n_req * n_pages:]
    past_len = n_pages * page_size
    n_chunk = past_len // CMP_STRIDE
    reqs = range(n_req)
    heads = lambda b: slice(b * N_HEADS, (b + 1) * N_HEADS)
    per_req = lambda fn: jnp.concatenate([fn(b) for b in reqs], axis=0)
    q8f = per_req(lambda b: q_ref[b])
    q8 = q8f.astype(BF16)
    new_row = lambda which: per_req(lambda b: jnp.broadcast_to(kvnew_ref[b, which:which + 1, :], (N_HEADS, LANES)))

    def stream(ref, which):
        slab = ref[which * KV_HEADS:(which + 1) * KV_HEADS]
        return slab.reshape(KV_WIDTH, slab.shape[-1])

    for i, pr in enumerate(page_refs):
        rows = slice(i * page_size, (i + 1) * page_size)
        kcmp_ref[rows, :] = stream(pr, 0).T
        vcmp_ref[rows, :] = stream(pr, 1).T

    kvc = []
    for kind, src_ref in enumerate((kcmp_ref, vcmp_ref)):
        c = jnp.concatenate([src_ref[pl.ds(r, n_req * n_chunk, stride=CMP_STRIDE), :] for r in range(CMP_STRIDE)],
                            axis=1)
        lo, hi = _compress_chunks(c, kind, pelo_ref, pehi_ref, w1lo_ref, w1hi_ref)
        hid = lo + pltpu.roll(hi, n_req * n_chunk - 1, 0)
        kvc.append(_dot(_silu(hid).astype(BF16), w2_ref[kind]).astype(BF16))
    kc, vc = kvc
    chunks = lambda b: slice(b * n_chunk, (b + 1) * n_chunk)

    c_end = lax.broadcasted_iota(jnp.int32, (1, n_chunk), 1) * CMP_STRIDE + (CMP_BLOCK - 1)
    e, den = _masked_softmax_parts(per_req(lambda b: _dot_nt(q8[heads(b)], kc[chunks(b)])), c_end <= past_len)
    p = e * (1.0 / den)
    pb = p.astype(BF16)
    o_cmp = per_req(lambda b: _dot(pb[heads(b)], vc[chunks(b)]))

    row = lax.broadcasted_iota(jnp.int32, p.shape, 0)
    in_g0 = (row & (N_HEADS - 1)) < HEADS_PER_GROUP

    def group_sums(b):
        pr_, g0_ = p[heads(b)], in_g0[heads(b)]
        g0 = jnp.sum(jnp.where(g0_, pr_, 0.0), axis=0, keepdims=True)
        g1 = jnp.sum(jnp.where(g0_, 0.0, pr_), axis=0, keepdims=True)
        return jnp.where(g0_, g0, g1)

    hi, lo = _split_bf16(per_req(group_sums))
    imp = _dot(hi, ovl_ref[...]) + _dot(lo, ovl_ref[...])
    cur = past_len // SEL_BLOCK
    j = lax.broadcasted_iota(jnp.int32, imp.shape, 1)
    forced = (j == 0) | (j == cur) | (j == cur - 1)
    score = jnp.where(j > cur, -jnp.inf, jnp.where(forced, FORCED_SCORE, imp))
    bias_keys = _dot(_top_k_bias_t(score.T).T.astype(BF16), expand_ref[...])

    pages = lambda b: page_refs[b * n_pages:(b + 1) * n_pages]
    s = per_req(lambda b: jnp.concatenate([_dot(q8[heads(b)], stream(pr, 2).astype(BF16)) for pr in pages(b)],
                                          axis=1)) + bias_keys
    s_new = jnp.sum(q8f * new_row(2), axis=1, keepdims=True)
    m = jnp.maximum(jnp.max(s, axis=1, keepdims=True), s_new)
    e = jnp.exp(s - m)
    e_new = jnp.exp(s_new - m)
    den = jnp.sum(e, axis=1, keepdims=True) + e_new
    eb = e.astype(BF16)

    def sel_values(b):
        acc = None
        for i, pr in enumerate(pages(b)):
            part = _dot_nt(eb[heads(b), i * page_size:(i + 1) * page_size], stream(pr, 3).astype(BF16))
            acc = part if acc is None else acc + part
        return acc

    o_sel = (per_req(sel_values) + e_new * new_row(3)) * (1.0 / den)

    kwpos = past_len - wbuf + lax.broadcasted_iota(jnp.int32, (1, wbuf), 1)
    dt = past_len - kwpos
    valid = (dt >= 0) & (dt < WINDOW) & (kwpos >= 0)
    kw_t = [stream(win_ref.at[b], 0) for b in reqs]
    vw_t = [stream(win_ref.at[b], 1) for b in reqs]
    s = jnp.where(valid, per_req(lambda b: _dot(q8[heads(b)], kw_t[b].astype(BF16))), -jnp.inf)
    s_new = jnp.sum(q8f * new_row(4), axis=1, keepdims=True)
    m = jnp.maximum(jnp.max(s, axis=1, keepdims=True), s_new)
    e = jnp.exp(s - m)
    e_new = jnp.exp(s_new - m)
    den = jnp.sum(e, axis=1, keepdims=True) + e_new
    eb = e.astype(BF16)
    o_win = (per_req(lambda b: _dot_nt(eb[heads(b)], vw_t[b].astype(BF16))) + e_new * new_row(5)) * (1.0 / den)

    gates = per_req(lambda b: gates_ref[b])
    o = gates[:, 0:1] * o_cmp + gates[:, 1:2] * o_sel + gates[:, 2:3] * o_win
    o = jnp.where(in_g0, o, pltpu.roll(o, HEAD_DIM, 1))
    lane = lax.broadcasted_iota(jnp.int32, (KV_WIDTH, wbuf), 1)
    for b in reqs:
        o_ref[b] = o[heads(b)]
        new_cols = jnp.concatenate([kvnew_ref[b], jnp.zeros((SUBLANES - 6, LANES), F32)], axis=0).T
        for kind, old in enumerate((kw_t[b], vw_t[b])):
            shifted = jnp.where(lane == wbuf - 1, new_cols[:, 4 + kind:5 + kind], pltpu.roll(old, wbuf - 1, 1))
            winout_ref[b, kind * KV_HEADS:(kind + 1) * KV_HEADS] = shifted.reshape(KV_HEADS, HEAD_DIM, wbuf)


DECODE_REQS_PER_STEP = 2


def _attn_decode(cache_t, page_table, win_t, q8, kvnew, gates8, cw, ovl, expand):
    page_size = cache_t.shape[-1]
    batch, n_pages = page_table.shape
    wbuf = win_t.shape[-1]
    past_len = n_pages * page_size
    n_req = DECODE_REQS_PER_STEP if batch % DECODE_REQS_PER_STEP == 0 else 1

    def page_spec(r, k):
        return pl.BlockSpec((None, PAGE_ROWS, HEAD_DIM, page_size), lambda i, pt: (pt[i * n_req + r, k], 0, 0, 0))

    per_step = lambda a: pl.BlockSpec((n_req,) + a.shape[1:], lambda i, pt: (i,) + (0,) * (a.ndim - 1))
    const = lambda a: pl.BlockSpec(a.shape, lambda i, pt: (0,) * a.ndim, pipeline_mode=pl.Buffered(1))
    consts = [cw["pe_lo"], cw["pe_hi"], cw["w1_lo"], cw["w1_hi"], cw["w2"], ovl, expand]
    grid_spec = pltpu.PrefetchScalarGridSpec(
        num_scalar_prefetch=1,
        grid=(batch // n_req,),
        in_specs=[page_spec(r, k) for r in range(n_req) for k in range(n_pages)]
        + [per_step(win_t), per_step(q8), per_step(kvnew), per_step(gates8)]
        + [const(a) for a in consts],
        out_specs=(pl.BlockSpec((n_req, N_HEADS, LANES), lambda i, pt: (i, 0, 0)), per_step(win_t)),
        scratch_shapes=[pltpu.VMEM((n_req * past_len, LANES), F32), pltpu.VMEM((n_req * past_len, LANES), F32)],
    )
    o8, win_out = pl.pallas_call(
        functools.partial(_attn_decode_kernel, n_req=n_req, n_pages=n_pages, page_size=page_size, wbuf=wbuf),
        grid_spec=grid_spec,
        out_shape=(jax.ShapeDtypeStruct((batch, N_HEADS, LANES), F32),
                   jax.ShapeDtypeStruct(win_t.shape, F32)),
        compiler_params=pltpu.CompilerParams(dimension_semantics=("arbitrary",), vmem_limit_bytes=VMEM_LIMIT),
        name="attn_decode",
    )(page_table, *([cache_t] * (n_req * n_pages)), win_t, q8, kvnew, gates8, *consts)
    return o8, win_out


def _overlap_matrix(n_cmp_pad, n_cmp, n_selb):
    cs = jnp.arange(n_cmp_pad) * CMP_STRIDE
    ss = jnp.arange(LANES) * SEL_BLOCK
    ov = (cs[:, None] < ss[None, :] + SEL_BLOCK) & (cs[:, None] + CMP_BLOCK > ss[None, :])
    ov = ov & (jnp.arange(n_cmp_pad) < n_cmp)[:, None] & (jnp.arange(LANES) < n_selb)[None, :]
    return ov.astype(BF16)


def _prompt_layer(x, mod, wts):
    batch, seq, _ = x.shape
    x2d = x.reshape(batch * seq, D_MODEL)
    shift, scale, gate = mod[:, 0:D_MODEL], mod[:, D_MODEL:2 * D_MODEL], mod[:, 2 * D_MODEL:]
    tabs = _rope_tables(jnp.arange(seq))
    tm = min(256, seq)
    (qt, kvt, kvc, kwin, ksel, kwinb, vselt, vwint, gatest, sa, ga, pb, utail) = _proj_prompt(
        x2d, shift, scale, wts["g_pre"], wts["w_in"], tabs, wts["conv_w"], wts["w_br_b"], batch, seq, tm)
    kc, vct = _compress_prompt(kvc, wts["cmp"], batch, seq)
    n_chunk = seq // CMP_STRIDE
    ovlt = _overlap_matrix(n_chunk, n_chunk - 1, -(-seq // SEL_BLOCK)).T
    o_attn = _attn_prompt_t(qt, ksel, kwinb, vselt, vwint, kc, vct, gatest, ovlt, batch, seq)
    y = _finish(x2d, o_attn, sa, ga, pb, gate, wts["g_post"], wts["w_br_a"], wts["w_out"], tm, seq)
    n_keep = min(WINDOW, seq)
    return (y.reshape(batch, seq, D_MODEL),
            jnp.transpose(kvt.reshape(batch, 4, KV_HEADS, HEAD_DIM, seq), (0, 4, 1, 2, 3)),
            kwin.reshape(batch, seq, 2, KV_HEADS, HEAD_DIM)[:, seq - n_keep:],
            utail[:, SUBLANES - (CONV_K - 1):])


def _sample_layer(x, mod, cache, page_table, win, conv_state, wts):
    batch, dec_seq, _ = x.shape
    assert dec_seq == 1
    n_pages, page_size, wbuf = page_table.shape[1], cache.shape[1], win.shape[1]
    past_len = n_pages * page_size
    assert past_len % SEL_BLOCK == 0 and wbuf == WINDOW and past_len // SEL_BLOCK < LANES
    x2d = x.reshape(batch, D_MODEL)
    shift, scale, gate = mod[:, 0:D_MODEL], mod[:, D_MODEL:2 * D_MODEL], mod[:, 2 * D_MODEL:]
    tabs = _rope_tables(jnp.full((1,), past_len, jnp.int32))
    cbuf = conv_state.reshape(batch, (CONV_K - 1) * CONV_WIDTH)
    qpad, kvnew, gates, sa, ga, pb, u = _proj_sample(
        x2d, shift, scale, wts["g_pre"], wts["w_in"], tabs, wts["conv_w"], wts["w_br_b"], cbuf)
    n_gate = N_HEADS * 3
    gates8 = jnp.pad(gates[:, :n_gate].reshape(batch, N_HEADS, 3), ((0, 0), (0, 0), (0, LANES - 3)))
    n_chunk = past_len // CMP_STRIDE
    ovl = _overlap_matrix(n_chunk, n_chunk - 1, past_len // SEL_BLOCK + 1)
    expand = (jnp.arange(LANES)[:, None] == (jnp.arange(past_len) // SEL_BLOCK)[None, :]).astype(BF16)
    cache_t = jnp.transpose(cache, (0, 2, 3, 4, 1)).reshape(cache.shape[0], PAGE_ROWS, HEAD_DIM, page_size)
    win_t = jnp.transpose(win, (0, 2, 3, 4, 1)).reshape(batch, WIN_ROWS, HEAD_DIM, wbuf)
    o8, win_out = _attn_decode(cache_t, page_table, win_t, qpad.reshape(batch, N_HEADS, LANES).astype(F32),
                               kvnew.reshape(batch, 6, LANES), gates8, wts["cmp"], ovl, expand)
    win_out = jnp.transpose(win_out.reshape(batch, 2, KV_HEADS, HEAD_DIM, wbuf), (0, 4, 1, 2, 3))
    y = _finish(x2d, o8[:, :, :HEAD_DIM].reshape(batch, ATTN_WIDTH), sa, ga, pb, gate, wts["g_post"], wts["w_br_a"],
                wts["w_out"], batch, 1)
    return (y.reshape(batch, 1, D_MODEL),
            kvnew[:, :4 * KV_WIDTH].reshape(batch, 1, 4, KV_HEADS, HEAD_DIM),
            win_out,
            jnp.stack([conv_state[:, CONV_K - 2], u], axis=1))


def _prep_weights(w_ada, b_ada, g_pre, g_post, w_in, pe_cmp, w_cmp1, w_cmp2, conv_w, w_br_a, w_br_b, w_out):
    n_unpadded_gate = HEADS_PER_GROUP * KV_HEADS * 3
    w_pad = jnp.concatenate(
        [w_in[:, :C_G + n_unpadded_gate], jnp.zeros((D_MODEL, LANES - n_unpadded_gate), w_in.dtype),
         w_in[:, C_G + n_unpadded_gate:]], axis=1).astype(BF16)
    half = CMP_STRIDE * HEAD_DIM
    return dict(
        w_ada=w_ada, b_ada=b_ada, g_pre=g_pre.reshape(1, -1), g_post=g_post.reshape(1, -1), w_in=w_pad,
        cmp=_compress_weights(pe_cmp, w_cmp1, w_cmp2), conv_w=conv_w,
        w_br_a=w_br_a.astype(BF16), w_br_b=w_br_b.astype(BF16), w_out=w_out.astype(BF16))


def kernel(x_prompt, x_sample, cache_kv_pages, state_win_kv, state_conv, page_table, c_prompt, c_sample, w_ada, b_ada, g_pre, g_post, w_in, pe_cmp, w_cmp1, w_cmp2, conv_w, w_br_a, w_br_b, w_out):
    depth = w_in.shape[0]
    assert depth == 1
    wts = _prep_weights(w_ada[0], b_ada[0], g_pre[0], g_post[0], w_in[0], pe_cmp[0], w_cmp1[0], w_cmp2[0],
                        conv_w[0], w_br_a[0], w_br_b[0], w_out[0])
    n_prompt = c_prompt.shape[0]
    mod = _ada(jnp.concatenate([c_prompt, c_sample], axis=0), wts["w_ada"], wts["b_ada"])
    yp, kvp, wp, cp = _prompt_layer(x_prompt, mod[:n_prompt], wts)
    ys, kvs, ws, cs = _sample_layer(x_sample, mod[n_prompt:], cache_kv_pages[0], page_table, state_win_kv[0],
                                    state_conv[0], wts)
    return (yp, ys, kvp[None], wp[None], cp[None], kvs[None], ws[None], cs[None])
```

```python
import functools

import jax
import jax.numpy as jnp
from jax import lax
from jax.experimental import pallas as pl
from jax.experimental.pallas import tpu as pltpu

F32 = jnp.float32
BF16 = jnp.bfloat16

D_MODEL = 1024
N_HEADS = 8
KV_HEADS = 2
HEADS_PER_GROUP = N_HEADS // KV_HEADS
HEAD_DIM = 64
ROPE_DIM = HEAD_DIM // 4
ROPE_THETA = 500000.0
CMP_BLOCK = 32
CMP_STRIDE = 16
CMP_HIDDEN = 256
SEL_BLOCK = 64
N_SEL = 16
WINDOW = 512
Q_BLOCK = 128
FORCED_SCORE = 1e6
CONV_WIDTH = D_MODEL // 2
CONV_K = 3
ATTN_WIDTH = N_HEADS * HEAD_DIM
KV_WIDTH = KV_HEADS * HEAD_DIM
RMS_EPS = 1e-6

LANES = 128
SUBLANES = 8
VMEM_LIMIT = 56 * 1024 * 1024

C_Q = 0
C_KV = C_Q + ATTN_WIDTH
C_G = C_KV + 6 * KV_WIDTH
C_A = C_G + LANES
C_CB = C_A + ATTN_WIDTH
C_CC = C_CB + CONV_WIDTH
C_CX = C_CC + CONV_WIDTH
C_CG = C_CX + CONV_WIDTH
C_MA = C_CG + CONV_WIDTH
C_MB = C_MA + D_MODEL
IN_PAD = C_MB + D_MODEL

PAGE_ROWS = 4 * KV_HEADS
WIN_ROWS = 2 * KV_HEADS

GATE_ROWS = 32
LOG2_E = 1.4426950408889634

NEG_BIG = -1e30
KV_TILE = 512
ROW_CHUNK = 256


def _sigmoid(x):
    return 1.0 / (1.0 + jnp.exp(-x))


def _silu(x):
    return x * _sigmoid(x)


def _dot(a, b):
    return jnp.dot(a, b, preferred_element_type=F32)


def _dot_nt(a, b):
    return lax.dot_general(a, b, (((1,), (1,)), ((), ())), preferred_element_type=F32)


def _rope_tables(pos):
    half = ROPE_DIM // 2
    inv = ROPE_THETA ** (-jnp.arange(half, dtype=F32) / half)
    ang = pos.astype(F32)[:, None] * inv[None, :]
    cos, sin = jnp.cos(ang), jnp.sin(ang)
    n = pos.shape[0]
    a = jnp.concatenate([cos, cos, jnp.ones((n, HEAD_DIM - ROPE_DIM), F32)], axis=1)
    p = jnp.concatenate([jnp.zeros((n, half), F32), sin, jnp.zeros((n, HEAD_DIM - ROPE_DIM), F32)], axis=1)
    m = jnp.concatenate([-sin, jnp.zeros((n, HEAD_DIM - half), F32)], axis=1)
    tile = lambda t: jnp.concatenate([t, t], axis=1)
    return tile(a), tile(p), tile(m)


def _rope(x, ra, rp, rm):
    half = ROPE_DIM // 2
    return x * ra + pltpu.roll(x, half, 1) * rp + pltpu.roll(x, LANES - half, 1) * rm


def _ada_kernel(c_ref, w_ref, b_ref, o_ref):
    c = _silu(c_ref[...]).astype(BF16)
    o_ref[...] = _dot(c, w_ref[...].astype(BF16)) + b_ref[...]


def _ada(c_all, w_ada, b_ada):
    n = c_all.shape[0]
    tn = 512
    return pl.pallas_call(
        _ada_kernel,
        grid=(3 * D_MODEL // tn,),
        in_specs=[
            pl.BlockSpec((n, D_MODEL), lambda j: (0, 0)),
            pl.BlockSpec((D_MODEL, tn), lambda j: (0, j)),
            pl.BlockSpec((1, tn), lambda j: (0, j)),
        ],
        out_specs=pl.BlockSpec((n, tn), lambda j: (0, j)),
        out_shape=jax.ShapeDtypeStruct((n, 3 * D_MODEL), F32),
        compiler_params=pltpu.CompilerParams(dimension_semantics=("arbitrary",), vmem_limit_bytes=VMEM_LIMIT),
        name="ada",
    )(c_all, w_ada, b_ada.reshape(1, -1))


def _proj_common(x_ref, shift_ref, scale_ref, gpre_ref, w_ref, ra_ref, rp_ref, rm_ref):
    x = x_ref[...]
    ms = jnp.mean(x * x, axis=-1, keepdims=True)
    xn = x * lax.rsqrt(ms + RMS_EPS) * gpre_ref[...]
    h = xn * (1.0 + scale_ref[...]) + shift_ref[...]
    hb = h.astype(BF16)
    ra, rp, rm = ra_ref[...], rp_ref[...], rm_ref[...]

    def seg(lo, hi):
        return _dot(hb, w_ref[:, lo:hi])

    return seg, (ra, rp, rm)


def _padded_q_heads(seg, rope, scale):
    zq = seg(C_Q, C_Q + ATTN_WIDTH)
    lane = lax.broadcasted_iota(jnp.int32, (zq.shape[0], LANES), 1)
    lower = lane < HEAD_DIM
    heads = []
    for j in range(ATTN_WIDTH // LANES):
        c = _rope(zq[:, j * LANES:(j + 1) * LANES], *rope) * scale
        r = pltpu.roll(c, HEAD_DIM, 1)
        if (2 * j) // HEADS_PER_GROUP == 0:
            heads += [jnp.where(lower, c, 0.0), jnp.where(lower, r, 0.0)]
        else:
            heads += [jnp.where(lower, 0.0, r), jnp.where(lower, 0.0, c)]
    return heads


def _kv_pieces(seg, rope):
    zkv = seg(C_KV, C_KV + 6 * KV_WIDTH)
    pieces = []
    for p in range(6):
        c = zkv[:, p * LANES:(p + 1) * LANES]
        pieces.append(_rope(c, *rope) if p % 2 == 0 else c)
    return pieces


def _branch_b(seg, um2, um1, u, convw_ref, wbrb_ref):
    cb = seg(C_CB, C_CB + CONV_WIDTH)
    conv = convw_ref[0:1, :] * um2
    conv = conv + convw_ref[1:2, :] * um1
    conv = conv + convw_ref[2:3, :] * u
    ybin = cb * conv * _silu(seg(C_CG, C_CG + CONV_WIDTH))
    yb = _dot(ybin.astype(BF16), wbrb_ref[...])
    gb = _sigmoid(seg(C_MB, C_MB + D_MODEL))
    return gb * yb


def _proj_prompt_kernel(x_ref, shift_ref, scale_ref, gpre_ref, w_ref, ra_ref, rp_ref, rm_ref, convw_ref, wbrb_ref,
                        qt_ref, kvt_ref, kvcmp_ref, kwin_ref, ksel_ref, kwinb_ref, vselt_ref, vwint_ref, gatest_ref,
                        sa_ref, ga_ref, pb_ref, utail_ref, carry_ref):
    ti = pl.program_id(1)
    tm = x_ref.shape[0]
    seg, rope = _proj_common(x_ref, shift_ref, scale_ref, gpre_ref, w_ref, ra_ref, rp_ref, rm_ref)
    for n, head in enumerate(_padded_q_heads(seg, rope, HEAD_DIM ** -0.5 * LOG2_E)):
        qt_ref[n] = head.T.astype(BF16)

    pieces = _kv_pieces(seg, rope)
    pieces_t = [p.T for p in pieces]
    for p in range(4):
        kvt_ref[p * KV_WIDTH:(p + 1) * KV_WIDTH, :] = pieces_t[p]
    kvcmp_ref[:, 0:LANES] = pieces[0]
    kvcmp_ref[:, LANES:2 * LANES] = pieces[1]
    kwin_ref[:, 0:LANES] = pieces[4]
    kwin_ref[:, LANES:2 * LANES] = pieces[5]
    kwinb_ref[...] = pieces[4].astype(BF16)
    vselt_ref[...] = pieces_t[3].astype(BF16)
    vwint_ref[...] = pieces_t[5].astype(BF16)
    row = ti * tm + lax.broadcasted_iota(jnp.int32, (tm, LANES), 0)
    lane = lax.broadcasted_iota(jnp.int32, (tm, LANES), 1)
    onehot = jnp.where(lane == row // SEL_BLOCK, 1.0, 0.0)
    ksel_ref[:, 0:LANES] = pieces[2].astype(BF16)
    ksel_ref[:, LANES:2 * LANES] = onehot.astype(BF16)

    gatest_ref[...] = _sigmoid(seg(C_G, C_G + LANES)).T[0:GATE_ROWS, :]
    sa_ref[...] = _silu(seg(C_A, C_A + ATTN_WIDTH))
    ga_ref[...] = _sigmoid(seg(C_MA, C_MA + D_MODEL))

    @pl.when(ti == 0)
    def _():
        carry_ref[...] = jnp.zeros_like(carry_ref)

    u = seg(C_CC, C_CC + CONV_WIDTH) * seg(C_CX, C_CX + CONV_WIDTH)
    r = lax.broadcasted_iota(jnp.int32, u.shape, 0)
    c7 = carry_ref[SUBLANES - 1:SUBLANES, :]
    c6 = carry_ref[SUBLANES - 2:SUBLANES - 1, :]
    um1 = jnp.where(r == 0, c7, pltpu.roll(u, 1, 0))
    um2 = jnp.where(r == 0, c6, jnp.where(r == 1, c7, pltpu.roll(u, 2, 0)))
    pb_ref[...] = _branch_b(seg, um2, um1, u, convw_ref, wbrb_ref)
    tail = u[tm - SUBLANES:tm, :]
    carry_ref[...] = tail
    utail_ref[0] = tail


def _proj_prompt(x2d, shift, scale, g_pre, w_pad, rope_tabs, conv_w, w_br_b, batch, seq, tm):
    n = batch * seq
    nt = seq // tm
    row = lambda w: pl.BlockSpec((tm, w), lambda b, t: (b * nt + t, 0))
    per_b = lambda w: pl.BlockSpec((None, 1, w), lambda b, t: (b, 0, 0))
    const = lambda shp: pl.BlockSpec(shp, lambda b, t: (0,) * len(shp))
    tab = pl.BlockSpec((tm, LANES), lambda b, t: (t, 0))
    feat = lambda rows: pl.BlockSpec((None, rows, tm), lambda b, t: (b, 0, t))
    out_shapes = (
        jax.ShapeDtypeStruct((batch, N_HEADS, LANES, seq), BF16),
        jax.ShapeDtypeStruct((batch, 4 * KV_WIDTH, seq), F32),
        jax.ShapeDtypeStruct((n, 2 * KV_WIDTH), F32),
        jax.ShapeDtypeStruct((n, 2 * KV_WIDTH), F32),
        jax.ShapeDtypeStruct((n, 2 * LANES), BF16),
        jax.ShapeDtypeStruct((n, KV_WIDTH), BF16),
        jax.ShapeDtypeStruct((batch, KV_WIDTH, seq), BF16),
        jax.ShapeDtypeStruct((batch, KV_WIDTH, seq), BF16),
        jax.ShapeDtypeStruct((batch, GATE_ROWS, seq), F32),
        jax.ShapeDtypeStruct((n, ATTN_WIDTH), F32),
        jax.ShapeDtypeStruct((n, D_MODEL), F32),
        jax.ShapeDtypeStruct((n, D_MODEL), F32),
        jax.ShapeDtypeStruct((batch, SUBLANES, CONV_WIDTH), F32),
    )
    out_specs = (
        pl.BlockSpec((None, N_HEADS, LANES, tm), lambda b, t: (b, 0, 0, t)), feat(4 * KV_WIDTH),
        row(2 * KV_WIDTH), row(2 * KV_WIDTH), row(2 * LANES), row(KV_WIDTH), feat(KV_WIDTH), feat(KV_WIDTH),
        feat(GATE_ROWS), row(ATTN_WIDTH), row(D_MODEL), row(D_MODEL),
        pl.BlockSpec((1, SUBLANES, CONV_WIDTH), lambda b, t: (b, 0, 0)),
    )
    return pl.pallas_call(
        _proj_prompt_kernel,
        grid=(batch, nt),
        in_specs=[row(D_MODEL), per_b(D_MODEL), per_b(D_MODEL), const((1, D_MODEL)), const((D_MODEL, IN_PAD)),
                  tab, tab, tab, const((CONV_K, CONV_WIDTH)), const((CONV_WIDTH, D_MODEL))],
        out_specs=out_specs,
        out_shape=out_shapes,
        scratch_shapes=[pltpu.VMEM((SUBLANES, CONV_WIDTH), F32)],
        compiler_params=pltpu.CompilerParams(dimension_semantics=("arbitrary", "arbitrary"),
                                             vmem_limit_bytes=VMEM_LIMIT),
        name="proj_prompt",
    )(x2d, shift[:, None, :], scale[:, None, :], g_pre, w_pad, *rope_tabs, conv_w, w_br_b)


def _proj_sample_kernel(x_ref, shift_ref, scale_ref, gpre_ref, w_ref, ra_ref, rp_ref, rm_ref, convw_ref, wbrb_ref,
                        cbuf_ref, qpad_ref, kvnew_ref, gates_ref, sa_ref, ga_ref, pb_ref, u_ref):
    seg, rope = _proj_common(x_ref, shift_ref, scale_ref, gpre_ref, w_ref, ra_ref, rp_ref, rm_ref)
    for n, head in enumerate(_padded_q_heads(seg, rope, HEAD_DIM ** -0.5)):
        qpad_ref[:, n * LANES:(n + 1) * LANES] = head.astype(BF16)
    pieces = _kv_pieces(seg, rope)
    for p in range(6):
        kvnew_ref[:, p * LANES:(p + 1) * LANES] = pieces[p]
    gates_ref[...] = _sigmoid(seg(C_G, C_G + LANES))
    sa_ref[...] = _silu(seg(C_A, C_A + ATTN_WIDTH))
    ga_ref[...] = _sigmoid(seg(C_MA, C_MA + D_MODEL))
    u = seg(C_CC, C_CC + CONV_WIDTH) * seg(C_CX, C_CX + CONV_WIDTH)
    um2 = cbuf_ref[:, 0:CONV_WIDTH]
    um1 = cbuf_ref[:, CONV_WIDTH:2 * CONV_WIDTH]
    pb_ref[...] = _branch_b(seg, um2, um1, u, convw_ref, wbrb_ref)
    u_ref[...] = u


def _proj_sample(x2d, shift, scale, g_pre, w_pad, rope_tabs, conv_w, w_br_b, cbuf):
    n = x2d.shape[0]
    full = lambda shp: pl.BlockSpec(shp, lambda i: (0,) * len(shp))
    out_shapes = (
        jax.ShapeDtypeStruct((n, N_HEADS * LANES), BF16),
        jax.ShapeDtypeStruct((n, 6 * KV_WIDTH), F32),
        jax.ShapeDtypeStruct((n, LANES), F32),
        jax.ShapeDtypeStruct((n, ATTN_WIDTH), F32),
        jax.ShapeDtypeStruct((n, D_MODEL), F32),
        jax.ShapeDtypeStruct((n, D_MODEL), F32),
        jax.ShapeDtypeStruct((n, CONV_WIDTH), F32),
    )
    return pl.pallas_call(
        _proj_sample_kernel,
        grid=(1,),
        in_specs=[full((n, D_MODEL)), full((n, D_MODEL)), full((n, D_MODEL)), full((1, D_MODEL)),
                  full((D_MODEL, IN_PAD)), full((1, LANES)), full((1, LANES)), full((1, LANES)),
                  full((CONV_K, CONV_WIDTH)), full((CONV_WIDTH, D_MODEL)), full((n, 2 * CONV_WIDTH))],
        out_specs=tuple(full(s.shape) for s in out_shapes),
        out_shape=out_shapes,
        compiler_params=pltpu.CompilerParams(dimension_semantics=("arbitrary",), vmem_limit_bytes=VMEM_LIMIT),
        name="proj_sample",
    )(x2d, shift, scale, g_pre, w_pad, *rope_tabs, conv_w, w_br_b, cbuf)


CHUNK_LANES = CMP_STRIDE * KV_WIDTH


def _compress_weights(pe_cmp, w_cmp1, w_cmp2):
    zeros = jnp.zeros((2, CMP_STRIDE, HEAD_DIM, CMP_HIDDEN), w_cmp1.dtype)

    def both_groups(w_half):
        w = w_half.reshape(2, CMP_STRIDE, HEAD_DIM, CMP_HIDDEN)
        g0 = jnp.concatenate([w, zeros], axis=2)
        g1 = jnp.concatenate([zeros, w], axis=2)
        return jnp.concatenate([g0, g1], axis=3).reshape(2, CHUNK_LANES, KV_HEADS * CMP_HIDDEN).astype(BF16)

    half = CMP_STRIDE * HEAD_DIM
    z2 = jnp.zeros_like(w_cmp2)
    w2 = jnp.concatenate([jnp.concatenate([w_cmp2, z2], axis=2), jnp.concatenate([z2, w_cmp2], axis=2)], axis=1)
    pe = jnp.concatenate([pe_cmp, pe_cmp], axis=2)
    return dict(w1_lo=both_groups(w_cmp1[:, :half]), w1_hi=both_groups(w_cmp1[:, half:]), w2=w2.astype(BF16),
                pe_lo=pe[:, :CMP_STRIDE].reshape(2, 1, CHUNK_LANES), pe_hi=pe[:, CMP_STRIDE:].reshape(2, 1, CHUNK_LANES))


def _compress_chunks(c, kind, pelo_ref, pehi_ref, w1lo_ref, w1hi_ref):
    lo = _dot((c + pelo_ref[kind]).astype(BF16), w1lo_ref[kind])
    hi = _dot((c + pehi_ref[kind]).astype(BF16), w1hi_ref[kind])
    return lo, hi


def _compress_prompt_kernel(kc_ref, vc_ref, pelo_ref, pehi_ref, w1lo_ref, w1hi_ref, w2_ref, kc_out_ref, vct_out_ref):
    n_chunk = kc_ref.shape[0] // CMP_STRIDE
    outs = []
    for kind, src_ref in enumerate((kc_ref, vc_ref)):
        c = jnp.concatenate([src_ref[pl.ds(r, n_chunk, stride=CMP_STRIDE), :] for r in range(CMP_STRIDE)], axis=1)
        lo, hi = _compress_chunks(c, kind, pelo_ref, pehi_ref, w1lo_ref, w1hi_ref)
        hid = lo + pltpu.roll(hi, n_chunk - 1, 0)
        outs.append(_dot(_silu(hid).astype(BF16), w2_ref[kind]))
    kc_out_ref[0] = outs[0].astype(BF16)
    vct_out_ref[0] = outs[1].T.astype(BF16)


def _compress_prompt(kvc, cw, batch, seq):
    n_chunk = seq // CMP_STRIDE
    const = lambda a: pl.BlockSpec(a.shape, lambda b: (0,) * a.ndim)
    return pl.pallas_call(
        _compress_prompt_kernel,
        grid=(batch,),
        in_specs=[
            pl.BlockSpec((seq, KV_WIDTH), lambda b: (b, 0)),
            pl.BlockSpec((seq, KV_WIDTH), lambda b: (b, 1)),
            const(cw["pe_lo"]), const(cw["pe_hi"]), const(cw["w1_lo"]), const(cw["w1_hi"]), const(cw["w2"]),
        ],
        out_specs=(pl.BlockSpec((1, n_chunk, KV_WIDTH), lambda b: (b, 0, 0)),
                   pl.BlockSpec((1, KV_WIDTH, n_chunk), lambda b: (b, 0, 0))),
        out_shape=(jax.ShapeDtypeStruct((batch, n_chunk, KV_WIDTH), BF16),
                   jax.ShapeDtypeStruct((batch, KV_WIDTH, n_chunk), BF16)),
        compiler_params=pltpu.CompilerParams(dimension_semantics=("arbitrary",), vmem_limit_bytes=VMEM_LIMIT),
        name="compress_prompt",
    )(kvc, kvc, cw["pe_lo"], cw["pe_hi"], cw["w1_lo"], cw["w1_hi"], cw["w2"])


def _top_k_bias(score):
    lane = lax.broadcasted_iota(jnp.int32, score.shape, 1).astype(F32)
    bias = jnp.full(score.shape, NEG_BIG, F32)
    for _ in range(N_SEL):
        m = jnp.max(score, axis=1, keepdims=True)
        idx = jnp.min(jnp.where(score == m, lane, float(LANES)), axis=1, keepdims=True)
        hit = lane == idx
        bias = jnp.where(hit, 0.0, bias)
        score = jnp.where(hit, -jnp.inf, score)
    return bias


def _split_bf16(x):
    hi = x.astype(BF16)
    lo = (x - hi.astype(F32)).astype(BF16)
    return hi, lo


def _masked_softmax_parts(s, valid):
    s = jnp.where(valid, s, -jnp.inf)
    m = jnp.max(s, axis=-1, keepdims=True)
    m = jnp.where(m == -jnp.inf, 0.0, m)
    e = jnp.exp(s - m)
    den = jnp.maximum(jnp.sum(e, axis=-1, keepdims=True), 1e-30)
    return e, den


def _attn_prompt_kernel(qpad_ref, kaug_ref, vsel_ref, kwin_ref, vwin_ref, kvcmp_ref, gates_ref, ovl_ref,
                        o_ref, qaug_ref, m_ref, l_ref, acc_ref, oc_ref, ow_ref):
    qb = pl.program_id(1)
    q0 = qb * Q_BLOCK
    seq = kaug_ref.shape[0]
    n_cmp = kvcmp_ref.shape[1]
    n_rows = N_HEADS * Q_BLOCK
    n_chunks = n_rows // ROW_CHUNK
    heads_per_chunk = ROW_CHUNK // Q_BLOCK

    for n in range(N_HEADS):
        qaug_ref[n * Q_BLOCK:(n + 1) * Q_BLOCK, 0:LANES] = qpad_ref[:, n * LANES:(n + 1) * LANES]

    def qpos_rows(rows):
        r = lax.broadcasted_iota(jnp.int32, (rows, 1), 0)
        return q0 + (r & (Q_BLOCK - 1))

    kc = kvcmp_ref[0, :, 0:LANES]
    vc = kvcmp_ref[0, :, LANES:2 * LANES]
    c_end = lax.broadcasted_iota(jnp.int32, (1, n_cmp), 1) * CMP_STRIDE + (CMP_BLOCK - 1)
    psum = [None] * KV_HEADS
    for rc in range(n_chunks):
        rows = slice(rc * ROW_CHUNK, (rc + 1) * ROW_CHUNK)
        s = _dot_nt(qaug_ref[rows, 0:LANES], kc)
        e, den = _masked_softmax_parts(s, c_end <= qpos_rows(ROW_CHUNK))
        p = e * (1.0 / den)
        oc_ref[rows, :] = _dot(p.astype(BF16), vc)
        g = (rc * heads_per_chunk) // HEADS_PER_GROUP
        part = p[0:Q_BLOCK]
        for h in range(1, heads_per_chunk):
            part = part + p[h * Q_BLOCK:(h + 1) * Q_BLOCK]
        psum[g] = part if psum[g] is None else psum[g] + part

    qpos = q0 + lax.broadcasted_iota(jnp.int32, (Q_BLOCK, 1), 0)
    cur = qpos // SEL_BLOCK
    j = lax.broadcasted_iota(jnp.int32, (Q_BLOCK, LANES), 1)
    forced = (j == 0) | (j == cur) | (j == cur - 1)
    for g in range(KV_HEADS):
        hi, lo = _split_bf16(psum[g])
        imp = _dot(hi, ovl_ref[...]) + _dot(lo, ovl_ref[...])
        score = jnp.where(forced, FORCED_SCORE, jnp.where(j <= cur, imp, -1.0))
        bias = _top_k_bias(score).astype(BF16)
        for h in range(HEADS_PER_GROUP):
            n = g * HEADS_PER_GROUP + h
            qaug_ref[n * Q_BLOCK:(n + 1) * Q_BLOCK, LANES:2 * LANES] = bias

    m_ref[...] = jnp.full(m_ref.shape, NEG_BIG, F32)
    l_ref[...] = jnp.zeros(l_ref.shape, F32)
    acc_ref[...] = jnp.zeros(acc_ref.shape, F32)
    kt_last = (q0 + Q_BLOCK - 1) // KV_TILE

    def sel_tile(kt, causal):
        k0 = pl.multiple_of(kt * KV_TILE, KV_TILE)
        k = kaug_ref[pl.ds(k0, KV_TILE), :]
        v = vsel_ref[pl.ds(k0, KV_TILE), :]
        for rc in range(n_chunks):
            rows = slice(rc * ROW_CHUNK, (rc + 1) * ROW_CHUNK)
            s = _dot_nt(qaug_ref[rows, :], k)
            if causal:
                kpos = k0 + lax.broadcasted_iota(jnp.int32, (1, KV_TILE), 1)
                s = jnp.where(kpos <= qpos_rows(ROW_CHUNK), s, NEG_BIG)
            m_old = m_ref[rows, :]
            m_new = jnp.maximum(m_old, jnp.max(s, axis=-1, keepdims=True))
            alpha = jnp.exp(m_old - m_new)
            p = jnp.exp(s - m_new)
            l_ref[rows, :] = alpha * l_ref[rows, :] + jnp.sum(p, axis=-1, keepdims=True)
            acc_ref[rows, :] = alpha * acc_ref[rows, :] + _dot(p.astype(BF16), v)
            m_ref[rows, :] = m_new

    def body(kt, carry):
        sel_tile(kt, False)
        return carry

    lax.fori_loop(0, kt_last, body, 0)
    sel_tile(kt_last, True)

    wk = kwin_ref.shape[0] if kwin_ref.shape[0] < WINDOW + Q_BLOCK else WINDOW + Q_BLOCK
    start = pl.multiple_of(jnp.maximum(q0 - WINDOW, 0), Q_BLOCK)
    kw = kwin_ref[pl.ds(start, wk), :]
    vw = vwin_ref[pl.ds(start, wk), :]
    kwpos = start + lax.broadcasted_iota(jnp.int32, (1, wk), 1)
    for rc in range(n_chunks):
        rows = slice(rc * ROW_CHUNK, (rc + 1) * ROW_CHUNK)
        s = _dot_nt(qaug_ref[rows, 0:LANES], kw)
        dt = qpos_rows(ROW_CHUNK) - kwpos
        e, den = _masked_softmax_parts(s, (dt >= 0) & (dt < WINDOW))
        ow_ref[rows, :] = _dot(e.astype(BF16), vw) * (1.0 / den)

    lane = lax.broadcasted_iota(jnp.int32, (Q_BLOCK, LANES), 1)
    lower = lane < HEAD_DIM
    gates = gates_ref[...]
    for pair in range(N_HEADS // 2):
        halves = []
        for n in (2 * pair, 2 * pair + 1):
            rows = slice(n * Q_BLOCK, (n + 1) * Q_BLOCK)
            o_sel = acc_ref[rows, :] * (1.0 / l_ref[rows, :])
            o = (gates[:, 3 * n:3 * n + 1] * oc_ref[rows, :] + gates[:, 3 * n + 1:3 * n + 2] * o_sel
                 + gates[:, 3 * n + 2:3 * n + 3] * ow_ref[rows, :])
            halves.append(o)
        g = (2 * pair) // HEADS_PER_GROUP
        if g == 0:
            merged = jnp.where(lower, halves[0], pltpu.roll(halves[1], HEAD_DIM, 1))
        else:
            merged = jnp.where(lower, pltpu.roll(halves[0], HEAD_DIM, 1), halves[1])
        o_ref[:, pair * LANES:(pair + 1) * LANES] = merged


def _attn_prompt(qpad, ksel, kwinb, kvcmp, gates, ovl, batch, seq):
    nq = seq // Q_BLOCK
    n_cmp = kvcmp.shape[1]
    n_rows = N_HEADS * Q_BLOCK
    rowq = lambda w: pl.BlockSpec((Q_BLOCK, w), lambda b, i: (b * nq + i, 0))
    return pl.pallas_call(
        _attn_prompt_kernel,
        grid=(batch, nq),
        in_specs=[
            rowq(N_HEADS * LANES),
            pl.BlockSpec((seq, 2 * LANES), lambda b, i: (b, 0)),
            pl.BlockSpec((seq, LANES), lambda b, i: (b, 2)),
            pl.BlockSpec((seq, LANES), lambda b, i: (b, 0)),
            pl.BlockSpec((seq, LANES), lambda b, i: (b, 1)),
            pl.BlockSpec((1, n_cmp, 2 * LANES), lambda b, i: (b, 0, 0)),
            rowq(LANES),
            pl.BlockSpec((n_cmp, LANES), lambda b, i: (0, 0)),
        ],
        out_specs=rowq(ATTN_WIDTH),
        out_shape=jax.ShapeDtypeStruct((batch * seq, ATTN_WIDTH), F32),
        scratch_shapes=[
            pltpu.VMEM((n_rows, 2 * LANES), BF16),
            pltpu.VMEM((n_rows, 1), F32),
            pltpu.VMEM((n_rows, 1), F32),
            pltpu.VMEM((n_rows, LANES), F32),
            pltpu.VMEM((n_rows, LANES), F32),
            pltpu.VMEM((n_rows, LANES), F32),
        ],
        compiler_params=pltpu.CompilerParams(dimension_semantics=("arbitrary", "arbitrary"),
                                             vmem_limit_bytes=VMEM_LIMIT),
        name="attn_prompt",
    )(qpad, ksel, ksel, kwinb, kwinb, kvcmp, gates, ovl)


COL_CHUNK = 256


def _skewed(n, stages):
    for step in range(n + len(stages) - 1):
        for si, stage in enumerate(stages):
            if 0 <= step - si < n:
                stage(step - si)


def _skewed_thunks(n, stages):
    return [functools.partial(stage, step - si) for step in range(n + len(stages) - 1)
            for si, stage in enumerate(stages) if 0 <= step - si < n]


def _top_k_bias_t(score, fillers=()):
    blk = lax.broadcasted_iota(jnp.int32, score.shape, 0).astype(F32)
    bias = jnp.full(score.shape, NEG_BIG, F32)
    fillers = list(fillers)
    for _ in range(N_SEL):
        m = jnp.max(score, axis=0, keepdims=True)
        idx = jnp.min(jnp.where(score == m, blk, float(LANES)), axis=0, keepdims=True)
        hit = blk == idx
        bias = jnp.where(hit, 0.0, bias)
        score = jnp.where(hit, -jnp.inf, score)
        for thunk in fillers[:2]:
            thunk()
        fillers = fillers[2:]
    for thunk in fillers:
        thunk()
    return bias


def _masked_softmax_parts_t(s, valid):
    s = jnp.where(valid, s, -jnp.inf)
    m = jnp.max(s, axis=0, keepdims=True)
    m = jnp.where(m == -jnp.inf, 0.0, m)
    e = jnp.exp2(s - m)
    den = jnp.maximum(jnp.sum(e, axis=0, keepdims=True), 1e-30)
    return e, den


def _attn_prompt_t_kernel(qt_ref, kaug_ref, vselt_ref, kwin_ref, vwint_ref, kc_ref, vct_ref, gatest_ref, ovlt_ref,
                          o_ref, qaugt_ref, m_ref, l_ref, acct_ref, oct_ref, owt_ref, s_ref):
    qb = pl.program_id(1)
    q0 = qb * Q_BLOCK
    n_cmp = kc_ref.shape[1]
    n_cols = N_HEADS * Q_BLOCK
    n_chunks = n_cols // COL_CHUNK
    heads_per_chunk = COL_CHUNK // Q_BLOCK

    for n in range(N_HEADS):
        qaugt_ref[0:LANES, n * Q_BLOCK:(n + 1) * Q_BLOCK] = qt_ref[n]

    def qpos_cols(cols):
        c = lax.broadcasted_iota(jnp.int32, (1, cols), 1)
        return q0 + (c & (Q_BLOCK - 1))

    def chunk_cols(cc):
        return slice(cc * COL_CHUNK, (cc + 1) * COL_CHUNK)

    kc = kc_ref[0]
    vct = vct_ref[0]
    c_end = lax.broadcasted_iota(jnp.int32, (n_cmp, 1), 0) * CMP_STRIDE + (CMP_BLOCK - 1)
    psum = [None] * KV_HEADS
    cs, cp = {}, {}

    def cmp_scores(cc):
        cs[cc] = _dot(kc, qaugt_ref[0:LANES, chunk_cols(cc)])

    def cmp_softmax(cc):
        e, den = _masked_softmax_parts_t(cs.pop(cc), c_end <= qpos_cols(COL_CHUNK))
        p = e * (1.0 / den)
        cp[cc] = p.astype(BF16)
        g = (cc * heads_per_chunk) // HEADS_PER_GROUP
        part = p[:, 0:Q_BLOCK]
        for h in range(1, heads_per_chunk):
            part = part + p[:, h * Q_BLOCK:(h + 1) * Q_BLOCK]
        psum[g] = part if psum[g] is None else psum[g] + part

    def cmp_values(cc):
        oct_ref[:, chunk_cols(cc)] = _dot(vct, cp.pop(cc))

    _skewed(n_chunks, (cmp_scores, cmp_softmax, cmp_values))

    cur = qpos_cols(KV_HEADS * Q_BLOCK) // SEL_BLOCK
    j = lax.broadcasted_iota(jnp.int32, (LANES, KV_HEADS * Q_BLOCK), 0)
    forced = (j == 0) | (j == cur) | (j == cur - 1)
    hi, lo = _split_bf16(jnp.concatenate(psum, axis=1))
    imp = _dot(ovlt_ref[...], hi) + _dot(ovlt_ref[...], lo)
    score = jnp.where(forced, FORCED_SCORE, jnp.where(j <= cur, imp, -1.0))

    wk = kwin_ref.shape[0] if kwin_ref.shape[0] < WINDOW + Q_BLOCK else WINDOW + Q_BLOCK
    start = pl.multiple_of(jnp.maximum(q0 - WINDOW, 0), Q_BLOCK)
    kw = kwin_ref[pl.ds(start, wk), :]
    vwt = vwint_ref[:, pl.ds(start, wk)]
    kwpos = start + lax.broadcasted_iota(jnp.int32, (wk, 1), 0)
    ws, we, wden = {}, {}, {}

    def win_scores(cc):
        ws[cc] = _dot(kw, qaugt_ref[0:LANES, chunk_cols(cc)])

    def win_softmax(cc):
        dt = qpos_cols(COL_CHUNK) - kwpos
        e, wden[cc] = _masked_softmax_parts_t(ws.pop(cc), (dt >= 0) & (dt < WINDOW))
        we[cc] = e.astype(BF16)

    def win_values(cc):
        owt_ref[:, chunk_cols(cc)] = _dot(vwt, we.pop(cc)) * (1.0 / wden.pop(cc))

    bias = _top_k_bias_t(score, _skewed_thunks(n_chunks, (win_scores, win_softmax, win_values))).astype(BF16)
    for n in range(N_HEADS):
        g = n // HEADS_PER_GROUP
        qaugt_ref[LANES:2 * LANES, n * Q_BLOCK:(n + 1) * Q_BLOCK] = bias[:, g * Q_BLOCK:(g + 1) * Q_BLOCK]

    m_ref[...] = jnp.full(m_ref.shape, NEG_BIG, F32)
    l_ref[...] = jnp.zeros(l_ref.shape, F32)
    acct_ref[...] = jnp.zeros(acct_ref.shape, F32)
    kt_last = (q0 + Q_BLOCK - 1) // KV_TILE

    def key_tile(kt):
        return pl.multiple_of(kt * KV_TILE, KV_TILE)

    def scores(kt, cc):
        s_ref[:, chunk_cols(cc)] = _dot(kaug_ref[pl.ds(key_tile(kt), KV_TILE), :], qaugt_ref[:, chunk_cols(cc)])

    def sel_tile(kt, causal, issue_next):
        k0 = key_tile(kt)
        vt = vselt_ref[:, pl.ds(k0, KV_TILE)]
        pb, alpha = {}, {}

        def softmax(cc):
            cols = chunk_cols(cc)
            s = s_ref[:, cols]
            if causal:
                kpos = k0 + lax.broadcasted_iota(jnp.int32, (KV_TILE, 1), 0)
                s = jnp.where(kpos <= qpos_cols(COL_CHUNK), s, NEG_BIG)
            if issue_next:
                scores(kt + 1, cc)
            m_old = m_ref[:, cols]
            m_new = jnp.maximum(m_old, jnp.max(s, axis=0, keepdims=True))
            alpha[cc] = jnp.exp2(m_old - m_new)
            p = jnp.exp2(s - m_new)
            l_ref[:, cols] = alpha[cc] * l_ref[:, cols] + jnp.sum(p, axis=0, keepdims=True)
            m_ref[:, cols] = m_new
            pb[cc] = p.astype(BF16)

        def values(cc):
            cols = chunk_cols(cc)
            acct_ref[:, cols] = alpha.pop(cc) * acct_ref[:, cols] + _dot(vt, pb.pop(cc))

        _skewed(n_chunks, (softmax, values))

    for cc in range(n_chunks):
        scores(0, cc)

    def body(kt, carry):
        sel_tile(kt, False, True)
        return carry

    lax.fori_loop(0, kt_last, body, 0)
    sel_tile(kt_last, True, False)

    gates = gatest_ref[...]
    for pair in range(N_HEADS // 2):
        g = (2 * pair) // HEADS_PER_GROUP
        feat = slice(g * HEAD_DIM, (g + 1) * HEAD_DIM)
        halves = []
        for n in (2 * pair, 2 * pair + 1):
            cols = slice(n * Q_BLOCK, (n + 1) * Q_BLOCK)
            o_sel = acct_ref[feat, cols] * (1.0 / l_ref[:, cols])
            halves.append(gates[3 * n:3 * n + 1, :] * oct_ref[feat, cols] + gates[3 * n + 1:3 * n + 2, :] * o_sel
                          + gates[3 * n + 2:3 * n + 3, :] * owt_ref[feat, cols])
        o_ref[:, pair * LANES:(pair + 1) * LANES] = jnp.concatenate(halves, axis=0).T


def _attn_prompt_t(qt, ksel, kwinb, vselt, vwint, kc, vct, gatest, ovlt, batch, seq):
    nq = seq // Q_BLOCK
    n_cmp = kc.shape[1]
    n_cols = N_HEADS * Q_BLOCK
    per_b = lambda a: pl.BlockSpec((None,) + a.shape[1:], lambda b, i: (b,) + (0,) * (a.ndim - 1))
    return pl.pallas_call(
        _attn_prompt_t_kernel,
        grid=(batch, nq),
        in_specs=[
            pl.BlockSpec((None, N_HEADS, LANES, Q_BLOCK), lambda b, i: (b, 0, 0, i)),
            pl.BlockSpec((seq, 2 * LANES), lambda b, i: (b, 0)),
            per_b(vselt),
            pl.BlockSpec((seq, LANES), lambda b, i: (b, 0)),
            per_b(vwint),
            pl.BlockSpec((1, n_cmp, LANES), lambda b, i: (b, 0, 0)),
            pl.BlockSpec((1, LANES, n_cmp), lambda b, i: (b, 0, 0)),
            pl.BlockSpec((None, GATE_ROWS, Q_BLOCK), lambda b, i: (b, 0, i)),
            pl.BlockSpec((LANES, n_cmp), lambda b, i: (0, 0)),
        ],
        out_specs=pl.BlockSpec((Q_BLOCK, ATTN_WIDTH), lambda b, i: (b * nq + i, 0)),
        out_shape=jax.ShapeDtypeStruct((batch * seq, ATTN_WIDTH), F32),
        scratch_shapes=[
            pltpu.VMEM((2 * LANES, n_cols), BF16),
            pltpu.VMEM((1, n_cols), F32),
            pltpu.VMEM((1, n_cols), F32),
            pltpu.VMEM((LANES, n_cols), F32),
            pltpu.VMEM((LANES, n_cols), F32),
            pltpu.VMEM((LANES, n_cols), F32),
            pltpu.VMEM((KV_TILE, n_cols), F32),
        ],
        compiler_params=pltpu.CompilerParams(dimension_semantics=("arbitrary", "arbitrary"),
                                             vmem_limit_bytes=VMEM_LIMIT),
        name="attn_prompt",
    )(qt, ksel, vselt, kwinb, vwint, kc, vct, gatest, ovlt)


def _finish_kernel(x_ref, o_ref, sa_ref, ga_ref, pb_ref, gate_ref, gpost_ref, wbra_ref, wout_ref, y_ref):
    ya = _dot((o_ref[...] * sa_ref[...]).astype(BF16), wbra_ref[...])
    mix = ga_ref[...] * ya + pb_ref[...]
    o = _dot(mix.astype(BF16), wout_ref[...])
    ms = jnp.mean(o * o, axis=-1, keepdims=True)
    on = o * lax.rsqrt(ms + RMS_EPS) * gpost_ref[...]
    y_ref[...] = x_ref[...] + gate_ref[...] * on


def _finish(x2d, o_attn, sa, ga, pb, gate, g_post, w_br_a, w_out, tm, rows_per_gate):
    n = x2d.shape[0]
    row = lambda w: pl.BlockSpec((tm, w), lambda i: (i, 0))
    const = lambda shp: pl.BlockSpec(shp, lambda i: (0,) * len(shp))
    if rows_per_gate == 1:
        gate_spec = row(D_MODEL)
    else:
        tiles_per_gate = rows_per_gate // tm
        gate = gate[:, None, :]
        gate_spec = pl.BlockSpec((None, 1, D_MODEL), lambda i: (i // tiles_per_gate, 0, 0))
    return pl.pallas_call(
        _finish_kernel,
        grid=(n // tm,),
        in_specs=[row(D_MODEL), row(ATTN_WIDTH), row(ATTN_WIDTH), row(D_MODEL), row(D_MODEL), gate_spec,
                  const((1, D_MODEL)), const((ATTN_WIDTH, D_MODEL)), const((D_MODEL, D_MODEL))],
        out_specs=row(D_MODEL),
        out_shape=jax.ShapeDtypeStruct((n, D_MODEL), F32),
        compiler_params=pltpu.CompilerParams(dimension_semantics=("arbitrary",), vmem_limit_bytes=VMEM_LIMIT),
        name="finish",
    )(x2d, o_attn, sa, ga, pb, gate, g_post, w_br_a, w_out)


def _attn_decode_kernel(pt_ref, *refs, n_req, n_pages, page_size, wbuf):
    del pt_ref
    page_refs = refs[:n_req * n_pages]
    (win_ref, q_ref, kvnew_ref, gates_ref, pelo_ref, pehi_ref, w1lo_ref, w1hi_ref, w2_ref, ovl_ref, expand_ref,
     o_ref, winout_ref, kcmp_ref, vcmp_ref) = refs[n_req * n_pages:]
    past_len = n_pages * page_size
    n_chunk = past_len // CMP_STRIDE
    reqs = range(n_req)
    heads = lambda b: slice(b * N_HEADS, (b + 1) * N_HEADS)
    per_req = lambda fn: jnp.concatenate([fn(b) for b in reqs], axis=0)
    q8f = per_req(lambda b: q_ref[b])
    q8 = q8f.astype(BF16)
    new_row = lambda which: per_req(lambda b: jnp.broadcast_to(kvnew_ref[b, which:which + 1, :], (N_HEADS, LANES)))

    def stream(ref, which):
        slab = ref[which * KV_HEADS:(which + 1) * KV_HEADS]
        return slab.reshape(KV_WIDTH, slab.shape[-1])

    for i, pr in enumerate(page_refs):
        rows = slice(i * page_size, (i + 1) * page_size)
        kcmp_ref[rows, :] = stream(pr, 0).T
        vcmp_ref[rows, :] = stream(pr, 1).T

    kvc = []
    for kind, src_ref in enumerate((kcmp_ref, vcmp_ref)):
        c = jnp.concatenate([src_ref[pl.ds(r, n_req * n_chunk, stride=CMP_STRIDE), :] for r in range(CMP_STRIDE)],
                            axis=1)
        lo, hi = _compress_chunks(c, kind, pelo_ref, pehi_ref, w1lo_ref, w1hi_ref)
        hid = lo + pltpu.roll(hi, n_req * n_chunk - 1, 0)
        kvc.append(_dot(_silu(hid).astype(BF16), w2_ref[kind]).astype(BF16))
    kc, vc = kvc
    chunks = lambda b: slice(b * n_chunk, (b + 1) * n_chunk)

    c_end = lax.broadcasted_iota(jnp.int32, (1, n_chunk), 1) * CMP_STRIDE + (CMP_BLOCK - 1)
    e, den = _masked_softmax_parts(per_req(lambda b: _dot_nt(q8[heads(b)], kc[chunks(b)])), c_end <= past_len)
    p = e * (1.0 / den)
    pb = p.astype(BF16)
    o_cmp = per_req(lambda b: _dot(pb[heads(b)], vc[chunks(b)]))

    row = lax.broadcasted_iota(jnp.int32, p.shape, 0)
    in_g0 = (row & (N_HEADS - 1)) < HEADS_PER_GROUP

    def group_sums(b):
        pr_, g0_ = p[heads(b)], in_g0[heads(b)]
        g0 = jnp.sum(jnp.where(g0_, pr_, 0.0), axis=0, keepdims=True)
        g1 = jnp.sum(jnp.where(g0_, 0.0, pr_), axis=0, keepdims=True)
        return jnp.where(g0_, g0, g1)

    hi, lo = _split_bf16(per_req(group_sums))
    imp = _dot(hi, ovl_ref[...]) + _dot(lo, ovl_ref[...])
    cur = past_len // SEL_BLOCK
    j = lax.broadcasted_iota(jnp.int32, imp.shape, 1)
    forced = (j == 0) | (j == cur) | (j == cur - 1)
    score = jnp.where(j > cur, -jnp.inf, jnp.where(forced, FORCED_SCORE, imp))
    bias_keys = _dot(_top_k_bias_t(score.T).T.astype(BF16), expand_ref[...])

    pages = lambda b: page_refs[b * n_pages:(b + 1) * n_pages]
    s = per_req(lambda b: jnp.concatenate([_dot(q8[heads(b)], stream(pr, 2).astype(BF16)) for pr in pages(b)],
                                          axis=1)) + bias_keys
    s_new = jnp.sum(q8f * new_row(2), axis=1, keepdims=True)
    m = jnp.maximum(jnp.max(s, axis=1, keepdims=True), s_new)
    e = jnp.exp(s - m)
    e_new = jnp.exp(s_new - m)
    den = jnp.sum(e, axis=1, keepdims=True) + e_new
    eb = e.astype(BF16)

    def sel_values(b):
        acc = None
        for i, pr in enumerate(pages(b)):
            part = _dot_nt(eb[heads(b), i * page_size:(i + 1) * page_size], stream(pr, 3).astype(BF16))
            acc = part if acc is None else acc + part
        return acc

    o_sel = (per_req(sel_values) + e_new * new_row(3)) * (1.0 / den)

    kwpos = past_len - wbuf + lax.broadcasted_iota(jnp.int32, (1, wbuf), 1)
    dt = past_len - kwpos
    valid = (dt >= 0) & (dt < WINDOW) & (kwpos >= 0)
    kw_t = [stream(win_ref.at[b], 0) for b in reqs]
    vw_t = [stream(win_ref.at[b], 1) for b in reqs]
    s = jnp.where(valid, per_req(lambda b: _dot(q8[heads(b)], kw_t[b].astype(BF16))), -jnp.inf)
    s_new = jnp.sum(q8f * new_row(4), axis=1, keepdims=True)
    m = jnp.maximum(jnp.max(s, axis=1, keepdims=True), s_new)
    e = jnp.exp(s - m)
    e_new = jnp.exp(s_new - m)
    den = jnp.sum(e, axis=1, keepdims=True) + e_new
    eb = e.astype(BF16)
    o_win = (per_req(lambda b: _dot_nt(eb[heads(b)], vw_t[b].astype(BF16))) + e_new * new_row(5)) * (1.0 / den)

    gates = per_req(lambda b: gates_ref[b])
    o = gates[:, 0:1] * o_cmp + gates[:, 1:2] * o_sel + gates[:, 2:3] * o_win
    o = jnp.where(in_g0, o, pltpu.roll(o, HEAD_DIM, 1))
    lane = lax.broadcasted_iota(jnp.int32, (KV_WIDTH, wbuf), 1)
    for b in reqs:
        o_ref[b] = o[heads(b)]
        new_cols = jnp.concatenate([kvnew_ref[b], jnp.zeros((SUBLANES - 6, LANES), F32)], axis=0).T
        for kind, old in enumerate((kw_t[b], vw_t[b])):
            shifted = jnp.where(lane == wbuf - 1, new_cols[:, 4 + kind:5 + kind], pltpu.roll(old, wbuf - 1, 1))
            winout_ref[b, kind * KV_HEADS:(kind + 1) * KV_HEADS] = shifted.reshape(KV_HEADS, HEAD_DIM, wbuf)


DECODE_REQS_PER_STEP = 2


def _attn_decode(cache_t, page_table, win_t, q8, kvnew, gates8, cw, ovl, expand):
    page_size = cache_t.shape[-1]
    batch, n_pages = page_table.shape
    wbuf = win_t.shape[-1]
    past_len = n_pages * page_size
    n_req = DECODE_REQS_PER_STEP if batch % DECODE_REQS_PER_STEP == 0 else 1

    def page_spec(r, k):
        return pl.BlockSpec((None, PAGE_ROWS, HEAD_DIM, page_size), lambda i, pt: (pt[i * n_req + r, k], 0, 0, 0))

    per_step = lambda a: pl.BlockSpec((n_req,) + a.shape[1:], lambda i, pt: (i,) + (0,) * (a.ndim - 1))
    const = lambda a: pl.BlockSpec(a.shape, lambda i, pt: (0,) * a.ndim, pipeline_mode=pl.Buffered(1))
    consts = [cw["pe_lo"], cw["pe_hi"], cw["w1_lo"], cw["w1_hi"], cw["w2"], ovl, expand]
    grid_spec = pltpu.PrefetchScalarGridSpec(
        num_scalar_prefetch=1,
        grid=(batch // n_req,),
        in_specs=[page_spec(r, k) for r in range(n_req) for k in range(n_pages)]
        + [per_step(win_t), per_step(q8), per_step(kvnew), per_step(gates8)]
        + [const(a) for a in consts],
        out_specs=(pl.BlockSpec((n_req, N_HEADS, LANES), lambda i, pt: (i, 0, 0)), per_step(win_t)),
        scratch_shapes=[pltpu.VMEM((n_req * past_len, LANES), F32), pltpu.VMEM((n_req * past_len, LANES), F32)],
    )
    o8, win_out = pl.pallas_call(
        functools.partial(_attn_decode_kernel, n_req=n_req, n_pages=n_pages, page_size=page_size, wbuf=wbuf),
        grid_spec=grid_spec,
        out_shape=(jax.ShapeDtypeStruct((batch, N_HEADS, LANES), F32),
                   jax.ShapeDtypeStruct(win_t.shape, F32)),
        compiler_params=pltpu.CompilerParams(dimension_semantics=("arbitrary",), vmem_limit_bytes=VMEM_LIMIT),
        name="attn_decode",
    )(page_table, *([cache_t] * (n_req * n_pages)), win_t, q8, kvnew, gates8, *consts)
    return o8, win_out


def _overlap_matrix(n_cmp_pad, n_cmp, n_selb):
    cs = jnp.arange(n_cmp_pad) * CMP_STRIDE
    ss = jnp.arange(LANES) * SEL_BLOCK
    ov = (cs[:, None] < ss[None, :] + SEL_BLOCK) & (cs[:, None] + CMP_BLOCK > ss[None, :])
    ov = ov & (jnp.arange(n_cmp_pad) < n_cmp)[:, None] & (jnp.arange(LANES) < n_selb)[None, :]
    return ov.astype(BF16)


def _prompt_layer(x, mod, wts):
    batch, seq, _ = x.shape
    x2d = x.reshape(batch * seq, D_MODEL)
    shift, scale, gate = mod[:, 0:D_MODEL], mod[:, D_MODEL:2 * D_MODEL], mod[:, 2 * D_MODEL:]
    tabs = _rope_tables(jnp.arange(seq))
    tm = min(256, seq)
    (qt, kvt, kvc, kwin, ksel, kwinb, vselt, vwint, gatest, sa, ga, pb, utail) = _proj_prompt(
        x2d, shift, scale, wts["g_pre"], wts["w_in"], tabs, wts["conv_w"], wts["w_br_b"], batch, seq, tm)
    kc, vct = _compress_prompt(kvc, wts["cmp"], batch, seq)
    n_chunk = seq // CMP_STRIDE
    ovlt = _overlap_matrix(n_chunk, n_chunk - 1, -(-seq // SEL_BLOCK)).T
    o_attn = _attn_prompt_t(qt, ksel, kwinb, vselt, vwint, kc, vct, gatest, ovlt, batch, seq)
    y = _finish(x2d, o_attn, sa, ga, pb, gate, wts["g_post"], wts["w_br_a"], wts["w_out"], tm, seq)
    n_keep = min(WINDOW, seq)
    return (y.reshape(batch, seq, D_MODEL),
            jnp.transpose(kvt.reshape(batch, 4, KV_HEADS, HEAD_DIM, seq), (0, 4, 1, 2, 3)),
            kwin.reshape(batch, seq, 2, KV_HEADS, HEAD_DIM)[:, seq - n_keep:],
            utail[:, SUBLANES - (CONV_K - 1):])


def _sample_layer(x, mod, cache, page_table, win, conv_state, wts):
    batch, dec_seq, _ = x.shape
    assert dec_seq == 1
    n_pages, page_size, wbuf = page_table.shape[1], cache.shape[1], win.shape[1]
    past_len = n_pages * page_size
    assert past_len % SEL_BLOCK == 0 and wbuf == WINDOW and past_len // SEL_BLOCK < LANES
    x2d = x.reshape(batch, D_MODEL)
    shift, scale, gate = mod[:, 0:D_MODEL], mod[:, D_MODEL:2 * D_MODEL], mod[:, 2 * D_MODEL:]
    tabs = _rope_tables(jnp.full((1,), past_len, jnp.int32))
    cbuf = conv_state.reshape(batch, (CONV_K - 1) * CONV_WIDTH)
    qpad, kvnew, gates, sa, ga, pb, u = _proj_sample(
        x2d, shift, scale, wts["g_pre"], wts["w_in"], tabs, wts["conv_w"], wts["w_br_b"], cbuf)
    n_gate = N_HEADS * 3
    gates8 = jnp.pad(gates[:, :n_gate].reshape(batch, N_HEADS, 3), ((0, 0), (0, 0), (0, LANES - 3)))
    n_chunk = past_len // CMP_STRIDE
    ovl = _overlap_matrix(n_chunk, n_chunk - 1, past_len // SEL_BLOCK + 1)
    expand = (jnp.arange(LANES)[:, None] == (jnp.arange(past_len) // SEL_BLOCK)[None, :]).astype(BF16)
    cache_t = jnp.transpose(cache, (0, 2, 3, 4, 1)).reshape(cache.shape[0], PAGE_ROWS, HEAD_DIM, page_size)
    win_t = jnp.transpose(win, (0, 2, 3, 4, 1)).reshape(batch, WIN_ROWS, HEAD_DIM, wbuf)
    o8, win_out = _attn_decode(cache_t, page_table, win_t, qpad.reshape(batch, N_HEADS, LANES).astype(F32),
                               kvnew.reshape(batch, 6, LANES), gates8, wts["cmp"], ovl, expand)
    win_out = jnp.transpose(win_out.reshape(batch, 2, KV_HEADS, HEAD_DIM, wbuf), (0, 4, 1, 2, 3))
    y = _finish(x2d, o8[:, :, :HEAD_DIM].reshape(batch, ATTN_WIDTH), sa, ga, pb, gate, wts["g_post"], wts["w_br_a"],
                wts["w_out"], batch, 1)
    return (y.reshape(batch, 1, D_MODEL),
            kvnew[:, :4 * KV_WIDTH].reshape(batch, 1, 4, KV_HEADS, HEAD_DIM),
            win_out,
            jnp.stack([conv_state[:, CONV_K - 2], u], axis=1))


def _prep_weights(w_ada, b_ada, g_pre, g_post, w_in, pe_cmp, w_cmp1, w_cmp2, conv_w, w_br_a, w_br_b, w_out):
    n_unpadded_gate = HEADS_PER_GROUP * KV_HEADS * 3
    w_pad = jnp.concatenate(
        [w_in[:, :C_G + n_unpadded_gate], jnp.zeros((D_MODEL, LANES - n_unpadded_gate), w_in.dtype),
         w_in[:, C_G + n_unpadded_gate:]], axis=1).astype(BF16)
    half = CMP_STRIDE * HEAD_DIM
    return dict(
        w_ada=w_ada, b_ada=b_ada, g_pre=g_pre.reshape(1, -1), g_post=g_post.reshape(1, -1), w_in=w_pad,
        cmp=_compress_weights(pe_cmp, w_cmp1, w_cmp2), conv_w=conv_w,
        w_br_a=w_br_a.astype(BF16), w_br_b=w_br_b.astype(BF16), w_out=w_out.astype(BF16))


def kernel(x_prompt, x_sample, cache_kv_pages, state_win_kv, state_conv, page_table, c_prompt, c_sample, w_ada, b_ada, g_pre, g_post, w_in, pe_cmp, w_cmp1, w_cmp2, conv_w, w_br_a, w_br_b, w_out):
    depth = w_in.shape[0]
    assert depth == 1
    wts = _prep_weights(w_ada[0], b_ada[0], g_pre[0], g_post[0], w_in[0], pe_cmp[0], w_cmp1[0], w_cmp2[0],
                        conv_w[0], w_br_a[0], w_br_b[0], w_out[0])
    n_prompt = c_prompt.shape[0]
    mod = _ada(jnp.concatenate([c_prompt, c_sample], axis=0), wts["w_ada"], wts["b_ada"])
    yp, kvp, wp, cp = _prompt_layer(x_prompt, mod[:n_prompt], wts)
    ys, kvs, ws, cs = _sample_layer(x_sample, mod[n_prompt:], cache_kv_pages[0], page_table, state_win_kv[0],
                                    state_conv[0], wts)
    return (yp, ys, kvp[None], wp[None], cp[None], kvs[None], ws[None], cs[None])
```

```python
import functools

import jax
import jax.numpy as jnp
from jax import lax
from jax.experimental import pallas as pl
from jax.experimental.pallas import tpu as pltpu

F32 = jnp.float32
BF16 = jnp.bfloat16

D_MODEL = 1024
N_HEADS = 8
KV_HEADS = 2
HEADS_PER_GROUP = N_HEADS // KV_HEADS
HEAD_DIM = 64
ROPE_DIM = HEAD_DIM // 4
ROPE_THETA = 500000.0
CMP_BLOCK = 32
CMP_STRIDE = 16
CMP_HIDDEN = 256
SEL_BLOCK = 64
N_SEL = 16
WINDOW = 512
Q_BLOCK = 256
FORCED_SCORE = 1e6
CONV_WIDTH = D_MODEL // 2
CONV_K = 3
ATTN_WIDTH = N_HEADS * HEAD_DIM
KV_WIDTH = KV_HEADS * HEAD_DIM
RMS_EPS = 1e-6

LANES = 128
SUBLANES = 8
VMEM_LIMIT = 56 * 1024 * 1024

C_Q = 0
C_KV = C_Q + ATTN_WIDTH
C_G = C_KV + 6 * KV_WIDTH
C_A = C_G + LANES
C_CB = C_A + ATTN_WIDTH
C_CC = C_CB + CONV_WIDTH
C_CX = C_CC + CONV_WIDTH
C_CG = C_CX + CONV_WIDTH
C_MA = C_CG + CONV_WIDTH
C_MB = C_MA + D_MODEL
IN_PAD = C_MB + D_MODEL

PAGE_ROWS = 4 * KV_HEADS
WIN_ROWS = 2 * KV_HEADS

GATE_ROWS = 32
LOG2_E = 1.4426950408889634

NEG_BIG = -1e30
KV_TILE = 512


def _sigmoid(x):
    return 1.0 / (1.0 + jnp.exp(-x))


def _silu(x):
    return x * _sigmoid(x)


def _dot(a, b):
    return jnp.dot(a, b, preferred_element_type=F32)


def _dot_nt(a, b):
    return lax.dot_general(a, b, (((1,), (1,)), ((), ())), preferred_element_type=F32)


def _rope_tables(pos):
    half = ROPE_DIM // 2
    inv = ROPE_THETA ** (-jnp.arange(half, dtype=F32) / half)
    ang = pos.astype(F32)[:, None] * inv[None, :]
    cos, sin = jnp.cos(ang), jnp.sin(ang)
    n = pos.shape[0]
    a = jnp.concatenate([cos, cos, jnp.ones((n, HEAD_DIM - ROPE_DIM), F32)], axis=1)
    p = jnp.concatenate([jnp.zeros((n, half), F32), sin, jnp.zeros((n, HEAD_DIM - ROPE_DIM), F32)], axis=1)
    m = jnp.concatenate([-sin, jnp.zeros((n, HEAD_DIM - half), F32)], axis=1)
    tile = lambda t: jnp.concatenate([t, t], axis=1)
    return tile(a), tile(p), tile(m)


def _rope(x, ra, rp, rm):
    half = ROPE_DIM // 2
    return x * ra + pltpu.roll(x, half, 1) * rp + pltpu.roll(x, LANES - half, 1) * rm


def _ada_kernel(c_ref, w_ref, b_ref, o_ref):
    c = _silu(c_ref[...]).astype(BF16)
    o_ref[...] = _dot(c, w_ref[...].astype(BF16)) + b_ref[...]


def _ada(c_all, w_ada, b_ada):
    n = c_all.shape[0]
    tn = 512
    return pl.pallas_call(
        _ada_kernel,
        grid=(3 * D_MODEL // tn,),
        in_specs=[
            pl.BlockSpec((n, D_MODEL), lambda j: (0, 0)),
            pl.BlockSpec((D_MODEL, tn), lambda j: (0, j)),
            pl.BlockSpec((1, tn), lambda j: (0, j)),
        ],
        out_specs=pl.BlockSpec((n, tn), lambda j: (0, j)),
        out_shape=jax.ShapeDtypeStruct((n, 3 * D_MODEL), F32),
        compiler_params=pltpu.CompilerParams(dimension_semantics=("arbitrary",), vmem_limit_bytes=VMEM_LIMIT),
        name="ada",
    )(c_all, w_ada, b_ada.reshape(1, -1))


def _proj_common(x_ref, shift_ref, scale_ref, gpre_ref, w_ref, ra_ref, rp_ref, rm_ref):
    x = x_ref[...]
    ms = jnp.mean(x * x, axis=-1, keepdims=True)
    xn = x * lax.rsqrt(ms + RMS_EPS) * gpre_ref[...]
    h = xn * (1.0 + scale_ref[...]) + shift_ref[...]
    hb = h.astype(BF16)
    ra, rp, rm = ra_ref[...], rp_ref[...], rm_ref[...]

    def seg(lo, hi):
        return _dot(hb, w_ref[:, lo:hi])

    return seg, (ra, rp, rm)


def _padded_q_heads(seg, rope, scale):
    zq = seg(C_Q, C_Q + ATTN_WIDTH)
    lane = lax.broadcasted_iota(jnp.int32, (zq.shape[0], LANES), 1)
    lower = lane < HEAD_DIM
    heads = []
    for j in range(ATTN_WIDTH // LANES):
        c = _rope(zq[:, j * LANES:(j + 1) * LANES], *rope) * scale
        r = pltpu.roll(c, HEAD_DIM, 1)
        if (2 * j) // HEADS_PER_GROUP == 0:
            heads += [jnp.where(lower, c, 0.0), jnp.where(lower, r, 0.0)]
        else:
            heads += [jnp.where(lower, 0.0, r), jnp.where(lower, 0.0, c)]
    return heads


def _kv_pieces(seg, rope):
    zkv = seg(C_KV, C_KV + 6 * KV_WIDTH)
    pieces = []
    for p in range(6):
        c = zkv[:, p * LANES:(p + 1) * LANES]
        pieces.append(_rope(c, *rope) if p % 2 == 0 else c)
    return pieces


def _branch_b(seg, um2, um1, u, convw_ref, wbrb_ref):
    cb = seg(C_CB, C_CB + CONV_WIDTH)
    conv = convw_ref[0:1, :] * um2
    conv = conv + convw_ref[1:2, :] * um1
    conv = conv + convw_ref[2:3, :] * u
    ybin = cb * conv * _silu(seg(C_CG, C_CG + CONV_WIDTH))
    yb = _dot(ybin.astype(BF16), wbrb_ref[...])
    gb = _sigmoid(seg(C_MB, C_MB + D_MODEL))
    return gb * yb


def _proj_prompt_kernel(x_ref, shift_ref, scale_ref, gpre_ref, w_ref, ra_ref, rp_ref, rm_ref, convw_ref, wbrb_ref,
                        qt_ref, kvt_ref, kvcmp_ref, kwin_ref, ksel_ref, kwinb_ref, vselt_ref, vwint_ref, gatest_ref,
                        sa_ref, ga_ref, pb_ref, utail_ref, carry_ref):
    ti = pl.program_id(1)
    tm = x_ref.shape[0]
    seg, rope = _proj_common(x_ref, shift_ref, scale_ref, gpre_ref, w_ref, ra_ref, rp_ref, rm_ref)
    for n, head in enumerate(_padded_q_heads(seg, rope, HEAD_DIM ** -0.5 * LOG2_E)):
        qt_ref[n] = head.T.astype(BF16)

    pieces = _kv_pieces(seg, rope)
    pieces_t = [p.T for p in pieces]
    for p in range(4):
        kvt_ref[p * KV_WIDTH:(p + 1) * KV_WIDTH, :] = pieces_t[p]
    kvcmp_ref[:, 0:LANES] = pieces[0]
    kvcmp_ref[:, LANES:2 * LANES] = pieces[1]
    kwin_ref[:, 0:LANES] = pieces[4]
    kwin_ref[:, LANES:2 * LANES] = pieces[5]
    kwinb_ref[...] = pieces[4].astype(BF16)
    vselt_ref[...] = pieces_t[3].astype(BF16)
    vwint_ref[...] = pieces_t[5].astype(BF16)
    row = ti * tm + lax.broadcasted_iota(jnp.int32, (tm, LANES), 0)
    lane = lax.broadcasted_iota(jnp.int32, (tm, LANES), 1)
    onehot = jnp.where(lane == row // SEL_BLOCK, 1.0, 0.0)
    ksel_ref[:, 0:LANES] = pieces[2].astype(BF16)
    ksel_ref[:, LANES:2 * LANES] = onehot.astype(BF16)

    gatest_ref[...] = _sigmoid(seg(C_G, C_G + LANES)).T[0:GATE_ROWS, :]
    sa_ref[...] = _silu(seg(C_A, C_A + ATTN_WIDTH)).astype(BF16)
    ga_ref[...] = _sigmoid(seg(C_MA, C_MA + D_MODEL)).astype(BF16)

    @pl.when(ti == 0)
    def _():
        carry_ref[...] = jnp.zeros_like(carry_ref)

    u = seg(C_CC, C_CC + CONV_WIDTH) * seg(C_CX, C_CX + CONV_WIDTH)
    r = lax.broadcasted_iota(jnp.int32, u.shape, 0)
    c7 = carry_ref[SUBLANES - 1:SUBLANES, :]
    c6 = carry_ref[SUBLANES - 2:SUBLANES - 1, :]
    um1 = jnp.where(r == 0, c7, pltpu.roll(u, 1, 0))
    um2 = jnp.where(r == 0, c6, jnp.where(r == 1, c7, pltpu.roll(u, 2, 0)))
    pb_ref[...] = _branch_b(seg, um2, um1, u, convw_ref, wbrb_ref).astype(BF16)
    tail = u[tm - SUBLANES:tm, :]
    carry_ref[...] = tail
    utail_ref[0] = tail


def _proj_prompt(x2d, shift, scale, g_pre, w_pad, rope_tabs, conv_w, w_br_b, batch, seq, tm):
    n = batch * seq
    nt = seq // tm
    row = lambda w: pl.BlockSpec((tm, w), lambda b, t: (b * nt + t, 0))
    per_b = lambda w: pl.BlockSpec((None, 1, w), lambda b, t: (b, 0, 0))
    const = lambda shp: pl.BlockSpec(shp, lambda b, t: (0,) * len(shp))
    tab = pl.BlockSpec((tm, LANES), lambda b, t: (t, 0))
    feat = lambda rows: pl.BlockSpec((None, rows, tm), lambda b, t: (b, 0, t))
    out_shapes = (
        jax.ShapeDtypeStruct((batch, N_HEADS, LANES, seq), BF16),
        jax.ShapeDtypeStruct((batch, 4 * KV_WIDTH, seq), F32),
        jax.ShapeDtypeStruct((n, 2 * KV_WIDTH), F32),
        jax.ShapeDtypeStruct((n, 2 * KV_WIDTH), F32),
        jax.ShapeDtypeStruct((n, 2 * LANES), BF16),
        jax.ShapeDtypeStruct((n, KV_WIDTH), BF16),
        jax.ShapeDtypeStruct((batch, KV_WIDTH, seq), BF16),
        jax.ShapeDtypeStruct((batch, KV_WIDTH, seq), BF16),
        jax.ShapeDtypeStruct((batch, GATE_ROWS, seq), F32),
        jax.ShapeDtypeStruct((n, ATTN_WIDTH), BF16),
        jax.ShapeDtypeStruct((n, D_MODEL), BF16),
        jax.ShapeDtypeStruct((n, D_MODEL), BF16),
        jax.ShapeDtypeStruct((batch, SUBLANES, CONV_WIDTH), F32),
    )
    out_specs = (
        pl.BlockSpec((None, N_HEADS, LANES, tm), lambda b, t: (b, 0, 0, t)), feat(4 * KV_WIDTH),
        row(2 * KV_WIDTH), row(2 * KV_WIDTH), row(2 * LANES), row(KV_WIDTH), feat(KV_WIDTH), feat(KV_WIDTH),
        feat(GATE_ROWS), row(ATTN_WIDTH), row(D_MODEL), row(D_MODEL),
        pl.BlockSpec((1, SUBLANES, CONV_WIDTH), lambda b, t: (b, 0, 0)),
    )
    return pl.pallas_call(
        _proj_prompt_kernel,
        grid=(batch, nt),
        in_specs=[row(D_MODEL), per_b(D_MODEL), per_b(D_MODEL), const((1, D_MODEL)), const((D_MODEL, IN_PAD)),
                  tab, tab, tab, const((CONV_K, CONV_WIDTH)), const((CONV_WIDTH, D_MODEL))],
        out_specs=out_specs,
        out_shape=out_shapes,
        scratch_shapes=[pltpu.VMEM((SUBLANES, CONV_WIDTH), F32)],
        compiler_params=pltpu.CompilerParams(dimension_semantics=("arbitrary", "arbitrary"),
                                             vmem_limit_bytes=VMEM_LIMIT),
        name="proj_prompt",
    )(x2d, shift[:, None, :], scale[:, None, :], g_pre, w_pad, *rope_tabs, conv_w, w_br_b)


def _proj_sample_kernel(x_ref, shift_ref, scale_ref, gpre_ref, w_ref, ra_ref, rp_ref, rm_ref, convw_ref, wbrb_ref,
                        cbuf_ref, qpad_ref, kvnew_ref, gates_ref, sa_ref, ga_ref, pb_ref, u_ref):
    seg, rope = _proj_common(x_ref, shift_ref, scale_ref, gpre_ref, w_ref, ra_ref, rp_ref, rm_ref)
    for n, head in enumerate(_padded_q_heads(seg, rope, HEAD_DIM ** -0.5)):
        qpad_ref[:, n * LANES:(n + 1) * LANES] = head.astype(BF16)
    pieces = _kv_pieces(seg, rope)
    for p in range(6):
        kvnew_ref[:, p * LANES:(p + 1) * LANES] = pieces[p]
    gates_ref[...] = _sigmoid(seg(C_G, C_G + LANES))
    sa_ref[...] = _silu(seg(C_A, C_A + ATTN_WIDTH)).astype(BF16)
    ga_ref[...] = _sigmoid(seg(C_MA, C_MA + D_MODEL)).astype(BF16)
    u = seg(C_CC, C_CC + CONV_WIDTH) * seg(C_CX, C_CX + CONV_WIDTH)
    um2 = cbuf_ref[:, 0:CONV_WIDTH]
    um1 = cbuf_ref[:, CONV_WIDTH:2 * CONV_WIDTH]
    pb_ref[...] = _branch_b(seg, um2, um1, u, convw_ref, wbrb_ref).astype(BF16)
    u_ref[...] = u


def _proj_sample(x2d, shift, scale, g_pre, w_pad, rope_tabs, conv_w, w_br_b, cbuf):
    n = x2d.shape[0]
    full = lambda shp: pl.BlockSpec(shp, lambda i: (0,) * len(shp))
    out_shapes = (
        jax.ShapeDtypeStruct((n, N_HEADS * LANES), BF16),
        jax.ShapeDtypeStruct((n, 6 * KV_WIDTH), F32),
        jax.ShapeDtypeStruct((n, LANES), F32),
        jax.ShapeDtypeStruct((n, ATTN_WIDTH), BF16),
        jax.ShapeDtypeStruct((n, D_MODEL), BF16),
        jax.ShapeDtypeStruct((n, D_MODEL), BF16),
        jax.ShapeDtypeStruct((n, CONV_WIDTH), F32),
    )
    return pl.pallas_call(
        _proj_sample_kernel,
        grid=(1,),
        in_specs=[full((n, D_MODEL)), full((n, D_MODEL)), full((n, D_MODEL)), full((1, D_MODEL)),
                  full((D_MODEL, IN_PAD)), full((1, LANES)), full((1, LANES)), full((1, LANES)),
                  full((CONV_K, CONV_WIDTH)), full((CONV_WIDTH, D_MODEL)), full((n, 2 * CONV_WIDTH))],
        out_specs=tuple(full(s.shape) for s in out_shapes),
        out_shape=out_shapes,
        compiler_params=pltpu.CompilerParams(dimension_semantics=("arbitrary",), vmem_limit_bytes=VMEM_LIMIT),
        name="proj_sample",
    )(x2d, shift, scale, g_pre, w_pad, *rope_tabs, conv_w, w_br_b, cbuf)


CHUNK_LANES = CMP_STRIDE * KV_WIDTH


def _compress_weights(pe_cmp, w_cmp1, w_cmp2):
    zeros = jnp.zeros((2, CMP_STRIDE, HEAD_DIM, CMP_HIDDEN), w_cmp1.dtype)

    def both_groups(w_half):
        w = w_half.reshape(2, CMP_STRIDE, HEAD_DIM, CMP_HIDDEN)
        g0 = jnp.concatenate([w, zeros], axis=2)
        g1 = jnp.concatenate([zeros, w], axis=2)
        return jnp.concatenate([g0, g1], axis=3).reshape(2, CHUNK_LANES, KV_HEADS * CMP_HIDDEN).astype(BF16)

    half = CMP_STRIDE * HEAD_DIM
    z2 = jnp.zeros_like(w_cmp2)
    w2 = jnp.concatenate([jnp.concatenate([w_cmp2, z2], axis=2), jnp.concatenate([z2, w_cmp2], axis=2)], axis=1)
    pe = jnp.concatenate([pe_cmp, pe_cmp], axis=2)
    return dict(w1_lo=both_groups(w_cmp1[:, :half]), w1_hi=both_groups(w_cmp1[:, half:]), w2=w2.astype(BF16),
                pe_lo=pe[:, :CMP_STRIDE].reshape(2, 1, CHUNK_LANES), pe_hi=pe[:, CMP_STRIDE:].reshape(2, 1, CHUNK_LANES))


def _compress_chunks(c, kind, pelo_ref, pehi_ref, w1lo_ref, w1hi_ref):
    lo = _dot((c + pelo_ref[kind]).astype(BF16), w1lo_ref[kind])
    hi = _dot((c + pehi_ref[kind]).astype(BF16), w1hi_ref[kind])
    return lo, hi


def _compress_prompt_kernel(kc_ref, vc_ref, pelo_ref, pehi_ref, w1lo_ref, w1hi_ref, w2_ref, kc_out_ref, vct_out_ref):
    n_chunk = kc_ref.shape[0] // CMP_STRIDE
    outs = []
    for kind, src_ref in enumerate((kc_ref, vc_ref)):
        c = jnp.concatenate([src_ref[pl.ds(r, n_chunk, stride=CMP_STRIDE), :] for r in range(CMP_STRIDE)], axis=1)
        lo, hi = _compress_chunks(c, kind, pelo_ref, pehi_ref, w1lo_ref, w1hi_ref)
        hid = lo + pltpu.roll(hi, n_chunk - 1, 0)
        outs.append(_dot(_silu(hid).astype(BF16), w2_ref[kind]))
    kc_out_ref[0] = outs[0].astype(BF16)
    vct_out_ref[0] = outs[1].T.astype(BF16)


def _compress_prompt(kvc, cw, batch, seq):
    n_chunk = seq // CMP_STRIDE
    const = lambda a: pl.BlockSpec(a.shape, lambda b: (0,) * a.ndim)
    return pl.pallas_call(
        _compress_prompt_kernel,
        grid=(batch,),
        in_specs=[
            pl.BlockSpec((seq, KV_WIDTH), lambda b: (b, 0)),
            pl.BlockSpec((seq, KV_WIDTH), lambda b: (b, 1)),
            const(cw["pe_lo"]), const(cw["pe_hi"]), const(cw["w1_lo"]), const(cw["w1_hi"]), const(cw["w2"]),
        ],
        out_specs=(pl.BlockSpec((1, n_chunk, KV_WIDTH), lambda b: (b, 0, 0)),
                   pl.BlockSpec((1, KV_WIDTH, n_chunk), lambda b: (b, 0, 0))),
        out_shape=(jax.ShapeDtypeStruct((batch, n_chunk, KV_WIDTH), BF16),
                   jax.ShapeDtypeStruct((batch, KV_WIDTH, n_chunk), BF16)),
        compiler_params=pltpu.CompilerParams(dimension_semantics=("arbitrary",), vmem_limit_bytes=VMEM_LIMIT),
        name="compress_prompt",
    )(kvc, kvc, cw["pe_lo"], cw["pe_hi"], cw["w1_lo"], cw["w1_hi"], cw["w2"])


def _split_bf16(x):
    hi = x.astype(BF16)
    lo = (x - hi.astype(F32)).astype(BF16)
    return hi, lo


def _masked_softmax_parts(s, valid):
    s = jnp.where(valid, s, -jnp.inf)
    m = jnp.max(s, axis=-1, keepdims=True)
    m = jnp.where(m == -jnp.inf, 0.0, m)
    e = jnp.exp(s - m)
    den = jnp.maximum(jnp.sum(e, axis=-1, keepdims=True), 1e-30)
    return e, den


COL_CHUNK = 256


def _skewed(n, stages):
    for step in range(n + len(stages) - 1):
        for si, stage in enumerate(stages):
            if 0 <= step - si < n:
                stage(step - si)


def _skewed_thunks(n, stages):
    return [functools.partial(stage, step - si) for step in range(n + len(stages) - 1)
            for si, stage in enumerate(stages) if 0 <= step - si < n]


def _top_k_bias_t(score, fillers=()):
    blk = lax.broadcasted_iota(jnp.int32, score.shape, 0).astype(F32)
    bias = jnp.full(score.shape, NEG_BIG, F32)
    fillers = list(fillers)
    for _ in range(N_SEL):
        m = jnp.max(score, axis=0, keepdims=True)
        idx = jnp.min(jnp.where(score == m, blk, float(LANES)), axis=0, keepdims=True)
        hit = blk == idx
        bias = jnp.where(hit, 0.0, bias)
        score = jnp.where(hit, -jnp.inf, score)
        for thunk in fillers[:2]:
            thunk()
        fillers = fillers[2:]
    for thunk in fillers:
        thunk()
    return bias


def _masked_softmax_parts_t(s, valid):
    s = jnp.where(valid, s, -jnp.inf)
    m = jnp.max(s, axis=0, keepdims=True)
    m = jnp.where(m == -jnp.inf, 0.0, m)
    e = jnp.exp2(s - m)
    den = jnp.maximum(jnp.sum(e, axis=0, keepdims=True), 1e-30)
    return e, den


def _attn_prompt_t_kernel(qt_ref, kaug_ref, vselt_ref, kwin_ref, vwint_ref, kc_ref, vct_ref, gatest_ref, ovlt_ref,
                          o_ref, qaugt_ref, m_ref, l_ref, acct_ref, oct_ref, owt_ref, s_ref):
    qb = pl.program_id(1)
    q0 = qb * Q_BLOCK
    n_cmp = kc_ref.shape[1]
    n_cols = N_HEADS * Q_BLOCK
    n_chunks = n_cols // COL_CHUNK
    heads_per_chunk = COL_CHUNK // Q_BLOCK

    for n in range(N_HEADS):
        qaugt_ref[0:LANES, n * Q_BLOCK:(n + 1) * Q_BLOCK] = qt_ref[n]

    def qpos_cols(cols):
        c = lax.broadcasted_iota(jnp.int32, (1, cols), 1)
        return q0 + (c & (Q_BLOCK - 1))

    def chunk_cols(cc):
        return slice(cc * COL_CHUNK, (cc + 1) * COL_CHUNK)

    kc = kc_ref[0]
    vct = vct_ref[0]
    c_end = lax.broadcasted_iota(jnp.int32, (n_cmp, 1), 0) * CMP_STRIDE + (CMP_BLOCK - 1)
    psum = [None] * KV_HEADS
    cs, cp = {}, {}

    def cmp_scores(cc):
        cs[cc] = _dot(kc, qaugt_ref[0:LANES, chunk_cols(cc)])

    def cmp_softmax(cc):
        e, den = _masked_softmax_parts_t(cs.pop(cc), c_end <= qpos_cols(COL_CHUNK))
        p = e * (1.0 / den)
        cp[cc] = p.astype(BF16)
        g = (cc * heads_per_chunk) // HEADS_PER_GROUP
        part = p[:, 0:Q_BLOCK]
        for h in range(1, heads_per_chunk):
            part = part + p[:, h * Q_BLOCK:(h + 1) * Q_BLOCK]
        psum[g] = part if psum[g] is None else psum[g] + part

    def cmp_values(cc):
        oct_ref[:, chunk_cols(cc)] = _dot(vct, cp.pop(cc))

    _skewed(n_chunks, (cmp_scores, cmp_softmax, cmp_values))

    cur = qpos_cols(KV_HEADS * Q_BLOCK) // SEL_BLOCK
    j = lax.broadcasted_iota(jnp.int32, (LANES, KV_HEADS * Q_BLOCK), 0)
    forced = (j == 0) | (j == cur) | (j == cur - 1)
    hi, lo = _split_bf16(jnp.concatenate(psum, axis=1))
    imp = _dot(ovlt_ref[...], hi) + _dot(ovlt_ref[...], lo)
    score = jnp.where(forced, FORCED_SCORE, jnp.where(j <= cur, imp, -1.0))

    wk = kwin_ref.shape[0] if kwin_ref.shape[0] < WINDOW + Q_BLOCK else WINDOW + Q_BLOCK
    start = pl.multiple_of(jnp.maximum(q0 - WINDOW, 0), Q_BLOCK)
    kw = kwin_ref[pl.ds(start, wk), :]
    vwt = vwint_ref[:, pl.ds(start, wk)]
    kwpos = start + lax.broadcasted_iota(jnp.int32, (wk, 1), 0)
    ws, we, wden = {}, {}, {}

    def win_scores(cc):
        ws[cc] = _dot(kw, qaugt_ref[0:LANES, chunk_cols(cc)])

    def win_softmax(cc):
        dt = qpos_cols(COL_CHUNK) - kwpos
        e, wden[cc] = _masked_softmax_parts_t(ws.pop(cc), (dt >= 0) & (dt < WINDOW))
        we[cc] = e.astype(BF16)

    def win_values(cc):
        owt_ref[:, chunk_cols(cc)] = _dot(vwt, we.pop(cc)) * (1.0 / wden.pop(cc))

    bias = _top_k_bias_t(score, _skewed_thunks(n_chunks, (win_scores, win_softmax, win_values))).astype(BF16)
    for n in range(N_HEADS):
        g = n // HEADS_PER_GROUP
        qaugt_ref[LANES:2 * LANES, n * Q_BLOCK:(n + 1) * Q_BLOCK] = bias[:, g * Q_BLOCK:(g + 1) * Q_BLOCK]

    m_ref[...] = jnp.full(m_ref.shape, NEG_BIG, F32)
    l_ref[...] = jnp.zeros(l_ref.shape, F32)
    acct_ref[...] = jnp.zeros(acct_ref.shape, F32)
    kt_last = (q0 + Q_BLOCK - 1) // KV_TILE

    def key_tile(kt):
        return pl.multiple_of(kt * KV_TILE, KV_TILE)

    def scores(kt, cc):
        s_ref[:, chunk_cols(cc)] = _dot(kaug_ref[pl.ds(key_tile(kt), KV_TILE), :], qaugt_ref[:, chunk_cols(cc)])

    def sel_tile(kt, causal, issue_next):
        k0 = key_tile(kt)
        vt = vselt_ref[:, pl.ds(k0, KV_TILE)]
        pb, alpha = {}, {}

        def softmax(cc):
            cols = chunk_cols(cc)
            s = s_ref[:, cols]
            if causal:
                kpos = k0 + lax.broadcasted_iota(jnp.int32, (KV_TILE, 1), 0)
                s = jnp.where(kpos <= qpos_cols(COL_CHUNK), s, NEG_BIG)
            if issue_next:
                scores(kt + 1, cc)
            m_old = m_ref[:, cols]
            m_new = jnp.maximum(m_old, jnp.max(s, axis=0, keepdims=True))
            alpha[cc] = jnp.exp2(m_old - m_new)
            p = jnp.exp2(s - m_new)
            l_ref[:, cols] = alpha[cc] * l_ref[:, cols] + jnp.sum(p, axis=0, keepdims=True)
            m_ref[:, cols] = m_new
            pb[cc] = p.astype(BF16)

        def values(cc):
            cols = chunk_cols(cc)
            acct_ref[:, cols] = alpha.pop(cc) * acct_ref[:, cols] + _dot(vt, pb.pop(cc))

        _skewed(n_chunks, (softmax, values))

    for cc in range(n_chunks):
        scores(0, cc)

    def body(kt, carry):
        sel_tile(kt, False, True)
        return carry

    lax.fori_loop(0, kt_last, body, 0)
    sel_tile(kt_last, True, False)

    gates = gatest_ref[...]
    for pair in range(N_HEADS // 2):
        g = (2 * pair) // HEADS_PER_GROUP
        feat = slice(g * HEAD_DIM, (g + 1) * HEAD_DIM)
        halves = []
        for n in (2 * pair, 2 * pair + 1):
            cols = slice(n * Q_BLOCK, (n + 1) * Q_BLOCK)
            o_sel = acct_ref[feat, cols] * (1.0 / l_ref[:, cols])
            halves.append(gates[3 * n:3 * n + 1, :] * oct_ref[feat, cols] + gates[3 * n + 1:3 * n + 2, :] * o_sel
                          + gates[3 * n + 2:3 * n + 3, :] * owt_ref[feat, cols])
        o_ref[:, pair * LANES:(pair + 1) * LANES] = jnp.concatenate(halves, axis=0).T


def _attn_prompt_t(qt, ksel, kwinb, vselt, vwint, kc, vct, gatest, ovlt, batch, seq):
    nq = seq // Q_BLOCK
    n_cmp = kc.shape[1]
    n_cols = N_HEADS * Q_BLOCK
    per_b = lambda a: pl.BlockSpec((None,) + a.shape[1:], lambda b, i: (b,) + (0,) * (a.ndim - 1))
    return pl.pallas_call(
        _attn_prompt_t_kernel,
        grid=(batch, nq),
        in_specs=[
            pl.BlockSpec((None, N_HEADS, LANES, Q_BLOCK), lambda b, i: (b, 0, 0, i)),
            pl.BlockSpec((seq, 2 * LANES), lambda b, i: (b, 0)),
            per_b(vselt),
            pl.BlockSpec((seq, LANES), lambda b, i: (b, 0)),
            per_b(vwint),
            pl.BlockSpec((1, n_cmp, LANES), lambda b, i: (b, 0, 0)),
            pl.BlockSpec((1, LANES, n_cmp), lambda b, i: (b, 0, 0)),
            pl.BlockSpec((None, GATE_ROWS, Q_BLOCK), lambda b, i: (b, 0, i)),
            pl.BlockSpec((LANES, n_cmp), lambda b, i: (0, 0)),
        ],
        out_specs=pl.BlockSpec((Q_BLOCK, ATTN_WIDTH), lambda b, i: (b * nq + i, 0)),
        out_shape=jax.ShapeDtypeStruct((batch * seq, ATTN_WIDTH), F32),
        scratch_shapes=[
            pltpu.VMEM((2 * LANES, n_cols), BF16),
            pltpu.VMEM((1, n_cols), F32),
            pltpu.VMEM((1, n_cols), F32),
            pltpu.VMEM((LANES, n_cols), F32),
            pltpu.VMEM((LANES, n_cols), F32),
            pltpu.VMEM((LANES, n_cols), F32),
            pltpu.VMEM((KV_TILE, n_cols), F32),
        ],
        compiler_params=pltpu.CompilerParams(dimension_semantics=("arbitrary", "arbitrary"),
                                             vmem_limit_bytes=VMEM_LIMIT),
        name="attn_prompt",
    )(qt, ksel, vselt, kwinb, vwint, kc, vct, gatest, ovlt)


def _finish_kernel(x_ref, o_ref, sa_ref, ga_ref, pb_ref, gate_ref, gpost_ref, wbra_ref, wout_ref, y_ref):
    ya = _dot((o_ref[...] * sa_ref[...]).astype(BF16), wbra_ref[...])
    mix = ga_ref[...] * ya + pb_ref[...]
    o = _dot(mix.astype(BF16), wout_ref[...])
    ms = jnp.mean(o * o, axis=-1, keepdims=True)
    on = o * lax.rsqrt(ms + RMS_EPS) * gpost_ref[...]
    y_ref[...] = x_ref[...] + gate_ref[...] * on


def _finish(x2d, o_attn, sa, ga, pb, gate, g_post, w_br_a, w_out, tm, rows_per_gate):
    n = x2d.shape[0]
    row = lambda w: pl.BlockSpec((tm, w), lambda i: (i, 0))
    const = lambda shp: pl.BlockSpec(shp, lambda i: (0,) * len(shp))
    if rows_per_gate == 1:
        gate_spec = row(D_MODEL)
    else:
        tiles_per_gate = rows_per_gate // tm
        gate = gate[:, None, :]
        gate_spec = pl.BlockSpec((None, 1, D_MODEL), lambda i: (i // tiles_per_gate, 0, 0))
    return pl.pallas_call(
        _finish_kernel,
        grid=(n // tm,),
        in_specs=[row(D_MODEL), row(ATTN_WIDTH), row(ATTN_WIDTH), row(D_MODEL), row(D_MODEL), gate_spec,
                  const((1, D_MODEL)), const((ATTN_WIDTH, D_MODEL)), const((D_MODEL, D_MODEL))],
        out_specs=row(D_MODEL),
        out_shape=jax.ShapeDtypeStruct((n, D_MODEL), F32),
        compiler_params=pltpu.CompilerParams(dimension_semantics=("arbitrary",), vmem_limit_bytes=VMEM_LIMIT),
        name="finish",
    )(x2d, o_attn, sa, ga, pb, gate, g_post, w_br_a, w_out)


def _attn_decode_kernel(pt_ref, *refs, n_req, n_pages, page_size, wbuf):
    del pt_ref
    page_refs = refs[:n_req * n_pages]
    (win_ref, q_ref, kvnew_ref, gates_ref, pelo_ref, pehi_ref, w1lo_ref, w1hi_ref, w2_ref, ovl_ref, expand_ref,
     o_ref, winout_ref, kcmp_ref, vcmp_ref) = refs[n_req * n_pages:]
    past_len = n_pages * page_size
    n_chunk = past_len // CMP_STRIDE
    reqs = range(n_req)
    heads = lambda b: slice(b * N_HEADS, (b + 1) * N_HEADS)
    per_req = lambda fn: jnp.concatenate([fn(b) for b in reqs], axis=0)
    q8f = per_req(lambda b: q_ref[b])
    q8 = q8f.astype(BF16)
    new_row = lambda which: per_req(lambda b: jnp.broadcast_to(kvnew_ref[b, which:which + 1, :], (N_HEADS, LANES)))

    def stream(ref, which):
        slab = ref[which * KV_HEADS:(which + 1) * KV_HEADS]
        return slab.reshape(KV_WIDTH, slab.shape[-1])

    for i, pr in enumerate(page_refs):
        rows = slice(i * page_size, (i + 1) * page_size)
        kcmp_ref[rows, :] = stream(pr, 0).T
        vcmp_ref[rows, :] = stream(pr, 1).T

    kvc = []
    for kind, src_ref in enumerate((kcmp_ref, vcmp_ref)):
        c = jnp.concatenate([src_ref[pl.ds(r, n_req * n_chunk, stride=CMP_STRIDE), :] for r in range(CMP_STRIDE)],
                            axis=1)
        lo, hi = _compress_chunks(c, kind, pelo_ref, pehi_ref, w1lo_ref, w1hi_ref)
        hid = lo + pltpu.roll(hi, n_req * n_chunk - 1, 0)
        kvc.append(_dot(_silu(hid).astype(BF16), w2_ref[kind]).astype(BF16))
    kc, vc = kvc
    chunks = lambda b: slice(b * n_chunk, (b + 1) * n_chunk)

    c_end = lax.broadcasted_iota(jnp.int32, (1, n_chunk), 1) * CMP_STRIDE + (CMP_BLOCK - 1)
    e, den = _masked_softmax_parts(per_req(lambda b: _dot_nt(q8[heads(b)], kc[chunks(b)])), c_end <= past_len)
    p = e * (1.0 / den)
    pb = p.astype(BF16)
    o_cmp = per_req(lambda b: _dot(pb[heads(b)], vc[chunks(b)]))

    row = lax.broadcasted_iota(jnp.int32, p.shape, 0)
    in_g0 = (row & (N_HEADS - 1)) < HEADS_PER_GROUP

    def group_sums(b):
        pr_, g0_ = p[heads(b)], in_g0[heads(b)]
        g0 = jnp.sum(jnp.where(g0_, pr_, 0.0), axis=0, keepdims=True)
        g1 = jnp.sum(jnp.where(g0_, 0.0, pr_), axis=0, keepdims=True)
        return jnp.where(g0_, g0, g1)

    hi, lo = _split_bf16(per_req(group_sums))
    imp = _dot(hi, ovl_ref[...]) + _dot(lo, ovl_ref[...])
    cur = past_len // SEL_BLOCK
    j = lax.broadcasted_iota(jnp.int32, imp.shape, 1)
    forced = (j == 0) | (j == cur) | (j == cur - 1)
    score = jnp.where(j > cur, -jnp.inf, jnp.where(forced, FORCED_SCORE, imp))
    bias_keys = _dot(_top_k_bias_t(score.T).T.astype(BF16), expand_ref[...])

    pages = lambda b: page_refs[b * n_pages:(b + 1) * n_pages]
    s = per_req(lambda b: jnp.concatenate([_dot(q8[heads(b)], stream(pr, 2).astype(BF16)) for pr in pages(b)],
                                          axis=1)) + bias_keys
    s_new = jnp.sum(q8f * new_row(2), axis=1, keepdims=True)
    m = jnp.maximum(jnp.max(s, axis=1, keepdims=True), s_new)
    e = jnp.exp(s - m)
    e_new = jnp.exp(s_new - m)
    den = jnp.sum(e, axis=1, keepdims=True) + e_new
    eb = e.astype(BF16)

    def sel_values(b):
        acc = None
        for i, pr in enumerate(pages(b)):
            part = _dot_nt(eb[heads(b), i * page_size:(i + 1) * page_size], stream(pr, 3).astype(BF16))
            acc = part if acc is None else acc + part
        return acc

    o_sel = (per_req(sel_values) + e_new * new_row(3)) * (1.0 / den)

    kwpos = past_len - wbuf + lax.broadcasted_iota(jnp.int32, (1, wbuf), 1)
    dt = past_len - kwpos
    valid = (dt >= 0) & (dt < WINDOW) & (kwpos >= 0)
    kw_t = [stream(win_ref.at[b], 0) for b in reqs]
    vw_t = [stream(win_ref.at[b], 1) for b in reqs]
    s = jnp.where(valid, per_req(lambda b: _dot(q8[heads(b)], kw_t[b].astype(BF16))), -jnp.inf)
    s_new = jnp.sum(q8f * new_row(4), axis=1, keepdims=True)
    m = jnp.maximum(jnp.max(s, axis=1, keepdims=True), s_new)
    e = jnp.exp(s - m)
    e_new = jnp.exp(s_new - m)
    den = jnp.sum(e, axis=1, keepdims=True) + e_new
    eb = e.astype(BF16)
    o_win = (per_req(lambda b: _dot_nt(eb[heads(b)], vw_t[b].astype(BF16))) + e_new * new_row(5)) * (1.0 / den)

    gates = per_req(lambda b: gates_ref[b])
    o = gates[:, 0:1] * o_cmp + gates[:, 1:2] * o_sel + gates[:, 2:3] * o_win
    o = jnp.where(in_g0, o, pltpu.roll(o, HEAD_DIM, 1))
    lane = lax.broadcasted_iota(jnp.int32, (KV_WIDTH, wbuf), 1)
    for b in reqs:
        o_ref[b] = o[heads(b)]
        new_cols = jnp.concatenate([kvnew_ref[b], jnp.zeros((SUBLANES - 6, LANES), F32)], axis=0).T
        for kind, old in enumerate((kw_t[b], vw_t[b])):
            shifted = jnp.where(lane == wbuf - 1, new_cols[:, 4 + kind:5 + kind], pltpu.roll(old, wbuf - 1, 1))
            winout_ref[b, kind * KV_HEADS:(kind + 1) * KV_HEADS] = shifted.reshape(KV_HEADS, HEAD_DIM, wbuf)


DECODE_REQS_PER_STEP = 2


def _attn_decode(cache_t, page_table, win_t, q8, kvnew, gates8, cw, ovl, expand):
    page_size = cache_t.shape[-1]
    batch, n_pages = page_table.shape
    wbuf = win_t.shape[-1]
    past_len = n_pages * page_size
    n_req = DECODE_REQS_PER_STEP if batch % DECODE_REQS_PER_STEP == 0 else 1

    def page_spec(r, k):
        return pl.BlockSpec((None, PAGE_ROWS, HEAD_DIM, page_size), lambda i, pt: (pt[i * n_req + r, k], 0, 0, 0))

    per_step = lambda a: pl.BlockSpec((n_req,) + a.shape[1:], lambda i, pt: (i,) + (0,) * (a.ndim - 1))
    const = lambda a: pl.BlockSpec(a.shape, lambda i, pt: (0,) * a.ndim, pipeline_mode=pl.Buffered(1))
    consts = [cw["pe_lo"], cw["pe_hi"], cw["w1_lo"], cw["w1_hi"], cw["w2"], ovl, expand]
    grid_spec = pltpu.PrefetchScalarGridSpec(
        num_scalar_prefetch=1,
        grid=(batch // n_req,),
        in_specs=[page_spec(r, k) for r in range(n_req) for k in range(n_pages)]
        + [per_step(win_t), per_step(q8), per_step(kvnew), per_step(gates8)]
        + [const(a) for a in consts],
        out_specs=(pl.BlockSpec((n_req, N_HEADS, LANES), lambda i, pt: (i, 0, 0)), per_step(win_t)),
        scratch_shapes=[pltpu.VMEM((n_req * past_len, LANES), F32), pltpu.VMEM((n_req * past_len, LANES), F32)],
    )
    o8, win_out = pl.pallas_call(
        functools.partial(_attn_decode_kernel, n_req=n_req, n_pages=n_pages, page_size=page_size, wbuf=wbuf),
        grid_spec=grid_spec,
        out_shape=(jax.ShapeDtypeStruct((batch, N_HEADS, LANES), F32),
                   jax.ShapeDtypeStruct(win_t.shape, F32)),
        compiler_params=pltpu.CompilerParams(dimension_semantics=("arbitrary",), vmem_limit_bytes=VMEM_LIMIT),
        name="attn_decode",
    )(page_table, *([cache_t] * (n_req * n_pages)), win_t, q8, kvnew, gates8, *consts)
    return o8, win_out


def _overlap_matrix(n_cmp_pad, n_cmp, n_selb):
    cs = jnp.arange(n_cmp_pad) * CMP_STRIDE
    ss = jnp.arange(LANES) * SEL_BLOCK
    ov = (cs[:, None] < ss[None, :] + SEL_BLOCK) & (cs[:, None] + CMP_BLOCK > ss[None, :])
    ov = ov & (jnp.arange(n_cmp_pad) < n_cmp)[:, None] & (jnp.arange(LANES) < n_selb)[None, :]
    return ov.astype(BF16)


def _prompt_layer(x, mod, wts):
    batch, seq, _ = x.shape
    x2d = x.reshape(batch * seq, D_MODEL)
    shift, scale, gate = mod[:, 0:D_MODEL], mod[:, D_MODEL:2 * D_MODEL], mod[:, 2 * D_MODEL:]
    tabs = _rope_tables(jnp.arange(seq))
    tm = min(256, seq)
    (qt, kvt, kvc, kwin, ksel, kwinb, vselt, vwint, gatest, sa, ga, pb, utail) = _proj_prompt(
        x2d, shift, scale, wts["g_pre"], wts["w_in"], tabs, wts["conv_w"], wts["w_br_b"], batch, seq, tm)
    kc, vct = _compress_prompt(kvc, wts["cmp"], batch, seq)
    n_chunk = seq // CMP_STRIDE
    ovlt = _overlap_matrix(n_chunk, n_chunk - 1, -(-seq // SEL_BLOCK)).T
    o_attn = _attn_prompt_t(qt, ksel, kwinb, vselt, vwint, kc, vct, gatest, ovlt, batch, seq)
    y = _finish(x2d, o_attn, sa, ga, pb, gate, wts["g_post"], wts["w_br_a"], wts["w_out"], tm, seq)
    n_keep = min(WINDOW, seq)
    return (y.reshape(batch, seq, D_MODEL),
            jnp.transpose(kvt.reshape(batch, 4, KV_HEADS, HEAD_DIM, seq), (0, 4, 1, 2, 3)),
            kwin.reshape(batch, seq, 2, KV_HEADS, HEAD_DIM)[:, seq - n_keep:],
            utail[:, SUBLANES - (CONV_K - 1):])


def _sample_layer(x, mod, cache, page_table, win, conv_state, wts):
    batch, dec_seq, _ = x.shape
    assert dec_seq == 1
    n_pages, page_size, wbuf = page_table.shape[1], cache.shape[1], win.shape[1]
    past_len = n_pages * page_size
    assert past_len % SEL_BLOCK == 0 and wbuf == WINDOW and past_len // SEL_BLOCK < LANES
    x2d = x.reshape(batch, D_MODEL)
    shift, scale, gate = mod[:, 0:D_MODEL], mod[:, D_MODEL:2 * D_MODEL], mod[:, 2 * D_MODEL:]
    tabs = _rope_tables(jnp.full((1,), past_len, jnp.int32))
    cbuf = conv_state.reshape(batch, (CONV_K - 1) * CONV_WIDTH)
    qpad, kvnew, gates, sa, ga, pb, u = _proj_sample(
        x2d, shift, scale, wts["g_pre"], wts["w_in"], tabs, wts["conv_w"], wts["w_br_b"], cbuf)
    n_gate = N_HEADS * 3
    gates8 = jnp.pad(gates[:, :n_gate].reshape(batch, N_HEADS, 3), ((0, 0), (0, 0), (0, LANES - 3)))
    n_chunk = past_len // CMP_STRIDE
    ovl = _overlap_matrix(n_chunk, n_chunk - 1, past_len // SEL_BLOCK + 1)
    expand = (jnp.arange(LANES)[:, None] == (jnp.arange(past_len) // SEL_BLOCK)[None, :]).astype(BF16)
    cache_t = jnp.transpose(cache, (0, 2, 3, 4, 1)).reshape(cache.shape[0], PAGE_ROWS, HEAD_DIM, page_size)
    win_t = jnp.transpose(win, (0, 2, 3, 4, 1)).reshape(batch, WIN_ROWS, HEAD_DIM, wbuf)
    o8, win_out = _attn_decode(cache_t, page_table, win_t, qpad.reshape(batch, N_HEADS, LANES).astype(F32),
                               kvnew.reshape(batch, 6, LANES), gates8, wts["cmp"], ovl, expand)
    win_out = jnp.transpose(win_out.reshape(batch, 2, KV_HEADS, HEAD_DIM, wbuf), (0, 4, 1, 2, 3))
    y = _finish(x2d, o8[:, :, :HEAD_DIM].reshape(batch, ATTN_WIDTH), sa, ga, pb, gate, wts["g_post"], wts["w_br_a"],
                wts["w_out"], batch, 1)
    return (y.reshape(batch, 1, D_MODEL),
            kvnew[:, :4 * KV_WIDTH].reshape(batch, 1, 4, KV_HEADS, HEAD_DIM),
            win_out,
            jnp.stack([conv_state[:, CONV_K - 2], u], axis=1))


def _prep_weights(w_ada, b_ada, g_pre, g_post, w_in, pe_cmp, w_cmp1, w_cmp2, conv_w, w_br_a, w_br_b, w_out):
    n_unpadded_gate = HEADS_PER_GROUP * KV_HEADS * 3
    w_pad = jnp.concatenate(
        [w_in[:, :C_G + n_unpadded_gate], jnp.zeros((D_MODEL, LANES - n_unpadded_gate), w_in.dtype),
         w_in[:, C_G + n_unpadded_gate:]], axis=1).astype(BF16)
    half = CMP_STRIDE * HEAD_DIM
    return dict(
        w_ada=w_ada, b_ada=b_ada, g_pre=g_pre.reshape(1, -1), g_post=g_post.reshape(1, -1), w_in=w_pad,
        cmp=_compress_weights(pe_cmp, w_cmp1, w_cmp2), conv_w=conv_w,
        w_br_a=w_br_a.astype(BF16), w_br_b=w_br_b.astype(BF16), w_out=w_out.astype(BF16))


def kernel(x_prompt, x_sample, cache_kv_pages, state_win_kv, state_conv, page_table, c_prompt, c_sample, w_ada, b_ada, g_pre, g_post, w_in, pe_cmp, w_cmp1, w_cmp2, conv_w, w_br_a, w_br_b, w_out):
    depth = w_in.shape[0]
    assert depth == 1
    wts = _prep_weights(w_ada[0], b_ada[0], g_pre[0], g_post[0], w_in[0], pe_cmp[0], w_cmp1[0], w_cmp2[0],
                        conv_w[0], w_br_a[0], w_br_b[0], w_out[0])
    n_prompt = c_prompt.shape[0]
    mod = _ada(jnp.concatenate([c_prompt, c_sample], axis=0), wts["w_ada"], wts["b_ada"])
    yp, kvp, wp, cp = _prompt_layer(x_prompt, mod[:n_prompt], wts)
    ys, kvs, ws, cs = _sample_layer(x_sample, mod[n_prompt:], cache_kv_pages[0], page_table, state_win_kv[0],
                                    state_conv[0], wts)
    return (yp, ys, kvp[None], wp[None], cp[None], kvs[None], ws[None], cs[None])
```

```python
import functools

import jax
import jax.numpy as jnp
from jax import lax
from jax.experimental import pallas as pl
from jax.experimental.pallas import tpu as pltpu

F32 = jnp.float32
BF16 = jnp.bfloat16

D_MODEL = 1024
N_HEADS = 8
KV_HEADS = 2
HEADS_PER_GROUP = N_HEADS // KV_HEADS
HEAD_DIM = 64
ROPE_DIM = HEAD_DIM // 4
ROPE_THETA = 500000.0
CMP_BLOCK = 32
CMP_STRIDE = 16
CMP_HIDDEN = 256
SEL_BLOCK = 64
N_SEL = 16
WINDOW = 512
Q_BLOCK = 256
CONV_WIDTH = D_MODEL // 2
CONV_K = 3
ATTN_WIDTH = N_HEADS * HEAD_DIM
KV_WIDTH = KV_HEADS * HEAD_DIM
RMS_EPS = 1e-6

LANES = 128
SUBLANES = 8
VMEM_LIMIT = 56 * 1024 * 1024

C_Q = 0
C_KV = C_Q + ATTN_WIDTH
C_G = C_KV + 6 * KV_WIDTH
C_A = C_G + LANES
C_CB = C_A + ATTN_WIDTH
C_CC = C_CB + CONV_WIDTH
C_CX = C_CC + CONV_WIDTH
C_CG = C_CX + CONV_WIDTH
C_MA = C_CG + CONV_WIDTH
C_MB = C_MA + D_MODEL
IN_PAD = C_MB + D_MODEL

PAGE_ROWS = 4 * KV_HEADS
WIN_ROWS = 2 * KV_HEADS

GATE_ROWS = 32
LOG2_E = 1.4426950408889634

NEG_BIG = -1e30
KV_TILE = 512


def _sigmoid(x):
    return 1.0 / (1.0 + jnp.exp(-x))


def _silu(x):
    return x * _sigmoid(x)


def _dot(a, b):
    return jnp.dot(a, b, preferred_element_type=F32)


def _dot_nt(a, b):
    return lax.dot_general(a, b, (((1,), (1,)), ((), ())), preferred_element_type=F32)


def _rope_cos_sin(pos):
    half = ROPE_DIM // 2
    inv = ROPE_THETA ** (-jnp.arange(half, dtype=F32) / half)
    ang = pos.astype(F32)[:, None] * inv[None, :]
    return jnp.concatenate([jnp.cos(ang), jnp.sin(ang)], axis=1)


def _rope_lane_tables(cs):
    half = ROPE_DIM // 2
    shape = (cs.shape[0], LANES)
    d = lax.broadcasted_iota(jnp.int32, shape, 1) & (HEAD_DIM - 1)
    a, p, m = jnp.ones(shape, F32), jnp.zeros(shape, F32), jnp.zeros(shape, F32)
    for f in range(half):
        cos_f, sin_f = cs[:, f:f + 1], cs[:, half + f:half + f + 1]
        a = jnp.where(d == f, cos_f, jnp.where(d == f + half, cos_f, a))
        p = jnp.where(d == f + half, sin_f, p)
        m = jnp.where(d == f, -sin_f, m)
    return a, p, m


def _rope(x, ra, rp, rm):
    half = ROPE_DIM // 2
    return x * ra + pltpu.roll(x, half, 1) * rp + pltpu.roll(x, LANES - half, 1) * rm


def _ada_kernel(c_ref, w_ref, b_ref, o_ref):
    c = _silu(c_ref[...]).astype(BF16)
    o_ref[...] = _dot(c, w_ref[...].astype(BF16)) + b_ref[...]


def _ada(c_all, w_ada, b_ada):
    n = c_all.shape[0]
    tn = 512
    return pl.pallas_call(
        _ada_kernel,
        grid=(3 * D_MODEL // tn,),
        in_specs=[
            pl.BlockSpec((n, D_MODEL), lambda j: (0, 0)),
            pl.BlockSpec((D_MODEL, tn), lambda j: (0, j)),
            pl.BlockSpec((1, tn), lambda j: (0, j)),
        ],
        out_specs=pl.BlockSpec((n, tn), lambda j: (0, j)),
        out_shape=jax.ShapeDtypeStruct((n, 3 * D_MODEL), F32),
        compiler_params=pltpu.CompilerParams(dimension_semantics=("arbitrary",), vmem_limit_bytes=VMEM_LIMIT),
        name="ada",
    )(c_all, w_ada, b_ada.reshape(1, -1))


def _proj_common(x_ref, shift_ref, scale_ref, gpre_ref, w_ref, cs_ref):
    x = x_ref[...]
    ms = jnp.mean(x * x, axis=-1, keepdims=True)
    xn = x * lax.rsqrt(ms + RMS_EPS) * gpre_ref[...]
    h = xn * (1.0 + scale_ref[...]) + shift_ref[...]
    hb = h.astype(BF16)

    def seg(lo, hi):
        return _dot_nt(hb, w_ref[lo:hi, :])

    return seg, _rope_lane_tables(cs_ref[...])


def _padded_q_heads(seg, rope, scale):
    zq = seg(C_Q, C_Q + ATTN_WIDTH)
    lane = lax.broadcasted_iota(jnp.int32, (zq.shape[0], LANES), 1)
    lower = lane < HEAD_DIM
    heads = []
    for j in range(ATTN_WIDTH // LANES):
        c = _rope(zq[:, j * LANES:(j + 1) * LANES], *rope) * scale
        r = pltpu.roll(c, HEAD_DIM, 1)
        if (2 * j) // HEADS_PER_GROUP == 0:
            heads += [jnp.where(lower, c, 0.0), jnp.where(lower, r, 0.0)]
        else:
            heads += [jnp.where(lower, 0.0, r), jnp.where(lower, 0.0, c)]
    return heads


def _kv_pieces(seg, rope):
    zkv = seg(C_KV, C_KV + 6 * KV_WIDTH)
    pieces = []
    for p in range(6):
        c = zkv[:, p * LANES:(p + 1) * LANES]
        pieces.append(_rope(c, *rope) if p % 2 == 0 else c)
    return pieces


def _branch_b(seg, um2, um1, u, convw_ref, wbrb_ref):
    cb = seg(C_CB, C_CB + CONV_WIDTH)
    conv = convw_ref[0:1, :] * um2
    conv = conv + convw_ref[1:2, :] * um1
    conv = conv + convw_ref[2:3, :] * u
    ybin = cb * conv * _silu(seg(C_CG, C_CG + CONV_WIDTH))
    yb = _dot(ybin.astype(BF16), wbrb_ref[...])
    gb = _sigmoid(seg(C_MB, C_MB + D_MODEL))
    return gb * yb


def _proj_prompt_kernel(x_ref, shift_ref, scale_ref, gpre_ref, w_ref, cs_ref, convw_ref, wbrb_ref,
                        qt_ref, kvt_ref, kvcmp_ref, kwin_ref, ksel_ref, kwinb_ref, vselt_ref, vwint_ref, gatest_ref,
                        sa_ref, ga_ref, pb_ref, utail_ref, carry_ref):
    ti = pl.program_id(1)
    tm = x_ref.shape[0]
    seg, rope = _proj_common(x_ref, shift_ref, scale_ref, gpre_ref, w_ref, cs_ref)
    for n, head in enumerate(_padded_q_heads(seg, rope, HEAD_DIM ** -0.5 * LOG2_E)):
        qt_ref[n] = head.T.astype(BF16)

    pieces = _kv_pieces(seg, rope)
    pieces_t = [p.T for p in pieces]
    for p in range(4):
        kvt_ref[p * KV_WIDTH:(p + 1) * KV_WIDTH, :] = pieces_t[p]
    kvcmp_ref[:, 0:LANES] = pieces[0]
    kvcmp_ref[:, LANES:2 * LANES] = pieces[1]
    kwin_ref[0:KV_WIDTH, :] = pieces_t[4]
    kwin_ref[KV_WIDTH:2 * KV_WIDTH, :] = pieces_t[5]
    kwinb_ref[...] = pieces[4].astype(BF16)
    vselt_ref[...] = pieces_t[3].astype(BF16)
    vwint_ref[...] = pieces_t[5].astype(BF16)
    row = ti * tm + lax.broadcasted_iota(jnp.int32, (tm, LANES), 0)
    lane = lax.broadcasted_iota(jnp.int32, (tm, LANES), 1)
    onehot = jnp.where(lane == row // SEL_BLOCK, 1.0, 0.0)
    ksel_ref[:, 0:LANES] = pieces[2].astype(BF16)
    ksel_ref[:, LANES:2 * LANES] = onehot.astype(BF16)

    gatest_ref[...] = _sigmoid(seg(C_G, C_G + LANES)).T[0:GATE_ROWS, :]
    sa_ref[...] = _silu(seg(C_A, C_A + ATTN_WIDTH)).astype(BF16)
    ga_ref[...] = _sigmoid(seg(C_MA, C_MA + D_MODEL)).astype(BF16)

    @pl.when(ti == 0)
    def _():
        carry_ref[...] = jnp.zeros_like(carry_ref)

    u = seg(C_CC, C_CC + CONV_WIDTH) * seg(C_CX, C_CX + CONV_WIDTH)
    r = lax.broadcasted_iota(jnp.int32, u.shape, 0)
    c7 = carry_ref[SUBLANES - 1:SUBLANES, :]
    c6 = carry_ref[SUBLANES - 2:SUBLANES - 1, :]
    um1 = jnp.where(r == 0, c7, pltpu.roll(u, 1, 0))
    um2 = jnp.where(r == 0, c6, jnp.where(r == 1, c7, pltpu.roll(u, 2, 0)))
    pb_ref[...] = _branch_b(seg, um2, um1, u, convw_ref, wbrb_ref).astype(BF16)
    tail = u[tm - SUBLANES:tm, :]
    carry_ref[...] = tail
    utail_ref[0] = tail


def _proj_prompt(x2d, shift, scale, g_pre, w_pad, rope_cs, conv_w, w_br_b, batch, seq, tm):
    n = batch * seq
    nt = seq // tm
    n_keep = min(WINDOW, seq)
    assert n_keep % tm == 0
    row = lambda w: pl.BlockSpec((tm, w), lambda b, t: (b * nt + t, 0))
    per_b = lambda w: pl.BlockSpec((None, 1, w), lambda b, t: (b, 0, 0))
    const = lambda shp: pl.BlockSpec(shp, lambda b, t: (0,) * len(shp))
    tab = pl.BlockSpec((tm, ROPE_DIM), lambda b, t: (t, 0))
    feat = lambda rows: pl.BlockSpec((None, rows, tm), lambda b, t: (b, 0, t))
    out_shapes = (
        jax.ShapeDtypeStruct((batch, N_HEADS, LANES, seq), BF16),
        jax.ShapeDtypeStruct((batch, 4 * KV_WIDTH, seq), F32),
        jax.ShapeDtypeStruct((n, 2 * KV_WIDTH), F32),
        jax.ShapeDtypeStruct((batch, 2 * KV_WIDTH, n_keep), F32),
        jax.ShapeDtypeStruct((n, 2 * LANES), BF16),
        jax.ShapeDtypeStruct((n, KV_WIDTH), BF16),
        jax.ShapeDtypeStruct((batch, KV_WIDTH, seq), BF16),
        jax.ShapeDtypeStruct((batch, KV_WIDTH, seq), BF16),
        jax.ShapeDtypeStruct((batch, GATE_ROWS, seq), F32),
        jax.ShapeDtypeStruct((n, ATTN_WIDTH), BF16),
        jax.ShapeDtypeStruct((n, D_MODEL), BF16),
        jax.ShapeDtypeStruct((n, D_MODEL), BF16),
        jax.ShapeDtypeStruct((batch, SUBLANES, CONV_WIDTH), F32),
    )
    out_specs = (
        pl.BlockSpec((None, N_HEADS, LANES, tm), lambda b, t: (b, 0, 0, t)), feat(4 * KV_WIDTH),
        row(2 * KV_WIDTH),
        pl.BlockSpec((None, 2 * KV_WIDTH, tm), lambda b, t: (b, 0, jnp.maximum(t - (nt - n_keep // tm), 0))),
        row(2 * LANES), row(KV_WIDTH), feat(KV_WIDTH), feat(KV_WIDTH),
        feat(GATE_ROWS), row(ATTN_WIDTH), row(D_MODEL), row(D_MODEL),
        pl.BlockSpec((1, SUBLANES, CONV_WIDTH), lambda b, t: (b, 0, 0)),
    )
    return pl.pallas_call(
        _proj_prompt_kernel,
        grid=(batch, nt),
        in_specs=[row(D_MODEL), per_b(D_MODEL), per_b(D_MODEL), const((1, D_MODEL)), const((IN_PAD, D_MODEL)),
                  tab, const((CONV_K, CONV_WIDTH)), const((CONV_WIDTH, D_MODEL))],
        out_specs=out_specs,
        out_shape=out_shapes,
        scratch_shapes=[pltpu.VMEM((SUBLANES, CONV_WIDTH), F32)],
        compiler_params=pltpu.CompilerParams(dimension_semantics=("arbitrary", "arbitrary"),
                                             vmem_limit_bytes=VMEM_LIMIT),
        name="proj_prompt",
    )(x2d, shift[:, None, :], scale[:, None, :], g_pre, w_pad, rope_cs, conv_w, w_br_b)


def _proj_sample_kernel(x_ref, shift_ref, scale_ref, gpre_ref, w_ref, cs_ref, convw_ref, wbrb_ref,
                        cbuf_ref, qpad_ref, kvnew_ref, gates_ref, sa_ref, ga_ref, pb_ref, u_ref):
    seg, rope = _proj_common(x_ref, shift_ref, scale_ref, gpre_ref, w_ref, cs_ref)
    for n, head in enumerate(_padded_q_heads(seg, rope, HEAD_DIM ** -0.5)):
        qpad_ref[:, n * LANES:(n + 1) * LANES] = head.astype(BF16)
    pieces = _kv_pieces(seg, rope)
    for p in range(6):
        kvnew_ref[:, p * LANES:(p + 1) * LANES] = pieces[p]
    gates_ref[...] = _sigmoid(seg(C_G, C_G + LANES))
    sa_ref[...] = _silu(seg(C_A, C_A + ATTN_WIDTH)).astype(BF16)
    ga_ref[...] = _sigmoid(seg(C_MA, C_MA + D_MODEL)).astype(BF16)
    u = seg(C_CC, C_CC + CONV_WIDTH) * seg(C_CX, C_CX + CONV_WIDTH)
    um2 = cbuf_ref[:, 0:CONV_WIDTH]
    um1 = cbuf_ref[:, CONV_WIDTH:2 * CONV_WIDTH]
    pb_ref[...] = _branch_b(seg, um2, um1, u, convw_ref, wbrb_ref).astype(BF16)
    u_ref[...] = u


def _proj_sample(x2d, shift, scale, g_pre, w_pad, rope_cs, conv_w, w_br_b, cbuf):
    n = x2d.shape[0]
    full = lambda shp: pl.BlockSpec(shp, lambda i: (0,) * len(shp))
    out_shapes = (
        jax.ShapeDtypeStruct((n, N_HEADS * LANES), BF16),
        jax.ShapeDtypeStruct((n, 6 * KV_WIDTH), F32),
        jax.ShapeDtypeStruct((n, LANES), F32),
        jax.ShapeDtypeStruct((n, ATTN_WIDTH), BF16),
        jax.ShapeDtypeStruct((n, D_MODEL), BF16),
        jax.ShapeDtypeStruct((n, D_MODEL), BF16),
        jax.ShapeDtypeStruct((n, CONV_WIDTH), F32),
    )
    return pl.pallas_call(
        _proj_sample_kernel,
        grid=(1,),
        in_specs=[full((n, D_MODEL)), full((n, D_MODEL)), full((n, D_MODEL)), full((1, D_MODEL)),
                  full((IN_PAD, D_MODEL)), full((1, ROPE_DIM)),
                  full((CONV_K, CONV_WIDTH)), full((CONV_WIDTH, D_MODEL)), full((n, 2 * CONV_WIDTH))],
        out_specs=tuple(full(s.shape) for s in out_shapes),
        out_shape=out_shapes,
        compiler_params=pltpu.CompilerParams(dimension_semantics=("arbitrary",), vmem_limit_bytes=VMEM_LIMIT),
        name="proj_sample",
    )(x2d, shift, scale, g_pre, w_pad, rope_cs, conv_w, w_br_b, cbuf)


CHUNK_LANES = CMP_STRIDE * KV_WIDTH


def _compress_weights(pe_cmp, w_cmp1, w_cmp2):
    zeros = jnp.zeros((2, CMP_STRIDE, HEAD_DIM, CMP_HIDDEN), w_cmp1.dtype)

    def both_groups(w_half):
        w = w_half.reshape(2, CMP_STRIDE, HEAD_DIM, CMP_HIDDEN)
        g0 = jnp.concatenate([w, zeros], axis=2)
        g1 = jnp.concatenate([zeros, w], axis=2)
        return jnp.concatenate([g0, g1], axis=3).reshape(2, CHUNK_LANES, KV_HEADS * CMP_HIDDEN).astype(BF16)

    half = CMP_STRIDE * HEAD_DIM
    z2 = jnp.zeros_like(w_cmp2)
    w2 = jnp.concatenate([jnp.concatenate([w_cmp2, z2], axis=2), jnp.concatenate([z2, w_cmp2], axis=2)], axis=1)
    pe = jnp.concatenate([pe_cmp, pe_cmp], axis=2)
    return dict(w1_lo=both_groups(w_cmp1[:, :half]), w1_hi=both_groups(w_cmp1[:, half:]), w2=w2.astype(BF16),
                pe_lo=pe[:, :CMP_STRIDE].reshape(2, 1, CHUNK_LANES), pe_hi=pe[:, CMP_STRIDE:].reshape(2, 1, CHUNK_LANES))


def _compress_chunks(c, kind, pelo_ref, pehi_ref, w1lo_ref, w1hi_ref):
    lo = _dot((c + pelo_ref[kind]).astype(BF16), w1lo_ref[kind])
    hi = _dot((c + pehi_ref[kind]).astype(BF16), w1hi_ref[kind])
    return lo, hi


def _compress_prompt_kernel(kc_ref, vc_ref, pelo_ref, pehi_ref, w1lo_ref, w1hi_ref, w2_ref, kc_out_ref, vct_out_ref):
    n_chunk = kc_ref.shape[0] // CMP_STRIDE
    outs = []
    for kind, src_ref in enumerate((kc_ref, vc_ref)):
        c = jnp.concatenate([src_ref[pl.ds(r, n_chunk, stride=CMP_STRIDE), :] for r in range(CMP_STRIDE)], axis=1)
        lo, hi = _compress_chunks(c, kind, pelo_ref, pehi_ref, w1lo_ref, w1hi_ref)
        hid = lo + pltpu.roll(hi, n_chunk - 1, 0)
        outs.append(_dot(_silu(hid).astype(BF16), w2_ref[kind]))
    kc_out_ref[0] = outs[0].astype(BF16)
    vct_out_ref[0] = outs[1].T.astype(BF16)


def _compress_prompt(kvc, cw, batch, seq):
    n_chunk = seq // CMP_STRIDE
    const = lambda a: pl.BlockSpec(a.shape, lambda b: (0,) * a.ndim)
    return pl.pallas_call(
        _compress_prompt_kernel,
        grid=(batch,),
        in_specs=[
            pl.BlockSpec((seq, KV_WIDTH), lambda b: (b, 0)),
            pl.BlockSpec((seq, KV_WIDTH), lambda b: (b, 1)),
            const(cw["pe_lo"]), const(cw["pe_hi"]), const(cw["w1_lo"]), const(cw["w1_hi"]), const(cw["w2"]),
        ],
        out_specs=(pl.BlockSpec((1, n_chunk, KV_WIDTH), lambda b: (b, 0, 0)),
                   pl.BlockSpec((1, KV_WIDTH, n_chunk), lambda b: (b, 0, 0))),
        out_shape=(jax.ShapeDtypeStruct((batch, n_chunk, KV_WIDTH), BF16),
                   jax.ShapeDtypeStruct((batch, KV_WIDTH, n_chunk), BF16)),
        compiler_params=pltpu.CompilerParams(dimension_semantics=("arbitrary",), vmem_limit_bytes=VMEM_LIMIT),
        name="compress_prompt",
    )(kvc, kvc, cw["pe_lo"], cw["pe_hi"], cw["w1_lo"], cw["w1_hi"], cw["w2"])


def _split_bf16(x):
    hi = x.astype(BF16)
    lo = (x - hi.astype(F32)).astype(BF16)
    return hi, lo


def _masked_softmax_parts(s, valid):
    s = jnp.where(valid, s, -jnp.inf)
    m = jnp.max(s, axis=-1, keepdims=True)
    m = jnp.where(m == -jnp.inf, 0.0, m)
    e = jnp.exp(s - m)
    den = jnp.maximum(jnp.sum(e, axis=-1, keepdims=True), 1e-30)
    return e, den


COL_CHUNK = 256


def _skewed(n, stages):
    for step in range(n + len(stages) - 1):
        for si, stage in enumerate(stages):
            if 0 <= step - si < n:
                stage(step - si)


def _skewed_thunks(n, stages):
    return [functools.partial(stage, step - si) for step in range(n + len(stages) - 1)
            for si, stage in enumerate(stages) if 0 <= step - si < n]


N_FORCED = 3


def _top_k_bias_t(imp, forced, valid, fillers=()):
    blk = lax.broadcasted_iota(jnp.int32, imp.shape, 0).astype(F32)
    bias = jnp.where(forced, 0.0, NEG_BIG)
    score = jnp.where(forced, -jnp.inf, jnp.where(valid, imp, -1.0))
    fillers = list(fillers)
    for _ in range(N_SEL - N_FORCED):
        m = jnp.max(score, axis=0, keepdims=True)
        idx = jnp.min(jnp.where(score == m, blk, float(LANES)), axis=0, keepdims=True)
        hit = blk == idx
        bias = jnp.where(hit, 0.0, bias)
        score = jnp.where(hit, -jnp.inf, score)
        for thunk in fillers[:2]:
            thunk()
        fillers = fillers[2:]
    for thunk in fillers:
        thunk()
    return bias


def _masked_softmax_parts_t(s, valid):
    s = jnp.where(valid, s, -jnp.inf)
    m = jnp.max(s, axis=0, keepdims=True)
    m = jnp.where(m == -jnp.inf, 0.0, m)
    e = jnp.exp2(s - m)
    den = jnp.maximum(jnp.sum(e, axis=0, keepdims=True), 1e-30)
    return e, den


def _attn_prompt_t_kernel(qt_ref, kaug_ref, vselt_ref, kwin_ref, vwint_ref, kc_ref, vct_ref, gatest_ref, ovlt_ref,
                          o_ref, qaugt_ref, m_ref, l_ref, acct_ref, oct_ref, owt_ref, s_ref):
    qb = pl.program_id(1)
    q0 = qb * Q_BLOCK
    n_cmp = kc_ref.shape[1]
    n_cols = N_HEADS * Q_BLOCK
    n_chunks = n_cols // COL_CHUNK
    heads_per_chunk = COL_CHUNK // Q_BLOCK

    for n in range(N_HEADS):
        qaugt_ref[0:LANES, n * Q_BLOCK:(n + 1) * Q_BLOCK] = qt_ref[n]

    def qpos_cols(cols):
        c = lax.broadcasted_iota(jnp.int32, (1, cols), 1)
        return q0 + (c & (Q_BLOCK - 1))

    def chunk_cols(cc):
        return slice(cc * COL_CHUNK, (cc + 1) * COL_CHUNK)

    kc = kc_ref[0]
    vct = vct_ref[0]
    c_end = lax.broadcasted_iota(jnp.int32, (n_cmp, 1), 0) * CMP_STRIDE + (CMP_BLOCK - 1)
    psum = [None] * KV_HEADS
    cs, cp = {}, {}

    def cmp_scores(cc):
        cs[cc] = _dot(kc, qaugt_ref[0:LANES, chunk_cols(cc)])

    def cmp_softmax(cc):
        e, den = _masked_softmax_parts_t(cs.pop(cc), c_end <= qpos_cols(COL_CHUNK))
        p = e * (1.0 / den)
        cp[cc] = p.astype(BF16)
        g = (cc * heads_per_chunk) // HEADS_PER_GROUP
        part = p[:, 0:Q_BLOCK]
        for h in range(1, heads_per_chunk):
            part = part + p[:, h * Q_BLOCK:(h + 1) * Q_BLOCK]
        psum[g] = part if psum[g] is None else psum[g] + part

    def cmp_values(cc):
        oct_ref[:, chunk_cols(cc)] = _dot(vct, cp.pop(cc))

    _skewed(n_chunks, (cmp_scores, cmp_softmax, cmp_values))

    cur = qpos_cols(KV_HEADS * Q_BLOCK) // SEL_BLOCK
    j = lax.broadcasted_iota(jnp.int32, (LANES, KV_HEADS * Q_BLOCK), 0)
    forced = (j == 0) | (j == cur) | (j == cur - 1)
    hi, lo = _split_bf16(jnp.concatenate(psum, axis=1))
    imp = _dot(ovlt_ref[...], hi) + _dot(ovlt_ref[...], lo)

    wk = kwin_ref.shape[0] if kwin_ref.shape[0] < WINDOW + Q_BLOCK else WINDOW + Q_BLOCK
    start = pl.multiple_of(jnp.maximum(q0 - WINDOW, 0), Q_BLOCK)
    kw = kwin_ref[pl.ds(start, wk), :]
    vwt = vwint_ref[:, pl.ds(start, wk)]
    kwpos = start + lax.broadcasted_iota(jnp.int32, (wk, 1), 0)
    ws, we, wden = {}, {}, {}

    def win_scores(cc):
        ws[cc] = _dot(kw, qaugt_ref[0:LANES, chunk_cols(cc)])

    def win_softmax(cc):
        dt = qpos_cols(COL_CHUNK) - kwpos
        e, wden[cc] = _masked_softmax_parts_t(ws.pop(cc), (dt >= 0) & (dt < WINDOW))
        we[cc] = e.astype(BF16)

    def win_values(cc):
        owt_ref[:, chunk_cols(cc)] = _dot(vwt, we.pop(cc)) * (1.0 / wden.pop(cc))

    bias = _top_k_bias_t(imp, forced, j <= cur,
                         _skewed_thunks(n_chunks, (win_scores, win_softmax, win_values))).astype(BF16)
    for n in range(N_HEADS):
        g = n // HEADS_PER_GROUP
        qaugt_ref[LANES:2 * LANES, n * Q_BLOCK:(n + 1) * Q_BLOCK] = bias[:, g * Q_BLOCK:(g + 1) * Q_BLOCK]

    m_ref[...] = jnp.full(m_ref.shape, NEG_BIG, F32)
    l_ref[...] = jnp.zeros(l_ref.shape, F32)
    acct_ref[...] = jnp.zeros(acct_ref.shape, F32)
    kt_last = (q0 + Q_BLOCK - 1) // KV_TILE

    def key_tile(kt):
        return pl.multiple_of(kt * KV_TILE, KV_TILE)

    def scores(kt, cc):
        s_ref[:, chunk_cols(cc)] = _dot(kaug_ref[pl.ds(key_tile(kt), KV_TILE), :], qaugt_ref[:, chunk_cols(cc)])

    def sel_tile(kt, causal, issue_next):
        k0 = key_tile(kt)
        vt = vselt_ref[:, pl.ds(k0, KV_TILE)]
        pb, alpha = {}, {}

        def softmax(cc):
            cols = chunk_cols(cc)
            s = s_ref[:, cols]
            if causal:
                kpos = k0 + lax.broadcasted_iota(jnp.int32, (KV_TILE, 1), 0)
                s = jnp.where(kpos <= qpos_cols(COL_CHUNK), s, NEG_BIG)
            if issue_next:
                scores(kt + 1, cc)
            m_old = m_ref[:, cols]
            m_new = jnp.maximum(m_old, jnp.max(s, axis=0, keepdims=True))
            alpha[cc] = jnp.exp2(m_old - m_new)
            p = jnp.exp2(s - m_new)
            l_ref[:, cols] = alpha[cc] * l_ref[:, cols] + jnp.sum(p, axis=0, keepdims=True)
            m_ref[:, cols] = m_new
            pb[cc] = p.astype(BF16)

        def values(cc):
            cols = chunk_cols(cc)
            acct_ref[:, cols] = alpha.pop(cc) * acct_ref[:, cols] + _dot(vt, pb.pop(cc))

        _skewed(n_chunks, (softmax, values))

    for cc in range(n_chunks):
        scores(0, cc)

    def body(kt, carry):
        sel_tile(kt, False, True)
        return carry

    lax.fori_loop(0, kt_last, body, 0)
    sel_tile(kt_last, True, False)

    gates = gatest_ref[...]
    for pair in range(N_HEADS // 2):
        g = (2 * pair) // HEADS_PER_GROUP
        feat = slice(g * HEAD_DIM, (g + 1) * HEAD_DIM)
        halves = []
        for n in (2 * pair, 2 * pair + 1):
            cols = slice(n * Q_BLOCK, (n + 1) * Q_BLOCK)
            o_sel = acct_ref[feat, cols] * (1.0 / l_ref[:, cols])
            halves.append(gates[3 * n:3 * n + 1, :] * oct_ref[feat, cols] + gates[3 * n + 1:3 * n + 2, :] * o_sel
                          + gates[3 * n + 2:3 * n + 3, :] * owt_ref[feat, cols])
        o_ref[:, pair * LANES:(pair + 1) * LANES] = jnp.concatenate(halves, axis=0).T


def _attn_prompt_t(qt, ksel, kwinb, vselt, vwint, kc, vct, gatest, ovlt, batch, seq):
    nq = seq // Q_BLOCK
    n_cmp = kc.shape[1]
    n_cols = N_HEADS * Q_BLOCK
    per_b = lambda a: pl.BlockSpec((None,) + a.shape[1:], lambda b, i: (b,) + (0,) * (a.ndim - 1))
    return pl.pallas_call(
        _attn_prompt_t_kernel,
        grid=(batch, nq),
        in_specs=[
            pl.BlockSpec((None, N_HEADS, LANES, Q_BLOCK), lambda b, i: (b, 0, 0, i)),
            pl.BlockSpec((seq, 2 * LANES), lambda b, i: (b, 0)),
            per_b(vselt),
            pl.BlockSpec((seq, LANES), lambda b, i: (b, 0)),
            per_b(vwint),
            pl.BlockSpec((1, n_cmp, LANES), lambda b, i: (b, 0, 0)),
            pl.BlockSpec((1, LANES, n_cmp), lambda b, i: (b, 0, 0)),
            pl.BlockSpec((None, GATE_ROWS, Q_BLOCK), lambda b, i: (b, 0, i)),
            pl.BlockSpec((LANES, n_cmp), lambda b, i: (0, 0)),
        ],
        out_specs=pl.BlockSpec((Q_BLOCK, ATTN_WIDTH), lambda b, i: (b * nq + i, 0)),
        out_shape=jax.ShapeDtypeStruct((batch * seq, ATTN_WIDTH), F32),
        scratch_shapes=[
            pltpu.VMEM((2 * LANES, n_cols), BF16),
            pltpu.VMEM((1, n_cols), F32),
            pltpu.VMEM((1, n_cols), F32),
            pltpu.VMEM((LANES, n_cols), F32),
            pltpu.VMEM((LANES, n_cols), F32),
            pltpu.VMEM((LANES, n_cols), F32),
            pltpu.VMEM((KV_TILE, n_cols), F32),
        ],
        compiler_params=pltpu.CompilerParams(dimension_semantics=("arbitrary", "arbitrary"),
                                             vmem_limit_bytes=VMEM_LIMIT),
        name="attn_prompt",
    )(qt, ksel, vselt, kwinb, vwint, kc, vct, gatest, ovlt)


def _finish_kernel(x_ref, o_ref, sa_ref, ga_ref, pb_ref, gate_ref, gpost_ref, wbra_ref, wout_ref, y_ref):
    ya = _dot((o_ref[...] * sa_ref[...]).astype(BF16), wbra_ref[...])
    mix = ga_ref[...] * ya + pb_ref[...]
    o = _dot(mix.astype(BF16), wout_ref[...])
    ms = jnp.mean(o * o, axis=-1, keepdims=True)
    on = o * lax.rsqrt(ms + RMS_EPS) * gpost_ref[...]
    y_ref[...] = x_ref[...] + gate_ref[...] * on


def _finish(x2d, o_attn, sa, ga, pb, gate, g_post, w_br_a, w_out, tm, rows_per_gate):
    n = x2d.shape[0]
    row = lambda w: pl.BlockSpec((tm, w), lambda i: (i, 0))
    const = lambda shp: pl.BlockSpec(shp, lambda i: (0,) * len(shp))
    if rows_per_gate == 1:
        gate_spec = row(D_MODEL)
    else:
        tiles_per_gate = rows_per_gate // tm
        gate = gate[:, None, :]
        gate_spec = pl.BlockSpec((None, 1, D_MODEL), lambda i: (i // tiles_per_gate, 0, 0))
    return pl.pallas_call(
        _finish_kernel,
        grid=(n // tm,),
        in_specs=[row(D_MODEL), row(ATTN_WIDTH), row(ATTN_WIDTH), row(D_MODEL), row(D_MODEL), gate_spec,
                  const((1, D_MODEL)), const((ATTN_WIDTH, D_MODEL)), const((D_MODEL, D_MODEL))],
        out_specs=row(D_MODEL),
        out_shape=jax.ShapeDtypeStruct((n, D_MODEL), F32),
        compiler_params=pltpu.CompilerParams(dimension_semantics=("arbitrary",), vmem_limit_bytes=VMEM_LIMIT),
        name="finish",
    )(x2d, o_attn, sa, ga, pb, gate, g_post, w_br_a, w_out)


def _attn_decode_kernel(pt_ref, *refs, n_req, n_pages, page_size, wbuf):
    del pt_ref
    page_refs = refs[:n_req * n_pages]
    (win_ref, q_ref, kvnew_ref, gates_ref, pelo_ref, pehi_ref, w1lo_ref, w1hi_ref, w2_ref, ovl_ref, expand_ref,
     o_ref, winout_ref, kcmp_ref, vcmp_ref) = refs[n_req * n_pages:]
    past_len = n_pages * page_size
    n_chunk = past_len // CMP_STRIDE
    reqs = range(n_req)
    heads = lambda b: slice(b * N_HEADS, (b + 1) * N_HEADS)
    per_req = lambda fn: jnp.concatenate([fn(b) for b in reqs], axis=0)
    q8f = per_req(lambda b: q_ref[b])
    q8 = q8f.astype(BF16)
    new_row = lambda which: per_req(lambda b: jnp.broadcast_to(kvnew_ref[b, which:which + 1, :], (N_HEADS, LANES)))

    def stream(ref, which):
        slab = ref[which * KV_HEADS:(which + 1) * KV_HEADS]
        return slab.reshape(KV_WIDTH, slab.shape[-1])

    for i, pr in enumerate(page_refs):
        rows = slice(i * page_size, (i + 1) * page_size)
        kcmp_ref[rows, :] = stream(pr, 0).T
        vcmp_ref[rows, :] = stream(pr, 1).T

    kvc = []
    for kind, src_ref in enumerate((kcmp_ref, vcmp_ref)):
        c = jnp.concatenate([src_ref[pl.ds(r, n_req * n_chunk, stride=CMP_STRIDE), :] for r in range(CMP_STRIDE)],
                            axis=1)
        lo, hi = _compress_chunks(c, kind, pelo_ref, pehi_ref, w1lo_ref, w1hi_ref)
        hid = lo + pltpu.roll(hi, n_req * n_chunk - 1, 0)
        kvc.append(_dot(_silu(hid).astype(BF16), w2_ref[kind]).astype(BF16))
    kc, vc = kvc
    chunks = lambda b: slice(b * n_chunk, (b + 1) * n_chunk)

    c_end = lax.broadcasted_iota(jnp.int32, (1, n_chunk), 1) * CMP_STRIDE + (CMP_BLOCK - 1)
    e, den = _masked_softmax_parts(per_req(lambda b: _dot_nt(q8[heads(b)], kc[chunks(b)])), c_end <= past_len)
    p = e * (1.0 / den)
    pb = p.astype(BF16)
    o_cmp = per_req(lambda b: _dot(pb[heads(b)], vc[chunks(b)]))

    row = lax.broadcasted_iota(jnp.int32, p.shape, 0)
    in_g0 = (row & (N_HEADS - 1)) < HEADS_PER_GROUP

    def group_sums(b):
        pr_, g0_ = p[heads(b)], in_g0[heads(b)]
        g0 = jnp.sum(jnp.where(g0_, pr_, 0.0), axis=0, keepdims=True)
        g1 = jnp.sum(jnp.where(g0_, 0.0, pr_), axis=0, keepdims=True)
        return jnp.where(g0_, g0, g1)

    hi, lo = _split_bf16(per_req(group_sums))
    imp = _dot(hi, ovl_ref[...]) + _dot(lo, ovl_ref[...])
    cur = past_len // SEL_BLOCK
    n_blk = -(-(cur + 1) // SUBLANES) * SUBLANES
    imp_t = imp.T[0:n_blk]
    j = lax.broadcasted_iota(jnp.int32, imp_t.shape, 0)
    forced = (j == 0) | (j == cur) | (j == cur - 1)
    bias_t = _top_k_bias_t(imp_t, forced, j <= cur)
    bias_t = jnp.concatenate([bias_t, jnp.full((LANES - n_blk, bias_t.shape[1]), NEG_BIG, F32)], axis=0)
    bias_keys = _dot(bias_t.T.astype(BF16), expand_ref[...])

    pages = lambda b: page_refs[b * n_pages:(b + 1) * n_pages]
    s = per_req(lambda b: jnp.concatenate([_dot(q8[heads(b)], stream(pr, 2).astype(BF16)) for pr in pages(b)],
                                          axis=1)) + bias_keys
    s_new = jnp.sum(q8f * new_row(2), axis=1, keepdims=True)
    m = jnp.maximum(jnp.max(s, axis=1, keepdims=True), s_new)
    e = jnp.exp(s - m)
    e_new = jnp.exp(s_new - m)
    den = jnp.sum(e, axis=1, keepdims=True) + e_new
    eb = e.astype(BF16)

    def sel_values(b):
        acc = None
        for i, pr in enumerate(pages(b)):
            part = _dot_nt(eb[heads(b), i * page_size:(i + 1) * page_size], stream(pr, 3).astype(BF16))
            acc = part if acc is None else acc + part
        return acc

    o_sel = (per_req(sel_values) + e_new * new_row(3)) * (1.0 / den)

    kwpos = past_len - wbuf + lax.broadcasted_iota(jnp.int32, (1, wbuf), 1)
    dt = past_len - kwpos
    valid = (dt >= 0) & (dt < WINDOW) & (kwpos >= 0)
    kw_t = [stream(win_ref.at[b], 0) for b in reqs]
    vw_t = [stream(win_ref.at[b], 1) for b in reqs]
    s = jnp.where(valid, per_req(lambda b: _dot(q8[heads(b)], kw_t[b].astype(BF16))), -jnp.inf)
    s_new = jnp.sum(q8f * new_row(4), axis=1, keepdims=True)
    m = jnp.maximum(jnp.max(s, axis=1, keepdims=True), s_new)
    e = jnp.exp(s - m)
    e_new = jnp.exp(s_new - m)
    den = jnp.sum(e, axis=1, keepdims=True) + e_new
    eb = e.astype(BF16)
    o_win = (per_req(lambda b: _dot_nt(eb[heads(b)], vw_t[b].astype(BF16))) + e_new * new_row(5)) * (1.0 / den)

    gates = per_req(lambda b: gates_ref[b])
    o = gates[:, 0:1] * o_cmp + gates[:, 1:2] * o_sel + gates[:, 2:3] * o_win
    o = jnp.where(in_g0, o, pltpu.roll(o, HEAD_DIM, 1))
    lane = lax.broadcasted_iota(jnp.int32, (KV_WIDTH, wbuf), 1)
    for b in reqs:
        o_ref[b] = o[heads(b)]
        new_cols = jnp.concatenate([kvnew_ref[b], jnp.zeros((SUBLANES - 6, LANES), F32)], axis=0).T
        for kind, old in enumerate((kw_t[b], vw_t[b])):
            shifted = jnp.where(lane == wbuf - 1, new_cols[:, 4 + kind:5 + kind], pltpu.roll(old, wbuf - 1, 1))
            winout_ref[b, kind * KV_HEADS:(kind + 1) * KV_HEADS] = shifted.reshape(KV_HEADS, HEAD_DIM, wbuf)


DECODE_REQS_PER_STEP = 2


def _attn_decode(cache_t, page_table, win_t, q8, kvnew, gates8, cw, ovl, expand):
    page_size = cache_t.shape[-1]
    batch, n_pages = page_table.shape
    wbuf = win_t.shape[-1]
    past_len = n_pages * page_size
    n_req = DECODE_REQS_PER_STEP if batch % DECODE_REQS_PER_STEP == 0 else 1

    def page_spec(r, k):
        return pl.BlockSpec((None, PAGE_ROWS, HEAD_DIM, page_size), lambda i, pt: (pt[i * n_req + r, k], 0, 0, 0))

    per_step = lambda a: pl.BlockSpec((n_req,) + a.shape[1:], lambda i, pt: (i,) + (0,) * (a.ndim - 1))
    const = lambda a: pl.BlockSpec(a.shape, lambda i, pt: (0,) * a.ndim, pipeline_mode=pl.Buffered(1))
    consts = [cw["pe_lo"], cw["pe_hi"], cw["w1_lo"], cw["w1_hi"], cw["w2"], ovl, expand]
    grid_spec = pltpu.PrefetchScalarGridSpec(
        num_scalar_prefetch=1,
        grid=(batch // n_req,),
        in_specs=[page_spec(r, k) for r in range(n_req) for k in range(n_pages)]
        + [per_step(win_t), per_step(q8), per_step(kvnew), per_step(gates8)]
        + [const(a) for a in consts],
        out_specs=(pl.BlockSpec((n_req, N_HEADS, LANES), lambda i, pt: (i, 0, 0)), per_step(win_t)),
        scratch_shapes=[pltpu.VMEM((n_req * past_len, LANES), F32), pltpu.VMEM((n_req * past_len, LANES), F32)],
    )
    o8, win_out = pl.pallas_call(
        functools.partial(_attn_decode_kernel, n_req=n_req, n_pages=n_pages, page_size=page_size, wbuf=wbuf),
        grid_spec=grid_spec,
        out_shape=(jax.ShapeDtypeStruct((batch, N_HEADS, LANES), F32),
                   jax.ShapeDtypeStruct(win_t.shape, F32)),
        compiler_params=pltpu.CompilerParams(dimension_semantics=("arbitrary",), vmem_limit_bytes=VMEM_LIMIT),
        name="attn_decode",
    )(page_table, *([cache_t] * (n_req * n_pages)), win_t, q8, kvnew, gates8, *consts)
    return o8, win_out


def _overlap_matrix(n_cmp_pad, n_cmp, n_selb):
    cs = jnp.arange(n_cmp_pad) * CMP_STRIDE
    ss = jnp.arange(LANES) * SEL_BLOCK
    ov = (cs[:, None] < ss[None, :] + SEL_BLOCK) & (cs[:, None] + CMP_BLOCK > ss[None, :])
    ov = ov & (jnp.arange(n_cmp_pad) < n_cmp)[:, None] & (jnp.arange(LANES) < n_selb)[None, :]
    return ov.astype(BF16)


def _prompt_layer(x, mod, wts):
    batch, seq, _ = x.shape
    x2d = x.reshape(batch * seq, D_MODEL)
    shift, scale, gate = mod[:, 0:D_MODEL], mod[:, D_MODEL:2 * D_MODEL], mod[:, 2 * D_MODEL:]
    tabs = _rope_cos_sin(jnp.arange(seq))
    tm = min(256, seq)
    (qt, kvt, kvc, kwin, ksel, kwinb, vselt, vwint, gatest, sa, ga, pb, utail) = _proj_prompt(
        x2d, shift, scale, wts["g_pre"], wts["w_in"], tabs, wts["conv_w"], wts["w_br_b"], batch, seq, tm)
    kc, vct = _compress_prompt(kvc, wts["cmp"], batch, seq)
    n_chunk = seq // CMP_STRIDE
    ovlt = _overlap_matrix(n_chunk, n_chunk - 1, -(-seq // SEL_BLOCK)).T
    o_attn = _attn_prompt_t(qt, ksel, kwinb, vselt, vwint, kc, vct, gatest, ovlt, batch, seq)
    y = _finish(x2d, o_attn, sa, ga, pb, gate, wts["g_post"], wts["w_br_a"], wts["w_out"], tm, seq)
    return (y.reshape(batch, seq, D_MODEL),
            jnp.transpose(kvt.reshape(batch, 4, KV_HEADS, HEAD_DIM, seq), (0, 4, 1, 2, 3)),
            jnp.transpose(kwin.reshape(batch, 2, KV_HEADS, HEAD_DIM, kwin.shape[-1]), (0, 4, 1, 2, 3)),
            utail[:, SUBLANES - (CONV_K - 1):])


def _sample_layer(x, mod, cache, page_table, win, conv_state, wts):
    batch, dec_seq, _ = x.shape
    assert dec_seq == 1
    n_pages, page_size, wbuf = page_table.shape[1], cache.shape[1], win.shape[1]
    past_len = n_pages * page_size
    assert past_len % SEL_BLOCK == 0 and wbuf == WINDOW and past_len // SEL_BLOCK < LANES
    x2d = x.reshape(batch, D_MODEL)
    shift, scale, gate = mod[:, 0:D_MODEL], mod[:, D_MODEL:2 * D_MODEL], mod[:, 2 * D_MODEL:]
    tabs = _rope_cos_sin(jnp.full((1,), past_len, jnp.int32))
    cbuf = conv_state.reshape(batch, (CONV_K - 1) * CONV_WIDTH)
    qpad, kvnew, gates, sa, ga, pb, u = _proj_sample(
        x2d, shift, scale, wts["g_pre"], wts["w_in"], tabs, wts["conv_w"], wts["w_br_b"], cbuf)
    n_gate = N_HEADS * 3
    gates8 = jnp.pad(gates[:, :n_gate].reshape(batch, N_HEADS, 3), ((0, 0), (0, 0), (0, LANES - 3)))
    n_chunk = past_len // CMP_STRIDE
    ovl = _overlap_matrix(n_chunk, n_chunk - 1, past_len // SEL_BLOCK + 1)
    expand = (jnp.arange(LANES)[:, None] == (jnp.arange(past_len) // SEL_BLOCK)[None, :]).astype(BF16)
    cache_t = jnp.transpose(cache, (0, 2, 3, 4, 1)).reshape(cache.shape[0], PAGE_ROWS, HEAD_DIM, page_size)
    win_t = jnp.transpose(win, (0, 2, 3, 4, 1)).reshape(batch, WIN_ROWS, HEAD_DIM, wbuf)
    o8, win_out = _attn_decode(cache_t, page_table, win_t, qpad.reshape(batch, N_HEADS, LANES).astype(F32),
                               kvnew.reshape(batch, 6, LANES), gates8, wts["cmp"], ovl, expand)
    win_out = jnp.transpose(win_out.reshape(batch, 2, KV_HEADS, HEAD_DIM, wbuf), (0, 4, 1, 2, 3))
    y = _finish(x2d, o8[:, :, :HEAD_DIM].reshape(batch, ATTN_WIDTH), sa, ga, pb, gate, wts["g_post"], wts["w_br_a"],
                wts["w_out"], batch, 1)
    return (y.reshape(batch, 1, D_MODEL),
            kvnew[:, :4 * KV_WIDTH].reshape(batch, 1, 4, KV_HEADS, HEAD_DIM),
            win_out,
            jnp.stack([conv_state[:, CONV_K - 2], u], axis=1))


def _prep_weights(w_ada, b_ada, g_pre, g_post, w_in, pe_cmp, w_cmp1, w_cmp2, conv_w, w_br_a, w_br_b, w_out):
    n_unpadded_gate = HEADS_PER_GROUP * KV_HEADS * 3
    w_t = w_in.T
    w_pad = jnp.concatenate(
        [w_t[:C_G + n_unpadded_gate], jnp.zeros((LANES - n_unpadded_gate, D_MODEL), w_in.dtype),
         w_t[C_G + n_unpadded_gate:]], axis=0).astype(BF16)
    half = CMP_STRIDE * HEAD_DIM
    return dict(
        w_ada=w_ada, b_ada=b_ada, g_pre=g_pre.reshape(1, -1), g_post=g_post.reshape(1, -1), w_in=w_pad,
        cmp=_compress_weights(pe_cmp, w_cmp1, w_cmp2), conv_w=conv_w,
        w_br_a=w_br_a.astype(BF16), w_br_b=w_br_b.astype(BF16), w_out=w_out.astype(BF16))


def kernel(x_prompt, x_sample, cache_kv_pages, state_win_kv, state_conv, page_table, c_prompt, c_sample, w_ada, b_ada, g_pre, g_post, w_in, pe_cmp, w_cmp1, w_cmp2, conv_w, w_br_a, w_br_b, w_out):
    depth = w_in.shape[0]
    assert depth == 1
    wts = _prep_weights(w_ada[0], b_ada[0], g_pre[0], g_post[0], w_in[0], pe_cmp[0], w_cmp1[0], w_cmp2[0],
                        conv_w[0], w_br_a[0], w_br_b[0], w_out[0])
    n_prompt = c_prompt.shape[0]
    mod = _ada(jnp.concatenate([c_prompt, c_sample], axis=0), wts["w_ada"], wts["b_ada"])
    yp, kvp, wp, cp = _prompt_layer(x_prompt, mod[:n_prompt], wts)
    ys, kvs, ws, cs = _sample_layer(x_sample, mod[n_prompt:], cache_kv_pages[0], page_table, state_win_kv[0],
                                    state_conv[0], wts)
    return (yp, ys, kvp[None], wp[None], cp[None], kvs[None], ws[None], cs[None])
```

```python
import functools

import jax
import jax.numpy as jnp
from jax import lax
from jax.experimental import pallas as pl
from jax.experimental.pallas import tpu as pltpu

F32 = jnp.float32
BF16 = jnp.bfloat16

D_MODEL = 1024
N_HEADS = 8
KV_HEADS = 2
HEADS_PER_GROUP = N_HEADS // KV_HEADS
HEAD_DIM = 64
ROPE_DIM = HEAD_DIM // 4
ROPE_THETA = 500000.0
CMP_BLOCK = 32
CMP_STRIDE = 16
CMP_HIDDEN = 256
SEL_BLOCK = 64
N_SEL = 16
WINDOW = 512
Q_BLOCK = 256
CONV_WIDTH = D_MODEL // 2
CONV_K = 3
ATTN_WIDTH = N_HEADS * HEAD_DIM
KV_WIDTH = KV_HEADS * HEAD_DIM
RMS_EPS = 1e-6

LANES = 128
SUBLANES = 8
VMEM_LIMIT = 56 * 1024 * 1024

C_Q = 0
C_KV = C_Q + ATTN_WIDTH
C_G = C_KV + 6 * KV_WIDTH
C_A = C_G + LANES
C_CB = C_A + ATTN_WIDTH
C_CC = C_CB + CONV_WIDTH
C_CX = C_CC + CONV_WIDTH
C_CG = C_CX + CONV_WIDTH
C_MA = C_CG + CONV_WIDTH
C_MB = C_MA + D_MODEL
IN_PAD = C_MB + D_MODEL

PAGE_ROWS = 4 * KV_HEADS
WIN_ROWS = 2 * KV_HEADS

GATE_ROWS = 32
LOG2_E = 1.4426950408889634

NEG_BIG = -1e30
KV_TILE = 512


def _sigmoid(x):
    return 1.0 / (1.0 + jnp.exp(-x))


def _silu(x):
    return x * _sigmoid(x)


def _dot(a, b):
    return jnp.dot(a, b, preferred_element_type=F32)


def _dot_nt(a, b):
    return lax.dot_general(a, b, (((1,), (1,)), ((), ())), preferred_element_type=F32)


def _rope_cos_sin(pos):
    half = ROPE_DIM // 2
    inv = ROPE_THETA ** (-jnp.arange(half, dtype=F32) / half)
    ang = pos.astype(F32)[:, None] * inv[None, :]
    return jnp.concatenate([jnp.cos(ang), jnp.sin(ang)], axis=1)


def _rope_lane_tables(cs):
    half = ROPE_DIM // 2
    shape = (cs.shape[0], LANES)
    d = lax.broadcasted_iota(jnp.int32, shape, 1) & (HEAD_DIM - 1)
    a, p, m = jnp.ones(shape, F32), jnp.zeros(shape, F32), jnp.zeros(shape, F32)
    for f in range(half):
        cos_f, sin_f = cs[:, f:f + 1], cs[:, half + f:half + f + 1]
        a = jnp.where(d == f, cos_f, jnp.where(d == f + half, cos_f, a))
        p = jnp.where(d == f + half, sin_f, p)
        m = jnp.where(d == f, -sin_f, m)
    return a, p, m


def _rope(x, ra, rp, rm):
    half = ROPE_DIM // 2
    return x * ra + pltpu.roll(x, half, 1) * rp + pltpu.roll(x, LANES - half, 1) * rm


def _ada_kernel(c_ref, w_ref, b_ref, o_ref):
    c = _silu(c_ref[...]).astype(BF16)
    o_ref[...] = _dot(c, w_ref[...].astype(BF16)) + b_ref[...]


def _ada(c_all, w_ada, b_ada):
    n = c_all.shape[0]
    tn = 512
    return pl.pallas_call(
        _ada_kernel,
        grid=(3 * D_MODEL // tn,),
        in_specs=[
            pl.BlockSpec((n, D_MODEL), lambda j: (0, 0)),
            pl.BlockSpec((D_MODEL, tn), lambda j: (0, j)),
            pl.BlockSpec((1, tn), lambda j: (0, j)),
        ],
        out_specs=pl.BlockSpec((n, tn), lambda j: (0, j)),
        out_shape=jax.ShapeDtypeStruct((n, 3 * D_MODEL), F32),
        compiler_params=pltpu.CompilerParams(dimension_semantics=("arbitrary",), vmem_limit_bytes=VMEM_LIMIT),
        name="ada",
    )(c_all, w_ada, b_ada.reshape(1, -1))


def _proj_common(x_ref, shift_ref, scale_ref, gpre_ref, w_ref, cs_ref):
    x = x_ref[...]
    ms = jnp.mean(x * x, axis=-1, keepdims=True)
    xn = x * lax.rsqrt(ms + RMS_EPS) * gpre_ref[...]
    h = xn * (1.0 + scale_ref[...]) + shift_ref[...]
    hb = h.astype(BF16)

    def seg(lo, hi):
        return _dot_nt(hb, w_ref[lo:hi, :])

    return seg, _rope_lane_tables(cs_ref[...])


def _padded_q_heads(seg, rope, scale):
    zq = seg(C_Q, C_Q + ATTN_WIDTH)
    lane = lax.broadcasted_iota(jnp.int32, (zq.shape[0], LANES), 1)
    lower = lane < HEAD_DIM
    heads = []
    for j in range(ATTN_WIDTH // LANES):
        c = _rope(zq[:, j * LANES:(j + 1) * LANES], *rope) * scale
        r = pltpu.roll(c, HEAD_DIM, 1)
        if (2 * j) // HEADS_PER_GROUP == 0:
            heads += [jnp.where(lower, c, 0.0), jnp.where(lower, r, 0.0)]
        else:
            heads += [jnp.where(lower, 0.0, r), jnp.where(lower, 0.0, c)]
    return heads


def _kv_pieces(seg, rope):
    zkv = seg(C_KV, C_KV + 6 * KV_WIDTH)
    pieces = []
    for p in range(6):
        c = zkv[:, p * LANES:(p + 1) * LANES]
        pieces.append(_rope(c, *rope) if p % 2 == 0 else c)
    return pieces


def _branch_b(seg, um2, um1, u, convw_ref, wbrb_ref):
    cb = seg(C_CB, C_CB + CONV_WIDTH)
    conv = convw_ref[0:1, :] * um2
    conv = conv + convw_ref[1:2, :] * um1
    conv = conv + convw_ref[2:3, :] * u
    ybin = cb * conv * _silu(seg(C_CG, C_CG + CONV_WIDTH))
    yb = _dot(ybin.astype(BF16), wbrb_ref[...])
    gb = _sigmoid(seg(C_MB, C_MB + D_MODEL))
    return gb * yb


def _proj_prompt_kernel(x_ref, shift_ref, scale_ref, gpre_ref, w_ref, cs_ref, convw_ref, wbrb_ref,
                        qt_ref, kvt_ref, kvcmp_ref, kwin_ref, ksel_ref, kwinb_ref, vselt_ref, vwint_ref, gatest_ref,
                        sa_ref, ga_ref, pb_ref, utail_ref, carry_ref):
    ti = pl.program_id(1)
    tm = x_ref.shape[0]
    seg, rope = _proj_common(x_ref, shift_ref, scale_ref, gpre_ref, w_ref, cs_ref)
    for n, head in enumerate(_padded_q_heads(seg, rope, HEAD_DIM ** -0.5 * LOG2_E)):
        qt_ref[n] = head.T.astype(BF16)

    pieces = _kv_pieces(seg, rope)
    pieces_t = [p.T for p in pieces]
    for p in range(4):
        kvt_ref[p * KV_WIDTH:(p + 1) * KV_WIDTH, :] = pieces_t[p]
    kvcmp_ref[:, 0:LANES] = pieces[0]
    kvcmp_ref[:, LANES:2 * LANES] = pieces[1]
    kwin_ref[0:KV_WIDTH, :] = pieces_t[4]
    kwin_ref[KV_WIDTH:2 * KV_WIDTH, :] = pieces_t[5]
    kwinb_ref[...] = pieces[4].astype(BF16)
    vselt_ref[...] = pieces_t[3].astype(BF16)
    vwint_ref[...] = pieces_t[5].astype(BF16)
    row = ti * tm + lax.broadcasted_iota(jnp.int32, (tm, LANES), 0)
    lane = lax.broadcasted_iota(jnp.int32, (tm, LANES), 1)
    onehot = jnp.where(lane == row // SEL_BLOCK, 1.0, 0.0)
    ksel_ref[:, 0:LANES] = pieces[2].astype(BF16)
    ksel_ref[:, LANES:2 * LANES] = onehot.astype(BF16)

    gatest_ref[...] = _sigmoid(seg(C_G, C_G + LANES)).T[0:GATE_ROWS, :]
    sa_ref[...] = _silu(seg(C_A, C_A + ATTN_WIDTH)).astype(BF16)
    ga_ref[...] = _sigmoid(seg(C_MA, C_MA + D_MODEL)).astype(BF16)

    @pl.when(ti == 0)
    def _():
        carry_ref[...] = jnp.zeros_like(carry_ref)

    u = seg(C_CC, C_CC + CONV_WIDTH) * seg(C_CX, C_CX + CONV_WIDTH)
    r = lax.broadcasted_iota(jnp.int32, u.shape, 0)
    c7 = carry_ref[SUBLANES - 1:SUBLANES, :]
    c6 = carry_ref[SUBLANES - 2:SUBLANES - 1, :]
    um1 = jnp.where(r == 0, c7, pltpu.roll(u, 1, 0))
    um2 = jnp.where(r == 0, c6, jnp.where(r == 1, c7, pltpu.roll(u, 2, 0)))
    pb_ref[...] = _branch_b(seg, um2, um1, u, convw_ref, wbrb_ref).astype(BF16)
    tail = u[tm - SUBLANES:tm, :]
    carry_ref[...] = tail
    utail_ref[0] = tail


def _proj_prompt(x2d, shift, scale, g_pre, w_pad, rope_cs, conv_w, w_br_b, batch, seq, tm):
    n = batch * seq
    nt = seq // tm
    n_keep = min(WINDOW, seq)
    assert n_keep % tm == 0
    row = lambda w: pl.BlockSpec((tm, w), lambda b, t: (b * nt + t, 0))
    per_b = lambda w: pl.BlockSpec((None, 1, w), lambda b, t: (b, 0, 0))
    const = lambda shp: pl.BlockSpec(shp, lambda b, t: (0,) * len(shp))
    tab = pl.BlockSpec((tm, ROPE_DIM), lambda b, t: (t, 0))
    feat = lambda rows: pl.BlockSpec((None, rows, tm), lambda b, t: (b, 0, t))
    out_shapes = (
        jax.ShapeDtypeStruct((batch, N_HEADS, LANES, seq), BF16),
        jax.ShapeDtypeStruct((batch, 4 * KV_WIDTH, seq), F32),
        jax.ShapeDtypeStruct((n, 2 * KV_WIDTH), F32),
        jax.ShapeDtypeStruct((batch, 2 * KV_WIDTH, n_keep), F32),
        jax.ShapeDtypeStruct((n, 2 * LANES), BF16),
        jax.ShapeDtypeStruct((n, KV_WIDTH), BF16),
        jax.ShapeDtypeStruct((batch, KV_WIDTH, seq), BF16),
        jax.ShapeDtypeStruct((batch, KV_WIDTH, seq), BF16),
        jax.ShapeDtypeStruct((batch, GATE_ROWS, seq), F32),
        jax.ShapeDtypeStruct((n, ATTN_WIDTH), BF16),
        jax.ShapeDtypeStruct((n, D_MODEL), BF16),
        jax.ShapeDtypeStruct((n, D_MODEL), BF16),
        jax.ShapeDtypeStruct((batch, SUBLANES, CONV_WIDTH), F32),
    )
    out_specs = (
        pl.BlockSpec((None, N_HEADS, LANES, tm), lambda b, t: (b, 0, 0, t)), feat(4 * KV_WIDTH),
        row(2 * KV_WIDTH),
        pl.BlockSpec((None, 2 * KV_WIDTH, tm), lambda b, t: (b, 0, jnp.maximum(t - (nt - n_keep // tm), 0))),
        row(2 * LANES), row(KV_WIDTH), feat(KV_WIDTH), feat(KV_WIDTH),
        feat(GATE_ROWS), row(ATTN_WIDTH), row(D_MODEL), row(D_MODEL),
        pl.BlockSpec((1, SUBLANES, CONV_WIDTH), lambda b, t: (b, 0, 0)),
    )
    return pl.pallas_call(
        _proj_prompt_kernel,
        grid=(batch, nt),
        in_specs=[row(D_MODEL), per_b(D_MODEL), per_b(D_MODEL), const((1, D_MODEL)), const((IN_PAD, D_MODEL)),
                  tab, const((CONV_K, CONV_WIDTH)), const((CONV_WIDTH, D_MODEL))],
        out_specs=out_specs,
        out_shape=out_shapes,
        scratch_shapes=[pltpu.VMEM((SUBLANES, CONV_WIDTH), F32)],
        compiler_params=pltpu.CompilerParams(dimension_semantics=("arbitrary", "arbitrary"),
                                             vmem_limit_bytes=VMEM_LIMIT),
        name="proj_prompt",
    )(x2d, shift[:, None, :], scale[:, None, :], g_pre, w_pad, rope_cs, conv_w, w_br_b)


def _proj_sample_kernel(x_ref, shift_ref, scale_ref, gpre_ref, w_ref, cs_ref, convw_ref, wbrb_ref,
                        cbuf_ref, qpad_ref, kvnew_ref, gates_ref, sa_ref, ga_ref, pb_ref, u_ref):
    seg, rope = _proj_common(x_ref, shift_ref, scale_ref, gpre_ref, w_ref, cs_ref)
    for n, head in enumerate(_padded_q_heads(seg, rope, HEAD_DIM ** -0.5)):
        qpad_ref[:, n * LANES:(n + 1) * LANES] = head.astype(BF16)
    pieces = _kv_pieces(seg, rope)
    for p in range(6):
        kvnew_ref[:, p * LANES:(p + 1) * LANES] = pieces[p]
    gates_ref[...] = _sigmoid(seg(C_G, C_G + LANES))
    sa_ref[...] = _silu(seg(C_A, C_A + ATTN_WIDTH)).astype(BF16)
    ga_ref[...] = _sigmoid(seg(C_MA, C_MA + D_MODEL)).astype(BF16)
    u = seg(C_CC, C_CC + CONV_WIDTH) * seg(C_CX, C_CX + CONV_WIDTH)
    um2 = cbuf_ref[:, 0:CONV_WIDTH]
    um1 = cbuf_ref[:, CONV_WIDTH:2 * CONV_WIDTH]
    pb_ref[...] = _branch_b(seg, um2, um1, u, convw_ref, wbrb_ref).astype(BF16)
    u_ref[...] = u


def _proj_sample(x2d, shift, scale, g_pre, w_pad, rope_cs, conv_w, w_br_b, cbuf):
    n = x2d.shape[0]
    full = lambda shp: pl.BlockSpec(shp, lambda i: (0,) * len(shp))
    out_shapes = (
        jax.ShapeDtypeStruct((n, N_HEADS * LANES), BF16),
        jax.ShapeDtypeStruct((n, 6 * KV_WIDTH), F32),
        jax.ShapeDtypeStruct((n, LANES), F32),
        jax.ShapeDtypeStruct((n, ATTN_WIDTH), BF16),
        jax.ShapeDtypeStruct((n, D_MODEL), BF16),
        jax.ShapeDtypeStruct((n, D_MODEL), BF16),
        jax.ShapeDtypeStruct((n, CONV_WIDTH), F32),
    )
    return pl.pallas_call(
        _proj_sample_kernel,
        grid=(1,),
        in_specs=[full((n, D_MODEL)), full((n, D_MODEL)), full((n, D_MODEL)), full((1, D_MODEL)),
                  full((IN_PAD, D_MODEL)), full((1, ROPE_DIM)),
                  full((CONV_K, CONV_WIDTH)), full((CONV_WIDTH, D_MODEL)), full((n, 2 * CONV_WIDTH))],
        out_specs=tuple(full(s.shape) for s in out_shapes),
        out_shape=out_shapes,
        compiler_params=pltpu.CompilerParams(dimension_semantics=("arbitrary",), vmem_limit_bytes=VMEM_LIMIT),
        name="proj_sample",
    )(x2d, shift, scale, g_pre, w_pad, rope_cs, conv_w, w_br_b, cbuf)


CHUNK_LANES = CMP_STRIDE * KV_WIDTH


def _compress_weights(pe_cmp, w_cmp1, w_cmp2):
    zeros = jnp.zeros((2, CMP_STRIDE, HEAD_DIM, CMP_HIDDEN), w_cmp1.dtype)

    def both_groups(w_half):
        w = w_half.reshape(2, CMP_STRIDE, HEAD_DIM, CMP_HIDDEN)
        g0 = jnp.concatenate([w, zeros], axis=2)
        g1 = jnp.concatenate([zeros, w], axis=2)
        return jnp.concatenate([g0, g1], axis=3).reshape(2, CHUNK_LANES, KV_HEADS * CMP_HIDDEN).astype(BF16)

    half = CMP_STRIDE * HEAD_DIM
    z2 = jnp.zeros_like(w_cmp2)
    w2 = jnp.concatenate([jnp.concatenate([w_cmp2, z2], axis=2), jnp.concatenate([z2, w_cmp2], axis=2)], axis=1)
    pe = jnp.concatenate([pe_cmp, pe_cmp], axis=2)
    return dict(w1_lo=both_groups(w_cmp1[:, :half]), w1_hi=both_groups(w_cmp1[:, half:]), w2=w2.astype(BF16),
                pe_lo=pe[:, :CMP_STRIDE].reshape(2, 1, CHUNK_LANES), pe_hi=pe[:, CMP_STRIDE:].reshape(2, 1, CHUNK_LANES))


def _compress_chunks(c, kind, pelo_ref, pehi_ref, w1lo_ref, w1hi_ref):
    lo = _dot((c + pelo_ref[kind]).astype(BF16), w1lo_ref[kind])
    hi = _dot((c + pehi_ref[kind]).astype(BF16), w1hi_ref[kind])
    return lo, hi


def _compress_prompt_kernel(kc_ref, vc_ref, pelo_ref, pehi_ref, w1lo_ref, w1hi_ref, w2_ref, kc_out_ref, vct_out_ref):
    n_chunk = kc_ref.shape[0] // CMP_STRIDE
    outs = []
    for kind, src_ref in enumerate((kc_ref, vc_ref)):
        c = jnp.concatenate([src_ref[pl.ds(r, n_chunk, stride=CMP_STRIDE), :] for r in range(CMP_STRIDE)], axis=1)
        lo, hi = _compress_chunks(c, kind, pelo_ref, pehi_ref, w1lo_ref, w1hi_ref)
        hid = lo + pltpu.roll(hi, n_chunk - 1, 0)
        outs.append(_dot(_silu(hid).astype(BF16), w2_ref[kind]))
    kc_out_ref[0] = outs[0].astype(BF16)
    vct_out_ref[0] = outs[1].T.astype(BF16)


def _compress_prompt(kvc, cw, batch, seq):
    n_chunk = seq // CMP_STRIDE
    const = lambda a: pl.BlockSpec(a.shape, lambda b: (0,) * a.ndim)
    return pl.pallas_call(
        _compress_prompt_kernel,
        grid=(batch,),
        in_specs=[
            pl.BlockSpec((seq, KV_WIDTH), lambda b: (b, 0)),
            pl.BlockSpec((seq, KV_WIDTH), lambda b: (b, 1)),
            const(cw["pe_lo"]), const(cw["pe_hi"]), const(cw["w1_lo"]), const(cw["w1_hi"]), const(cw["w2"]),
        ],
        out_specs=(pl.BlockSpec((1, n_chunk, KV_WIDTH), lambda b: (b, 0, 0)),
                   pl.BlockSpec((1, KV_WIDTH, n_chunk), lambda b: (b, 0, 0))),
        out_shape=(jax.ShapeDtypeStruct((batch, n_chunk, KV_WIDTH), BF16),
                   jax.ShapeDtypeStruct((batch, KV_WIDTH, n_chunk), BF16)),
        compiler_params=pltpu.CompilerParams(dimension_semantics=("arbitrary",), vmem_limit_bytes=VMEM_LIMIT),
        name="compress_prompt",
    )(kvc, kvc, cw["pe_lo"], cw["pe_hi"], cw["w1_lo"], cw["w1_hi"], cw["w2"])


def _split_bf16(x):
    hi = x.astype(BF16)
    lo = (x - hi.astype(F32)).astype(BF16)
    return hi, lo


def _masked_softmax_parts(s, valid):
    s = jnp.where(valid, s, -jnp.inf)
    m = jnp.max(s, axis=-1, keepdims=True)
    m = jnp.where(m == -jnp.inf, 0.0, m)
    e = jnp.exp(s - m)
    den = jnp.maximum(jnp.sum(e, axis=-1, keepdims=True), 1e-30)
    return e, den


COL_CHUNK = 256


def _skewed(n, stages):
    for step in range(n + len(stages) - 1):
        for si, stage in enumerate(stages):
            if 0 <= step - si < n:
                stage(step - si)


def _skewed_thunks(n, stages):
    return [functools.partial(stage, step - si) for step in range(n + len(stages) - 1)
            for si, stage in enumerate(stages) if 0 <= step - si < n]


N_FORCED = 3


def _top_k_bias_t(imp, forced, valid, fillers=()):
    blk = lax.broadcasted_iota(jnp.int32, imp.shape, 0).astype(F32)
    bias = jnp.where(forced, 0.0, NEG_BIG)
    score = jnp.where(forced, -jnp.inf, jnp.where(valid, imp, -1.0))
    fillers = list(fillers)
    for _ in range(N_SEL - N_FORCED):
        m = jnp.max(score, axis=0, keepdims=True)
        idx = jnp.min(jnp.where(score == m, blk, float(LANES)), axis=0, keepdims=True)
        hit = blk == idx
        bias = jnp.where(hit, 0.0, bias)
        score = jnp.where(hit, -jnp.inf, score)
        for thunk in fillers[:2]:
            thunk()
        fillers = fillers[2:]
    for thunk in fillers:
        thunk()
    return bias


def _masked_softmax_parts_t(s, valid):
    s = jnp.where(valid, s, -jnp.inf)
    m = jnp.max(s, axis=0, keepdims=True)
    m = jnp.where(m == -jnp.inf, 0.0, m)
    e = jnp.exp2(s - m)
    den = jnp.maximum(jnp.sum(e, axis=0, keepdims=True), 1e-30)
    return e, den


def _attn_prompt_t_kernel(qt_ref, kaug_ref, vselt_ref, kwin_ref, vwint_ref, kc_ref, vct_ref, gatest_ref, ovlt_ref,
                          o_ref, qaugt_ref, m_ref, l_ref, acct_ref, oct_ref, owt_ref, s_ref):
    qb = pl.program_id(1)
    q0 = qb * Q_BLOCK
    n_cmp = kc_ref.shape[1]
    n_cols = N_HEADS * Q_BLOCK
    n_chunks = n_cols // COL_CHUNK
    heads_per_chunk = COL_CHUNK // Q_BLOCK

    for n in range(N_HEADS):
        qaugt_ref[0:LANES, n * Q_BLOCK:(n + 1) * Q_BLOCK] = qt_ref[n]

    def qpos_cols(cols):
        c = lax.broadcasted_iota(jnp.int32, (1, cols), 1)
        return q0 + (c & (Q_BLOCK - 1))

    def chunk_cols(cc):
        return slice(cc * COL_CHUNK, (cc + 1) * COL_CHUNK)

    kc = kc_ref[0]
    vct = vct_ref[0]
    c_end = lax.broadcasted_iota(jnp.int32, (n_cmp, 1), 0) * CMP_STRIDE + (CMP_BLOCK - 1)
    psum = [None] * KV_HEADS
    cs, cp = {}, {}

    def cmp_scores(cc):
        cs[cc] = _dot(kc, qaugt_ref[0:LANES, chunk_cols(cc)])

    def cmp_softmax(cc):
        e, den = _masked_softmax_parts_t(cs.pop(cc), c_end <= qpos_cols(COL_CHUNK))
        p = e * (1.0 / den)
        cp[cc] = p.astype(BF16)
        g = (cc * heads_per_chunk) // HEADS_PER_GROUP
        part = p[:, 0:Q_BLOCK]
        for h in range(1, heads_per_chunk):
            part = part + p[:, h * Q_BLOCK:(h + 1) * Q_BLOCK]
        psum[g] = part if psum[g] is None else psum[g] + part

    def cmp_values(cc):
        oct_ref[:, chunk_cols(cc)] = _dot(vct, cp.pop(cc))

    _skewed(n_chunks, (cmp_scores, cmp_softmax, cmp_values))

    cur = qpos_cols(KV_HEADS * Q_BLOCK) // SEL_BLOCK
    j = lax.broadcasted_iota(jnp.int32, (LANES, KV_HEADS * Q_BLOCK), 0)
    forced = (j == 0) | (j == cur) | (j == cur - 1)
    hi, lo = _split_bf16(jnp.concatenate(psum, axis=1))
    imp = _dot(ovlt_ref[...], hi) + _dot(ovlt_ref[...], lo)

    wk = kwin_ref.shape[0] if kwin_ref.shape[0] < WINDOW + Q_BLOCK else WINDOW + Q_BLOCK
    start = pl.multiple_of(jnp.maximum(q0 - WINDOW, 0), Q_BLOCK)
    kw = kwin_ref[pl.ds(start, wk), :]
    vwt = vwint_ref[:, pl.ds(start, wk)]
    kwpos = start + lax.broadcasted_iota(jnp.int32, (wk, 1), 0)
    ws, we, wden = {}, {}, {}

    def win_scores(cc):
        ws[cc] = _dot(kw, qaugt_ref[0:LANES, chunk_cols(cc)])

    def win_softmax(cc):
        dt = qpos_cols(COL_CHUNK) - kwpos
        e, wden[cc] = _masked_softmax_parts_t(ws.pop(cc), (dt >= 0) & (dt < WINDOW))
        we[cc] = e.astype(BF16)

    def win_values(cc):
        owt_ref[:, chunk_cols(cc)] = _dot(vwt, we.pop(cc)) * (1.0 / wden.pop(cc))

    bias = _top_k_bias_t(imp, forced, j <= cur,
                         _skewed_thunks(n_chunks, (win_scores, win_softmax, win_values))).astype(BF16)
    for n in range(N_HEADS):
        g = n // HEADS_PER_GROUP
        qaugt_ref[LANES:2 * LANES, n * Q_BLOCK:(n + 1) * Q_BLOCK] = bias[:, g * Q_BLOCK:(g + 1) * Q_BLOCK]

    m_ref[...] = jnp.full(m_ref.shape, NEG_BIG, F32)
    l_ref[...] = jnp.zeros(l_ref.shape, F32)
    acct_ref[...] = jnp.zeros(acct_ref.shape, F32)
    kt_last = (q0 + Q_BLOCK - 1) // KV_TILE

    def key_tile(kt):
        return pl.multiple_of(kt * KV_TILE, KV_TILE)

    def scores(kt, cc):
        s_ref[:, chunk_cols(cc)] = _dot(kaug_ref[pl.ds(key_tile(kt), KV_TILE), :], qaugt_ref[:, chunk_cols(cc)])

    def sel_tile(kt, causal, issue_next):
        k0 = key_tile(kt)
        vt = vselt_ref[:, pl.ds(k0, KV_TILE)]
        pb, alpha = {}, {}

        def softmax(cc):
            cols = chunk_cols(cc)
            s = s_ref[:, cols]
            if causal:
                kpos = k0 + lax.broadcasted_iota(jnp.int32, (KV_TILE, 1), 0)
                s = jnp.where(kpos <= qpos_cols(COL_CHUNK), s, NEG_BIG)
            if issue_next:
                scores(kt + 1, cc)
            m_old = m_ref[:, cols]
            m_new = jnp.maximum(m_old, jnp.max(s, axis=0, keepdims=True))
            alpha[cc] = jnp.exp2(m_old - m_new)
            p = jnp.exp2(s - m_new)
            l_ref[:, cols] = alpha[cc] * l_ref[:, cols] + jnp.sum(p, axis=0, keepdims=True)
            m_ref[:, cols] = m_new
            pb[cc] = p.astype(BF16)

        def values(cc):
            cols = chunk_cols(cc)
            acct_ref[:, cols] = alpha.pop(cc) * acct_ref[:, cols] + _dot(vt, pb.pop(cc))

        _skewed(n_chunks, (softmax, values))

    for cc in range(n_chunks):
        scores(0, cc)

    def body(kt, carry):
        sel_tile(kt, False, True)
        return carry

    lax.fori_loop(0, kt_last, body, 0)
    sel_tile(kt_last, True, False)

    gates = gatest_ref[...]
    for pair in range(N_HEADS // 2):
        g = (2 * pair) // HEADS_PER_GROUP
        feat = slice(g * HEAD_DIM, (g + 1) * HEAD_DIM)
        halves = []
        for n in (2 * pair, 2 * pair + 1):
            cols = slice(n * Q_BLOCK, (n + 1) * Q_BLOCK)
            o_sel = acct_ref[feat, cols] * (1.0 / l_ref[:, cols])
            halves.append(gates[3 * n:3 * n + 1, :] * oct_ref[feat, cols] + gates[3 * n + 1:3 * n + 2, :] * o_sel
                          + gates[3 * n + 2:3 * n + 3, :] * owt_ref[feat, cols])
        o_ref[:, pair * LANES:(pair + 1) * LANES] = jnp.concatenate(halves, axis=0).T


def _attn_prompt_t(qt, ksel, kwinb, vselt, vwint, kc, vct, gatest, ovlt, batch, seq):
    nq = seq // Q_BLOCK
    n_cmp = kc.shape[1]
    n_cols = N_HEADS * Q_BLOCK
    per_b = lambda a: pl.BlockSpec((None,) + a.shape[1:], lambda b, i: (b,) + (0,) * (a.ndim - 1))
    return pl.pallas_call(
        _attn_prompt_t_kernel,
        grid=(batch, nq),
        in_specs=[
            pl.BlockSpec((None, N_HEADS, LANES, Q_BLOCK), lambda b, i: (b, 0, 0, i)),
            pl.BlockSpec((seq, 2 * LANES), lambda b, i: (b, 0)),
            per_b(vselt),
            pl.BlockSpec((seq, LANES), lambda b, i: (b, 0)),
            per_b(vwint),
            pl.BlockSpec((1, n_cmp, LANES), lambda b, i: (b, 0, 0)),
            pl.BlockSpec((1, LANES, n_cmp), lambda b, i: (b, 0, 0)),
            pl.BlockSpec((None, GATE_ROWS, Q_BLOCK), lambda b, i: (b, 0, i)),
            pl.BlockSpec((LANES, n_cmp), lambda b, i: (0, 0)),
        ],
        out_specs=pl.BlockSpec((Q_BLOCK, ATTN_WIDTH), lambda b, i: (b * nq + i, 0)),
        out_shape=jax.ShapeDtypeStruct((batch * seq, ATTN_WIDTH), F32),
        scratch_shapes=[
            pltpu.VMEM((2 * LANES, n_cols), BF16),
            pltpu.VMEM((1, n_cols), F32),
            pltpu.VMEM((1, n_cols), F32),
            pltpu.VMEM((LANES, n_cols), F32),
            pltpu.VMEM((LANES, n_cols), F32),
            pltpu.VMEM((LANES, n_cols), F32),
            pltpu.VMEM((KV_TILE, n_cols), F32),
        ],
        compiler_params=pltpu.CompilerParams(dimension_semantics=("arbitrary", "arbitrary"),
                                             vmem_limit_bytes=VMEM_LIMIT),
        name="attn_prompt",
    )(qt, ksel, vselt, kwinb, vwint, kc, vct, gatest, ovlt)


def _finish_kernel(x_ref, o_ref, sa_ref, ga_ref, pb_ref, gate_ref, gpost_ref, wbra_ref, wout_ref, y_ref):
    ya = _dot((o_ref[...] * sa_ref[...]).astype(BF16), wbra_ref[...])
    mix = ga_ref[...] * ya + pb_ref[...]
    o = _dot(mix.astype(BF16), wout_ref[...])
    ms = jnp.mean(o * o, axis=-1, keepdims=True)
    on = o * lax.rsqrt(ms + RMS_EPS) * gpost_ref[...]
    y_ref[...] = x_ref[...] + gate_ref[...] * on


def _finish(x2d, o_attn, sa, ga, pb, gate, g_post, w_br_a, w_out, tm, rows_per_gate):
    n = x2d.shape[0]
    row = lambda w: pl.BlockSpec((tm, w), lambda i: (i, 0))
    const = lambda shp: pl.BlockSpec(shp, lambda i: (0,) * len(shp))
    if rows_per_gate == 1:
        gate_spec = row(D_MODEL)
    else:
        tiles_per_gate = rows_per_gate // tm
        gate = gate[:, None, :]
        gate_spec = pl.BlockSpec((None, 1, D_MODEL), lambda i: (i // tiles_per_gate, 0, 0))
    return pl.pallas_call(
        _finish_kernel,
        grid=(n // tm,),
        in_specs=[row(D_MODEL), row(ATTN_WIDTH), row(ATTN_WIDTH), row(D_MODEL), row(D_MODEL), gate_spec,
                  const((1, D_MODEL)), const((ATTN_WIDTH, D_MODEL)), const((D_MODEL, D_MODEL))],
        out_specs=row(D_MODEL),
        out_shape=jax.ShapeDtypeStruct((n, D_MODEL), F32),
        compiler_params=pltpu.CompilerParams(dimension_semantics=("arbitrary",), vmem_limit_bytes=VMEM_LIMIT),
        name="finish",
    )(x2d, o_attn, sa, ga, pb, gate, g_post, w_br_a, w_out)


def _attn_decode_kernel(pt_ref, *refs, n_req, n_pages, page_size, wbuf):
    del pt_ref
    page_refs = refs[:n_req * n_pages]
    (win_ref, q_ref, kvnew_ref, gates_ref, pelo_ref, pehi_ref, w1lo_ref, w1hi_ref, w2_ref, ovl_ref, expand_ref,
     o_ref, winout_ref, kcmp_ref, vcmp_ref) = refs[n_req * n_pages:]
    past_len = n_pages * page_size
    n_chunk = past_len // CMP_STRIDE
    reqs = range(n_req)
    heads = lambda b: slice(b * N_HEADS, (b + 1) * N_HEADS)
    per_req = lambda fn: jnp.concatenate([fn(b) for b in reqs], axis=0)
    q8f = per_req(lambda b: q_ref[b])
    q8 = q8f.astype(BF16)
    new_row = lambda which: per_req(lambda b: jnp.broadcast_to(kvnew_ref[b, which:which + 1, :], (N_HEADS, LANES)))

    def stream(ref, which):
        slab = ref[which * KV_HEADS:(which + 1) * KV_HEADS]
        return slab.reshape(KV_WIDTH, slab.shape[-1])

    for i, pr in enumerate(page_refs):
        rows = slice(i * page_size, (i + 1) * page_size)
        kcmp_ref[rows, :] = stream(pr, 0).T
        vcmp_ref[rows, :] = stream(pr, 1).T

    kvc = []
    for kind, src_ref in enumerate((kcmp_ref, vcmp_ref)):
        c = jnp.concatenate([src_ref[pl.ds(r, n_req * n_chunk, stride=CMP_STRIDE), :] for r in range(CMP_STRIDE)],
                            axis=1)
        lo, hi = _compress_chunks(c, kind, pelo_ref, pehi_ref, w1lo_ref, w1hi_ref)
        hid = lo + pltpu.roll(hi, n_req * n_chunk - 1, 0)
        kvc.append(_dot(_silu(hid).astype(BF16), w2_ref[kind]).astype(BF16))
    kc, vc = kvc
    chunks = lambda b: slice(b * n_chunk, (b + 1) * n_chunk)

    c_end = lax.broadcasted_iota(jnp.int32, (1, n_chunk), 1) * CMP_STRIDE + (CMP_BLOCK - 1)
    e, den = _masked_softmax_parts(per_req(lambda b: _dot_nt(q8[heads(b)], kc[chunks(b)])), c_end <= past_len)
    p = e * (1.0 / den)
    pb = p.astype(BF16)
    o_cmp = per_req(lambda b: _dot(pb[heads(b)], vc[chunks(b)]))

    row = lax.broadcasted_iota(jnp.int32, p.shape, 0)
    in_g0 = (row & (N_HEADS - 1)) < HEADS_PER_GROUP

    def group_sums(b):
        pr_, g0_ = p[heads(b)], in_g0[heads(b)]
        g0 = jnp.sum(jnp.where(g0_, pr_, 0.0), axis=0, keepdims=True)
        g1 = jnp.sum(jnp.where(g0_, 0.0, pr_), axis=0, keepdims=True)
        return jnp.where(g0_, g0, g1)

    hi, lo = _split_bf16(per_req(group_sums))
    imp = _dot(hi, ovl_ref[...]) + _dot(lo, ovl_ref[...])
    cur = past_len // SEL_BLOCK
    n_blk = -(-(cur + 1) // SUBLANES) * SUBLANES
    imp_t = imp.T[0:n_blk]
    j = lax.broadcasted_iota(jnp.int32, imp_t.shape, 0)
    forced = (j == 0) | (j == cur) | (j == cur - 1)
    bias_t = _top_k_bias_t(imp_t, forced, j <= cur)
    bias_t = jnp.concatenate([bias_t, jnp.full((LANES - n_blk, bias_t.shape[1]), NEG_BIG, F32)], axis=0)
    bias_keys = _dot(bias_t.T.astype(BF16), expand_ref[...])

    pages = lambda b: page_refs[b * n_pages:(b + 1) * n_pages]
    s = per_req(lambda b: jnp.concatenate([_dot(q8[heads(b)], stream(pr, 2).astype(BF16)) for pr in pages(b)],
                                          axis=1)) + bias_keys
    s_new = jnp.sum(q8f * new_row(2), axis=1, keepdims=True)
    m = jnp.maximum(jnp.max(s, axis=1, keepdims=True), s_new)
    e = jnp.exp(s - m)
    e_new = jnp.exp(s_new - m)
    den = jnp.sum(e, axis=1, keepdims=True) + e_new
    eb = e.astype(BF16)

    def sel_values(b):
        acc = None
        for i, pr in enumerate(pages(b)):
            part = _dot_nt(eb[heads(b), i * page_size:(i + 1) * page_size], stream(pr, 3).astype(BF16))
            acc = part if acc is None else acc + part
        return acc

    o_sel = (per_req(sel_values) + e_new * new_row(3)) * (1.0 / den)

    kwpos = past_len - wbuf + lax.broadcasted_iota(jnp.int32, (1, wbuf), 1)
    dt = past_len - kwpos
    valid = (dt >= 0) & (dt < WINDOW) & (kwpos >= 0)
    kw_t = [stream(win_ref.at[b], 0) for b in reqs]
    vw_t = [stream(win_ref.at[b], 1) for b in reqs]
    s = jnp.where(valid, per_req(lambda b: _dot(q8[heads(b)], kw_t[b].astype(BF16))), -jnp.inf)
    s_new = jnp.sum(q8f * new_row(4), axis=1, keepdims=True)
    m = jnp.maximum(jnp.max(s, axis=1, keepdims=True), s_new)
    e = jnp.exp(s - m)
    e_new = jnp.exp(s_new - m)
    den = jnp.sum(e, axis=1, keepdims=True) + e_new
    eb = e.astype(BF16)
    o_win = (per_req(lambda b: _dot_nt(eb[heads(b)], vw_t[b].astype(BF16))) + e_new * new_row(5)) * (1.0 / den)

    gates = per_req(lambda b: gates_ref[b])
    o = gates[:, 0:1] * o_cmp + gates[:, 1:2] * o_sel + gates[:, 2:3] * o_win
    o = jnp.where(in_g0, o, pltpu.roll(o, HEAD_DIM, 1))
    lane = lax.broadcasted_iota(jnp.int32, (KV_WIDTH, wbuf), 1)
    for b in reqs:
        o_ref[b] = o[heads(b)]
        new_cols = jnp.concatenate([kvnew_ref[b], jnp.zeros((SUBLANES - 6, LANES), F32)], axis=0).T
        for kind, old in enumerate((kw_t[b], vw_t[b])):
            shifted = jnp.where(lane == wbuf - 1, new_cols[:, 4 + kind:5 + kind], pltpu.roll(old, wbuf - 1, 1))
            winout_ref[b, kind * KV_HEADS:(kind + 1) * KV_HEADS] = shifted.reshape(KV_HEADS, HEAD_DIM, wbuf)


DECODE_REQS_PER_STEP = 2


def _attn_decode(cache_t, page_table, win_t, q8, kvnew, gates8, cw, ovl, expand):
    page_size = cache_t.shape[-1]
    batch, n_pages = page_table.shape
    wbuf = win_t.shape[-1]
    past_len = n_pages * page_size
    n_req = DECODE_REQS_PER_STEP if batch % DECODE_REQS_PER_STEP == 0 else 1

    def page_spec(r, k):
        return pl.BlockSpec((None, PAGE_ROWS, HEAD_DIM, page_size), lambda i, pt: (pt[i * n_req + r, k], 0, 0, 0))

    per_step = lambda a: pl.BlockSpec((n_req,) + a.shape[1:], lambda i, pt: (i,) + (0,) * (a.ndim - 1))
    const = lambda a: pl.BlockSpec(a.shape, lambda i, pt: (0,) * a.ndim, pipeline_mode=pl.Buffered(1))
    consts = [cw["pe_lo"], cw["pe_hi"], cw["w1_lo"], cw["w1_hi"], cw["w2"], ovl, expand]
    grid_spec = pltpu.PrefetchScalarGridSpec(
        num_scalar_prefetch=1,
        grid=(batch // n_req,),
        in_specs=[page_spec(r, k) for r in range(n_req) for k in range(n_pages)]
        + [per_step(win_t), per_step(q8), per_step(kvnew), per_step(gates8)]
        + [const(a) for a in consts],
        out_specs=(pl.BlockSpec((n_req, N_HEADS, LANES), lambda i, pt: (i, 0, 0)), per_step(win_t)),
        scratch_shapes=[pltpu.VMEM((n_req * past_len, LANES), F32), pltpu.VMEM((n_req * past_len, LANES), F32)],
    )
    o8, win_out = pl.pallas_call(
        functools.partial(_attn_decode_kernel, n_req=n_req, n_pages=n_pages, page_size=page_size, wbuf=wbuf),
        grid_spec=grid_spec,
        out_shape=(jax.ShapeDtypeStruct((batch, N_HEADS, LANES), F32),
                   jax.ShapeDtypeStruct(win_t.shape, F32)),
        compiler_params=pltpu.CompilerParams(dimension_semantics=("arbitrary",), vmem_limit_bytes=VMEM_LIMIT),
        name="attn_decode",
    )(page_table, *([cache_t] * (n_req * n_pages)), win_t, q8, kvnew, gates8, *consts)
    return o8, win_out


def _overlap_matrix(n_cmp_pad, n_cmp, n_selb):
    cs = jnp.arange(n_cmp_pad) * CMP_STRIDE
    ss = jnp.arange(LANES) * SEL_BLOCK
    ov = (cs[:, None] < ss[None, :] + SEL_BLOCK) & (cs[:, None] + CMP_BLOCK > ss[None, :])
    ov = ov & (jnp.arange(n_cmp_pad) < n_cmp)[:, None] & (jnp.arange(LANES) < n_selb)[None, :]
    return ov.astype(BF16)


def _prompt_layer(x, mod, wts):
    batch, seq, _ = x.shape
    x2d = x.reshape(batch * seq, D_MODEL)
    shift, scale, gate = mod[:, 0:D_MODEL], mod[:, D_MODEL:2 * D_MODEL], mod[:, 2 * D_MODEL:]
    tabs = _rope_cos_sin(jnp.arange(seq))
    tm = min(512, seq)
    (qt, kvt, kvc, kwin, ksel, kwinb, vselt, vwint, gatest, sa, ga, pb, utail) = _proj_prompt(
        x2d, shift, scale, wts["g_pre"], wts["w_in"], tabs, wts["conv_w"], wts["w_br_b"], batch, seq, tm)
    kc, vct = _compress_prompt(kvc, wts["cmp"], batch, seq)
    n_chunk = seq // CMP_STRIDE
    ovlt = _overlap_matrix(n_chunk, n_chunk - 1, -(-seq // SEL_BLOCK)).T
    o_attn = _attn_prompt_t(qt, ksel, kwinb, vselt, vwint, kc, vct, gatest, ovlt, batch, seq)
    y = _finish(x2d, o_attn, sa, ga, pb, gate, wts["g_post"], wts["w_br_a"], wts["w_out"], tm, seq)
    return (y.reshape(batch, seq, D_MODEL),
            jnp.transpose(kvt.reshape(batch, 4, KV_HEADS, HEAD_DIM, seq), (0, 4, 1, 2, 3)),
            jnp.transpose(kwin.reshape(batch, 2, KV_HEADS, HEAD_DIM, kwin.shape[-1]), (0, 4, 1, 2, 3)),
            utail[:, SUBLANES - (CONV_K - 1):])


def _sample_layer(x, mod, cache, page_table, win, conv_state, wts):
    batch, dec_seq, _ = x.shape
    assert dec_seq == 1
    n_pages, page_size, wbuf = page_table.shape[1], cache.shape[1], win.shape[1]
    past_len = n_pages * page_size
    assert past_len % SEL_BLOCK == 0 and wbuf == WINDOW and past_len // SEL_BLOCK < LANES
    x2d = x.reshape(batch, D_MODEL)
    shift, scale, gate = mod[:, 0:D_MODEL], mod[:, D_MODEL:2 * D_MODEL], mod[:, 2 * D_MODEL:]
    tabs = _rope_cos_sin(jnp.full((1,), past_len, jnp.int32))
    cbuf = conv_state.reshape(batch, (CONV_K - 1) * CONV_WIDTH)
    qpad, kvnew, gates, sa, ga, pb, u = _proj_sample(
        x2d, shift, scale, wts["g_pre"], wts["w_in"], tabs, wts["conv_w"], wts["w_br_b"], cbuf)
    n_gate = N_HEADS * 3
    gates8 = jnp.pad(gates[:, :n_gate].reshape(batch, N_HEADS, 3), ((0, 0), (0, 0), (0, LANES - 3)))
    n_chunk = past_len // CMP_STRIDE
    ovl = _overlap_matrix(n_chunk, n_chunk - 1, past_len // SEL_BLOCK + 1)
    expand = (jnp.arange(LANES)[:, None] == (jnp.arange(past_len) // SEL_BLOCK)[None, :]).astype(BF16)
    cache_t = jnp.transpose(cache, (0, 2, 3, 4, 1)).reshape(cache.shape[0], PAGE_ROWS, HEAD_DIM, page_size)
    win_t = jnp.transpose(win, (0, 2, 3, 4, 1)).reshape(batch, WIN_ROWS, HEAD_DIM, wbuf)
    o8, win_out = _attn_decode(cache_t, page_table, win_t, qpad.reshape(batch, N_HEADS, LANES).astype(F32),
                               kvnew.reshape(batch, 6, LANES), gates8, wts["cmp"], ovl, expand)
    win_out = jnp.transpose(win_out.reshape(batch, 2, KV_HEADS, HEAD_DIM, wbuf), (0, 4, 1, 2, 3))
    y = _finish(x2d, o8[:, :, :HEAD_DIM].reshape(batch, ATTN_WIDTH), sa, ga, pb, gate, wts["g_post"], wts["w_br_a"],
                wts["w_out"], batch, 1)
    return (y.reshape(batch, 1, D_MODEL),
            kvnew[:, :4 * KV_WIDTH].reshape(batch, 1, 4, KV_HEADS, HEAD_DIM),
            win_out,
            jnp.stack([conv_state[:, CONV_K - 2], u], axis=1))


def _prep_weights(w_ada, b_ada, g_pre, g_post, w_in, pe_cmp, w_cmp1, w_cmp2, conv_w, w_br_a, w_br_b, w_out):
    n_unpadded_gate = HEADS_PER_GROUP * KV_HEADS * 3
    w_t = w_in.T
    w_pad = jnp.concatenate(
        [w_t[:C_G + n_unpadded_gate], jnp.zeros((LANES - n_unpadded_gate, D_MODEL), w_in.dtype),
         w_t[C_G + n_unpadded_gate:]], axis=0).astype(BF16)
    half = CMP_STRIDE * HEAD_DIM
    return dict(
        w_ada=w_ada, b_ada=b_ada, g_pre=g_pre.reshape(1, -1), g_post=g_post.reshape(1, -1), w_in=w_pad,
        cmp=_compress_weights(pe_cmp, w_cmp1, w_cmp2), conv_w=conv_w,
        w_br_a=w_br_a.astype(BF16), w_br_b=w_br_b.astype(BF16), w_out=w_out.astype(BF16))


def kernel(x_prompt, x_sample, cache_kv_pages, state_win_kv, state_conv, page_table, c_prompt, c_sample, w_ada, b_ada, g_pre, g_post, w_in, pe_cmp, w_cmp1, w_cmp2, conv_w, w_br_a, w_br_b, w_out):
    depth = w_in.shape[0]
    assert depth == 1
    wts = _prep_weights(w_ada[0], b_ada[0], g_pre[0], g_post[0], w_in[0], pe_cmp[0], w_cmp1[0], w_cmp2[0],
                        conv_w[0], w_br_a[0], w_br_b[0], w_out[0])
    n_prompt = c_prompt.shape[0]
    mod = _ada(jnp.concatenate([c_prompt, c_sample], axis=0), wts["w_ada"], wts["b_ada"])
    yp, kvp, wp, cp = _prompt_layer(x_prompt, mod[:n_prompt], wts)
    ys, kvs, ws, cs = _sample_layer(x_sample, mod[n_prompt:], cache_kv_pages[0], page_table, state_win_kv[0],
                                    state_conv[0], wts)
    return (yp, ys, kvp[None], wp[None], cp[None], kvs[None], ws[None], cs[None])
```

```python
import functools

import jax
import jax.numpy as jnp
from jax import lax
from jax.experimental import pallas as pl
from jax.experimental.pallas import tpu as pltpu

F32 = jnp.float32
BF16 = jnp.bfloat16

D_MODEL = 1024
N_HEADS = 8
KV_HEADS = 2
HEADS_PER_GROUP = N_HEADS // KV_HEADS
HEAD_DIM = 64
ROPE_DIM = HEAD_DIM // 4
ROPE_THETA = 500000.0
CMP_BLOCK = 32
CMP_STRIDE = 16
CMP_HIDDEN = 256
SEL_BLOCK = 64
N_SEL = 16
WINDOW = 512
Q_BLOCK = 256
CONV_WIDTH = D_MODEL // 2
CONV_K = 3
ATTN_WIDTH = N_HEADS * HEAD_DIM
KV_WIDTH = KV_HEADS * HEAD_DIM
RMS_EPS = 1e-6

LANES = 128
SUBLANES = 8
VMEM_LIMIT = 56 * 1024 * 1024

C_Q = 0
C_KV = C_Q + ATTN_WIDTH
C_G = C_KV + 6 * KV_WIDTH
C_A = C_G + LANES
C_CB = C_A + ATTN_WIDTH
C_CC = C_CB + CONV_WIDTH
C_CX = C_CC + CONV_WIDTH
C_CG = C_CX + CONV_WIDTH
C_MA = C_CG + CONV_WIDTH
C_MB = C_MA + D_MODEL
IN_PAD = C_MB + D_MODEL

PAGE_ROWS = 4 * KV_HEADS
WIN_ROWS = 2 * KV_HEADS

GATE_ROWS = 32
LOG2_E = 1.4426950408889634

NEG_BIG = -1e30
KV_TILE = 512


def _sigmoid(x):
    return 1.0 / (1.0 + jnp.exp(-x))


def _silu(x):
    return x * _sigmoid(x)


def _dot(a, b):
    return jnp.dot(a, b, preferred_element_type=F32)


def _dot_nt(a, b):
    return lax.dot_general(a, b, (((1,), (1,)), ((), ())), preferred_element_type=F32)


def _rope_cos_sin(pos):
    half = ROPE_DIM // 2
    inv = ROPE_THETA ** (-jnp.arange(half, dtype=F32) / half)
    ang = pos.astype(F32)[:, None] * inv[None, :]
    return jnp.concatenate([jnp.cos(ang), jnp.sin(ang)], axis=1)


def _rope_lane_tables(cs):
    half = ROPE_DIM // 2
    shape = (cs.shape[0], LANES)
    d = lax.broadcasted_iota(jnp.int32, shape, 1) & (HEAD_DIM - 1)
    a, p, m = jnp.ones(shape, F32), jnp.zeros(shape, F32), jnp.zeros(shape, F32)
    for f in range(half):
        cos_f, sin_f = cs[:, f:f + 1], cs[:, half + f:half + f + 1]
        a = jnp.where(d == f, cos_f, jnp.where(d == f + half, cos_f, a))
        p = jnp.where(d == f + half, sin_f, p)
        m = jnp.where(d == f, -sin_f, m)
    return a, p, m


def _rope(x, ra, rp, rm):
    half = ROPE_DIM // 2
    return x * ra + pltpu.roll(x, half, 1) * rp + pltpu.roll(x, LANES - half, 1) * rm


def _ada_kernel(c_ref, w_ref, b_ref, o_ref):
    c = _silu(c_ref[...]).astype(BF16)
    o_ref[...] = _dot(c, w_ref[...].astype(BF16)) + b_ref[...]


def _ada(c_all, w_ada, b_ada):
    n = c_all.shape[0]
    tn = 512
    return pl.pallas_call(
        _ada_kernel,
        grid=(3 * D_MODEL // tn,),
        in_specs=[
            pl.BlockSpec((n, D_MODEL), lambda j: (0, 0)),
            pl.BlockSpec((D_MODEL, tn), lambda j: (0, j)),
            pl.BlockSpec((1, tn), lambda j: (0, j)),
        ],
        out_specs=pl.BlockSpec((n, tn), lambda j: (0, j)),
        out_shape=jax.ShapeDtypeStruct((n, 3 * D_MODEL), F32),
        compiler_params=pltpu.CompilerParams(dimension_semantics=("arbitrary",), vmem_limit_bytes=VMEM_LIMIT),
        name="ada",
    )(c_all, w_ada, b_ada.reshape(1, -1))


def _proj_common(x_ref, shift_ref, scale_ref, gpre_ref, w_ref, cs_ref):
    x = x_ref[...]
    ms = jnp.mean(x * x, axis=-1, keepdims=True)
    xn = x * lax.rsqrt(ms + RMS_EPS) * gpre_ref[...]
    h = xn * (1.0 + scale_ref[...]) + shift_ref[...]
    hb = h.astype(BF16)

    def seg(lo, hi):
        return _dot_nt(hb, w_ref[lo:hi, :])

    return seg, _rope_lane_tables(cs_ref[...])


def _padded_q_heads(seg, rope, scale):
    zq = seg(C_Q, C_Q + ATTN_WIDTH)
    lane = lax.broadcasted_iota(jnp.int32, (zq.shape[0], LANES), 1)
    lower = lane < HEAD_DIM
    heads = []
    for j in range(ATTN_WIDTH // LANES):
        c = _rope(zq[:, j * LANES:(j + 1) * LANES], *rope) * scale
        r = pltpu.roll(c, HEAD_DIM, 1)
        if (2 * j) // HEADS_PER_GROUP == 0:
            heads += [jnp.where(lower, c, 0.0), jnp.where(lower, r, 0.0)]
        else:
            heads += [jnp.where(lower, 0.0, r), jnp.where(lower, 0.0, c)]
    return heads


def _kv_pieces(seg, rope):
    zkv = seg(C_KV, C_KV + 6 * KV_WIDTH)
    pieces = []
    for p in range(6):
        c = zkv[:, p * LANES:(p + 1) * LANES]
        pieces.append(_rope(c, *rope) if p % 2 == 0 else c)
    return pieces


def _branch_b(seg, um2, um1, u, convw_ref, wbrb_ref):
    cb = seg(C_CB, C_CB + CONV_WIDTH)
    conv = convw_ref[0:1, :] * um2
    conv = conv + convw_ref[1:2, :] * um1
    conv = conv + convw_ref[2:3, :] * u
    ybin = cb * conv * _silu(seg(C_CG, C_CG + CONV_WIDTH))
    yb = _dot(ybin.astype(BF16), wbrb_ref[...])
    gb = _sigmoid(seg(C_MB, C_MB + D_MODEL))
    return gb * yb


def _proj_prompt_kernel(x_ref, shift_ref, scale_ref, gpre_ref, w_ref, cs_ref, convw_ref, wbrb_ref,
                        qt_ref, kvt_ref, kvcmp_ref, kwin_ref, ksel_ref, kwinb_ref, vselt_ref, vwint_ref, gatest_ref,
                        sa_ref, ga_ref, pb_ref, utail_ref, carry_ref):
    ti = pl.program_id(1)
    tm = x_ref.shape[0]
    seg, rope = _proj_common(x_ref, shift_ref, scale_ref, gpre_ref, w_ref, cs_ref)
    for n, head in enumerate(_padded_q_heads(seg, rope, HEAD_DIM ** -0.5 * LOG2_E)):
        qt_ref[n] = head.T.astype(BF16)

    pieces = _kv_pieces(seg, rope)
    pieces_t = [p.T for p in pieces]
    for p in range(4):
        kvt_ref[p * KV_WIDTH:(p + 1) * KV_WIDTH, :] = pieces_t[p]
    kvcmp_ref[:, 0:LANES] = pieces[0]
    kvcmp_ref[:, LANES:2 * LANES] = pieces[1]
    kwin_ref[0:KV_WIDTH, :] = pieces_t[4]
    kwin_ref[KV_WIDTH:2 * KV_WIDTH, :] = pieces_t[5]
    kwinb_ref[...] = pieces[4].astype(BF16)
    vselt_ref[...] = pieces_t[3].astype(BF16)
    vwint_ref[...] = pieces_t[5].astype(BF16)
    row = ti * tm + lax.broadcasted_iota(jnp.int32, (tm, LANES), 0)
    lane = lax.broadcasted_iota(jnp.int32, (tm, LANES), 1)
    onehot = jnp.where(lane == row // SEL_BLOCK, 1.0, 0.0)
    ksel_ref[:, 0:LANES] = pieces[2].astype(BF16)
    ksel_ref[:, LANES:2 * LANES] = onehot.astype(BF16)

    gatest_ref[...] = _sigmoid(seg(C_G, C_G + LANES)).T[0:GATE_ROWS, :]
    sa_ref[...] = _silu(seg(C_A, C_A + ATTN_WIDTH)).astype(BF16)
    ga_ref[...] = _sigmoid(seg(C_MA, C_MA + D_MODEL)).astype(BF16)

    @pl.when(ti == 0)
    def _():
        carry_ref[...] = jnp.zeros_like(carry_ref)

    u = seg(C_CC, C_CC + CONV_WIDTH) * seg(C_CX, C_CX + CONV_WIDTH)
    r = lax.broadcasted_iota(jnp.int32, u.shape, 0)
    c7 = carry_ref[SUBLANES - 1:SUBLANES, :]
    c6 = carry_ref[SUBLANES - 2:SUBLANES - 1, :]
    um1 = jnp.where(r == 0, c7, pltpu.roll(u, 1, 0))
    um2 = jnp.where(r == 0, c6, jnp.where(r == 1, c7, pltpu.roll(u, 2, 0)))
    pb_ref[...] = _branch_b(seg, um2, um1, u, convw_ref, wbrb_ref).astype(BF16)
    tail = u[tm - SUBLANES:tm, :]
    carry_ref[...] = tail
    utail_ref[0] = tail


def _proj_prompt(x2d, shift, scale, g_pre, w_pad, rope_cs, conv_w, w_br_b, batch, seq, tm):
    n = batch * seq
    nt = seq // tm
    n_keep = min(WINDOW, seq)
    assert n_keep % tm == 0
    row = lambda w: pl.BlockSpec((tm, w), lambda b, t: (b * nt + t, 0))
    per_b = lambda w: pl.BlockSpec((None, 1, w), lambda b, t: (b, 0, 0))
    const = lambda shp: pl.BlockSpec(shp, lambda b, t: (0,) * len(shp))
    tab = pl.BlockSpec((tm, ROPE_DIM), lambda b, t: (t, 0))
    feat = lambda rows: pl.BlockSpec((None, rows, tm), lambda b, t: (b, 0, t))
    out_shapes = (
        jax.ShapeDtypeStruct((batch, N_HEADS, LANES, seq), BF16),
        jax.ShapeDtypeStruct((batch, 4 * KV_WIDTH, seq), F32),
        jax.ShapeDtypeStruct((n, 2 * KV_WIDTH), F32),
        jax.ShapeDtypeStruct((batch, 2 * KV_WIDTH, n_keep), F32),
        jax.ShapeDtypeStruct((n, 2 * LANES), BF16),
        jax.ShapeDtypeStruct((n, KV_WIDTH), BF16),
        jax.ShapeDtypeStruct((batch, KV_WIDTH, seq), BF16),
        jax.ShapeDtypeStruct((batch, KV_WIDTH, seq), BF16),
        jax.ShapeDtypeStruct((batch, GATE_ROWS, seq), F32),
        jax.ShapeDtypeStruct((n, ATTN_WIDTH), BF16),
        jax.ShapeDtypeStruct((n, D_MODEL), BF16),
        jax.ShapeDtypeStruct((n, D_MODEL), BF16),
        jax.ShapeDtypeStruct((batch, SUBLANES, CONV_WIDTH), F32),
    )
    out_specs = (
        pl.BlockSpec((None, N_HEADS, LANES, tm), lambda b, t: (b, 0, 0, t)), feat(4 * KV_WIDTH),
        row(2 * KV_WIDTH),
        pl.BlockSpec((None, 2 * KV_WIDTH, tm), lambda b, t: (b, 0, jnp.maximum(t - (nt - n_keep // tm), 0))),
        row(2 * LANES), row(KV_WIDTH), feat(KV_WIDTH), feat(KV_WIDTH),
        feat(GATE_ROWS), row(ATTN_WIDTH), row(D_MODEL), row(D_MODEL),
        pl.BlockSpec((1, SUBLANES, CONV_WIDTH), lambda b, t: (b, 0, 0)),
    )
    return pl.pallas_call(
        _proj_prompt_kernel,
        grid=(batch, nt),
        in_specs=[row(D_MODEL), per_b(D_MODEL), per_b(D_MODEL), const((1, D_MODEL)), const((IN_PAD, D_MODEL)),
                  tab, const((CONV_K, CONV_WIDTH)), const((CONV_WIDTH, D_MODEL))],
        out_specs=out_specs,
        out_shape=out_shapes,
        scratch_shapes=[pltpu.VMEM((SUBLANES, CONV_WIDTH), F32)],
        compiler_params=pltpu.CompilerParams(dimension_semantics=("arbitrary", "arbitrary"),
                                             vmem_limit_bytes=VMEM_LIMIT),
        name="proj_prompt",
    )(x2d, shift[:, None, :], scale[:, None, :], g_pre, w_pad, rope_cs, conv_w, w_br_b)


def _proj_sample_kernel(x_ref, shift_ref, scale_ref, gpre_ref, w_ref, cs_ref, convw_ref, wbrb_ref,
                        cbuf_ref, qpad_ref, kvnew_ref, gates_ref, sa_ref, ga_ref, pb_ref, u_ref):
    seg, rope = _proj_common(x_ref, shift_ref, scale_ref, gpre_ref, w_ref, cs_ref)
    for n, head in enumerate(_padded_q_heads(seg, rope, HEAD_DIM ** -0.5)):
        qpad_ref[:, n * LANES:(n + 1) * LANES] = head.astype(BF16)
    pieces = _kv_pieces(seg, rope)
    for p in range(6):
        kvnew_ref[:, p * LANES:(p + 1) * LANES] = pieces[p]
    gates_ref[...] = _sigmoid(seg(C_G, C_G + LANES))
    sa_ref[...] = _silu(seg(C_A, C_A + ATTN_WIDTH)).astype(BF16)
    ga_ref[...] = _sigmoid(seg(C_MA, C_MA + D_MODEL)).astype(BF16)
    u = seg(C_CC, C_CC + CONV_WIDTH) * seg(C_CX, C_CX + CONV_WIDTH)
    um2 = cbuf_ref[:, 0:CONV_WIDTH]
    um1 = cbuf_ref[:, CONV_WIDTH:2 * CONV_WIDTH]
    pb_ref[...] = _branch_b(seg, um2, um1, u, convw_ref, wbrb_ref).astype(BF16)
    u_ref[...] = u


def _proj_sample(x2d, shift, scale, g_pre, w_pad, rope_cs, conv_w, w_br_b, cbuf):
    n = x2d.shape[0]
    full = lambda shp: pl.BlockSpec(shp, lambda i: (0,) * len(shp))
    out_shapes = (
        jax.ShapeDtypeStruct((n, N_HEADS * LANES), BF16),
        jax.ShapeDtypeStruct((n, 6 * KV_WIDTH), F32),
        jax.ShapeDtypeStruct((n, LANES), F32),
        jax.ShapeDtypeStruct((n, ATTN_WIDTH), BF16),
        jax.ShapeDtypeStruct((n, D_MODEL), BF16),
        jax.ShapeDtypeStruct((n, D_MODEL), BF16),
        jax.ShapeDtypeStruct((n, CONV_WIDTH), F32),
    )
    return pl.pallas_call(
        _proj_sample_kernel,
        grid=(1,),
        in_specs=[full((n, D_MODEL)), full((n, D_MODEL)), full((n, D_MODEL)), full((1, D_MODEL)),
                  full((IN_PAD, D_MODEL)), full((1, ROPE_DIM)),
                  full((CONV_K, CONV_WIDTH)), full((CONV_WIDTH, D_MODEL)), full((n, 2 * CONV_WIDTH))],
        out_specs=tuple(full(s.shape) for s in out_shapes),
        out_shape=out_shapes,
        compiler_params=pltpu.CompilerParams(dimension_semantics=("arbitrary",), vmem_limit_bytes=VMEM_LIMIT),
        name="proj_sample",
    )(x2d, shift, scale, g_pre, w_pad, rope_cs, conv_w, w_br_b, cbuf)


CHUNK_LANES = CMP_STRIDE * KV_WIDTH


def _compress_weights(pe_cmp, w_cmp1, w_cmp2):
    half = CHUNK_LANES // KV_HEADS
    z2 = jnp.zeros_like(w_cmp2)
    w2 = jnp.concatenate([jnp.concatenate([w_cmp2, z2], axis=2), jnp.concatenate([z2, w_cmp2], axis=2)], axis=1)
    return dict(w1_lo=w_cmp1[:, :half].astype(BF16), w1_hi=w_cmp1[:, half:].astype(BF16), w2=w2.astype(BF16),
                pe_lo=pe_cmp[:, :CMP_STRIDE].reshape(2, 1, half), pe_hi=pe_cmp[:, CMP_STRIDE:].reshape(2, 1, half))


def _compress_hidden(src_ref, n_chunk, kind, pelo_ref, pehi_ref, w1lo_ref, w1hi_ref):
    rows = [src_ref[pl.ds(r, n_chunk, stride=CMP_STRIDE), :] for r in range(CMP_STRIDE)]
    lower = lax.broadcasted_iota(jnp.int32, rows[0].shape, 1) < HEAD_DIM
    hidden = []
    for g in range(KV_HEADS):
        pairs = []
        for r in range(0, CMP_STRIDE, 2):
            a, b = rows[r], rows[r + 1]
            pairs.append(jnp.where(lower, a, pltpu.roll(b, HEAD_DIM, 1)) if g == 0
                         else jnp.where(lower, pltpu.roll(a, HEAD_DIM, 1), b))
        c = jnp.concatenate(pairs, axis=1)
        lo = _dot((c + pelo_ref[kind]).astype(BF16), w1lo_ref[kind])
        hi = _dot((c + pehi_ref[kind]).astype(BF16), w1hi_ref[kind])
        hidden.append(lo + pltpu.roll(hi, n_chunk - 1, 0))
    return jnp.concatenate(hidden, axis=1)


def _compress_prompt_kernel(kc_ref, vc_ref, pelo_ref, pehi_ref, w1lo_ref, w1hi_ref, w2_ref, kc_out_ref, vct_out_ref):
    n_chunk = kc_ref.shape[0] // CMP_STRIDE
    outs = []
    for kind, src_ref in enumerate((kc_ref, vc_ref)):
        hid = _compress_hidden(src_ref, n_chunk, kind, pelo_ref, pehi_ref, w1lo_ref, w1hi_ref)
        outs.append(_dot(_silu(hid).astype(BF16), w2_ref[kind]))
    kc_out_ref[0] = outs[0].astype(BF16)
    vct_out_ref[0] = outs[1].T.astype(BF16)


def _compress_prompt(kvc, cw, batch, seq):
    n_chunk = seq // CMP_STRIDE
    const = lambda a: pl.BlockSpec(a.shape, lambda b: (0,) * a.ndim)
    return pl.pallas_call(
        _compress_prompt_kernel,
        grid=(batch,),
        in_specs=[
            pl.BlockSpec((seq, KV_WIDTH), lambda b: (b, 0)),
            pl.BlockSpec((seq, KV_WIDTH), lambda b: (b, 1)),
            const(cw["pe_lo"]), const(cw["pe_hi"]), const(cw["w1_lo"]), const(cw["w1_hi"]), const(cw["w2"]),
        ],
        out_specs=(pl.BlockSpec((1, n_chunk, KV_WIDTH), lambda b: (b, 0, 0)),
                   pl.BlockSpec((1, KV_WIDTH, n_chunk), lambda b: (b, 0, 0))),
        out_shape=(jax.ShapeDtypeStruct((batch, n_chunk, KV_WIDTH), BF16),
                   jax.ShapeDtypeStruct((batch, KV_WIDTH, n_chunk), BF16)),
        compiler_params=pltpu.CompilerParams(dimension_semantics=("arbitrary",), vmem_limit_bytes=VMEM_LIMIT),
        name="compress_prompt",
    )(kvc, kvc, cw["pe_lo"], cw["pe_hi"], cw["w1_lo"], cw["w1_hi"], cw["w2"])


def _split_bf16(x):
    hi = x.astype(BF16)
    lo = (x - hi.astype(F32)).astype(BF16)
    return hi, lo


def _masked_softmax_parts(s, valid):
    s = jnp.where(valid, s, -jnp.inf)
    m = jnp.max(s, axis=-1, keepdims=True)
    m = jnp.where(m == -jnp.inf, 0.0, m)
    e = jnp.exp(s - m)
    den = jnp.maximum(jnp.sum(e, axis=-1, keepdims=True), 1e-30)
    return e, den


COL_CHUNK = 256


def _skewed(n, stages):
    for step in range(n + len(stages) - 1):
        for si, stage in enumerate(stages):
            if 0 <= step - si < n:
                stage(step - si)


def _skewed_thunks(n, stages):
    return [functools.partial(stage, step - si) for step in range(n + len(stages) - 1)
            for si, stage in enumerate(stages) if 0 <= step - si < n]


N_FORCED = 3


def _top_k_bias_t(imp, forced, valid, fillers=()):
    blk = lax.broadcasted_iota(jnp.int32, imp.shape, 0).astype(F32)
    bias = jnp.where(forced, 0.0, NEG_BIG)
    score = jnp.where(forced, -jnp.inf, jnp.where(valid, imp, -1.0))
    fillers = list(fillers)
    for _ in range(N_SEL - N_FORCED):
        m = jnp.max(score, axis=0, keepdims=True)
        idx = jnp.min(jnp.where(score == m, blk, float(LANES)), axis=0, keepdims=True)
        hit = blk == idx
        bias = jnp.where(hit, 0.0, bias)
        score = jnp.where(hit, -jnp.inf, score)
        for thunk in fillers[:2]:
            thunk()
        fillers = fillers[2:]
    for thunk in fillers:
        thunk()
    return bias


def _masked_softmax_parts_t(s, valid):
    s = jnp.where(valid, s, -jnp.inf)
    m = jnp.max(s, axis=0, keepdims=True)
    m = jnp.where(m == -jnp.inf, 0.0, m)
    e = jnp.exp2(s - m)
    den = jnp.maximum(jnp.sum(e, axis=0, keepdims=True), 1e-30)
    return e, den


def _attn_prompt_t_kernel(qt_ref, kaug_ref, vselt_ref, kwin_ref, vwint_ref, kc_ref, vct_ref, gatest_ref, ovlt_ref,
                          o_ref, qaugt_ref, m_ref, l_ref, acct_ref, oct_ref, owt_ref, s_ref):
    qb = pl.program_id(1)
    q0 = qb * Q_BLOCK
    n_cmp = kc_ref.shape[1]
    n_cols = N_HEADS * Q_BLOCK
    n_chunks = n_cols // COL_CHUNK
    heads_per_chunk = COL_CHUNK // Q_BLOCK

    for n in range(N_HEADS):
        qaugt_ref[0:LANES, n * Q_BLOCK:(n + 1) * Q_BLOCK] = qt_ref[n]

    def qpos_cols(cols):
        c = lax.broadcasted_iota(jnp.int32, (1, cols), 1)
        return q0 + (c & (Q_BLOCK - 1))

    def chunk_cols(cc):
        return slice(cc * COL_CHUNK, (cc + 1) * COL_CHUNK)

    kc = kc_ref[0]
    vct = vct_ref[0]
    c_end = lax.broadcasted_iota(jnp.int32, (n_cmp, 1), 0) * CMP_STRIDE + (CMP_BLOCK - 1)
    psum = [None] * KV_HEADS
    cs, cp = {}, {}

    def cmp_scores(cc):
        cs[cc] = _dot(kc, qaugt_ref[0:LANES, chunk_cols(cc)])

    def cmp_softmax(cc):
        e, den = _masked_softmax_parts_t(cs.pop(cc), c_end <= qpos_cols(COL_CHUNK))
        p = e * (1.0 / den)
        cp[cc] = p.astype(BF16)
        g = (cc * heads_per_chunk) // HEADS_PER_GROUP
        part = p[:, 0:Q_BLOCK]
        for h in range(1, heads_per_chunk):
            part = part + p[:, h * Q_BLOCK:(h + 1) * Q_BLOCK]
        psum[g] = part if psum[g] is None else psum[g] + part

    def cmp_values(cc):
        oct_ref[:, chunk_cols(cc)] = _dot(vct, cp.pop(cc))

    _skewed(n_chunks, (cmp_scores, cmp_softmax, cmp_values))

    cur = qpos_cols(KV_HEADS * Q_BLOCK) // SEL_BLOCK
    j = lax.broadcasted_iota(jnp.int32, (LANES, KV_HEADS * Q_BLOCK), 0)
    forced = (j == 0) | (j == cur) | (j == cur - 1)
    hi, lo = _split_bf16(jnp.concatenate(psum, axis=1))
    imp = _dot(ovlt_ref[...], hi) + _dot(ovlt_ref[...], lo)

    wk = kwin_ref.shape[0] if kwin_ref.shape[0] < WINDOW + Q_BLOCK else WINDOW + Q_BLOCK
    start = pl.multiple_of(jnp.maximum(q0 - WINDOW, 0), Q_BLOCK)
    kw = kwin_ref[pl.ds(start, wk), :]
    vwt = vwint_ref[:, pl.ds(start, wk)]
    kwpos = start + lax.broadcasted_iota(jnp.int32, (wk, 1), 0)
    ws, we, wden = {}, {}, {}

    def win_scores(cc):
        ws[cc] = _dot(kw, qaugt_ref[0:LANES, chunk_cols(cc)])

    def win_softmax(cc):
        dt = qpos_cols(COL_CHUNK) - kwpos
        e, wden[cc] = _masked_softmax_parts_t(ws.pop(cc), (dt >= 0) & (dt < WINDOW))
        we[cc] = e.astype(BF16)

    def win_values(cc):
        owt_ref[:, chunk_cols(cc)] = _dot(vwt, we.pop(cc)) * (1.0 / wden.pop(cc))

    bias = _top_k_bias_t(imp, forced, j <= cur,
                         _skewed_thunks(n_chunks, (win_scores, win_softmax, win_values))).astype(BF16)
    for n in range(N_HEADS):
        g = n // HEADS_PER_GROUP
        qaugt_ref[LANES:2 * LANES, n * Q_BLOCK:(n + 1) * Q_BLOCK] = bias[:, g * Q_BLOCK:(g + 1) * Q_BLOCK]

    m_ref[...] = jnp.full(m_ref.shape, NEG_BIG, F32)
    l_ref[...] = jnp.zeros(l_ref.shape, F32)
    acct_ref[...] = jnp.zeros(acct_ref.shape, F32)
    kt_last = (q0 + Q_BLOCK - 1) // KV_TILE

    def key_tile(kt):
        return pl.multiple_of(kt * KV_TILE, KV_TILE)

    def scores(kt, cc):
        s_ref[:, chunk_cols(cc)] = _dot(kaug_ref[pl.ds(key_tile(kt), KV_TILE), :], qaugt_ref[:, chunk_cols(cc)])

    def sel_tile(kt, causal, issue_next):
        k0 = key_tile(kt)
        vt = vselt_ref[:, pl.ds(k0, KV_TILE)]
        pb, alpha = {}, {}

        def softmax(cc):
            cols = chunk_cols(cc)
            s = s_ref[:, cols]
            if causal:
                kpos = k0 + lax.broadcasted_iota(jnp.int32, (KV_TILE, 1), 0)
                s = jnp.where(kpos <= qpos_cols(COL_CHUNK), s, NEG_BIG)
            if issue_next:
                scores(kt + 1, cc)
            m_old = m_ref[:, cols]
            m_new = jnp.maximum(m_old, jnp.max(s, axis=0, keepdims=True))
            alpha[cc] = jnp.exp2(m_old - m_new)
            p = jnp.exp2(s - m_new)
            l_ref[:, cols] = alpha[cc] * l_ref[:, cols] + jnp.sum(p, axis=0, keepdims=True)
            m_ref[:, cols] = m_new
            pb[cc] = p.astype(BF16)

        def values(cc):
            cols = chunk_cols(cc)
            acct_ref[:, cols] = alpha.pop(cc) * acct_ref[:, cols] + _dot(vt, pb.pop(cc))

        _skewed(n_chunks, (softmax, values))

    for cc in range(n_chunks):
        scores(0, cc)

    def body(kt, carry):
        sel_tile(kt, False, True)
        return carry

    lax.fori_loop(0, kt_last, body, 0)
    sel_tile(kt_last, True, False)

    gates = gatest_ref[...]
    for pair in range(N_HEADS // 2):
        g = (2 * pair) // HEADS_PER_GROUP
        feat = slice(g * HEAD_DIM, (g + 1) * HEAD_DIM)
        halves = []
        for n in (2 * pair, 2 * pair + 1):
            cols = slice(n * Q_BLOCK, (n + 1) * Q_BLOCK)
            o_sel = acct_ref[feat, cols] * (1.0 / l_ref[:, cols])
            halves.append(gates[3 * n:3 * n + 1, :] * oct_ref[feat, cols] + gates[3 * n + 1:3 * n + 2, :] * o_sel
                          + gates[3 * n + 2:3 * n + 3, :] * owt_ref[feat, cols])
        o_ref[:, pair * LANES:(pair + 1) * LANES] = jnp.concatenate(halves, axis=0).T


def _attn_prompt_t(qt, ksel, kwinb, vselt, vwint, kc, vct, gatest, ovlt, batch, seq):
    nq = seq // Q_BLOCK
    n_cmp = kc.shape[1]
    n_cols = N_HEADS * Q_BLOCK
    per_b = lambda a: pl.BlockSpec((None,) + a.shape[1:], lambda b, i: (b,) + (0,) * (a.ndim - 1))
    return pl.pallas_call(
        _attn_prompt_t_kernel,
        grid=(batch, nq),
        in_specs=[
            pl.BlockSpec((None, N_HEADS, LANES, Q_BLOCK), lambda b, i: (b, 0, 0, i)),
            pl.BlockSpec((seq, 2 * LANES), lambda b, i: (b, 0)),
            per_b(vselt),
            pl.BlockSpec((seq, LANES), lambda b, i: (b, 0)),
            per_b(vwint),
            pl.BlockSpec((1, n_cmp, LANES), lambda b, i: (b, 0, 0)),
            pl.BlockSpec((1, LANES, n_cmp), lambda b, i: (b, 0, 0)),
            pl.BlockSpec((None, GATE_ROWS, Q_BLOCK), lambda b, i: (b, 0, i)),
            pl.BlockSpec((LANES, n_cmp), lambda b, i: (0, 0)),
        ],
        out_specs=pl.BlockSpec((Q_BLOCK, ATTN_WIDTH), lambda b, i: (b * nq + i, 0)),
        out_shape=jax.ShapeDtypeStruct((batch * seq, ATTN_WIDTH), F32),
        scratch_shapes=[
            pltpu.VMEM((2 * LANES, n_cols), BF16),
            pltpu.VMEM((1, n_cols), F32),
            pltpu.VMEM((1, n_cols), F32),
            pltpu.VMEM((LANES, n_cols), F32),
            pltpu.VMEM((LANES, n_cols), F32),
            pltpu.VMEM((LANES, n_cols), F32),
            pltpu.VMEM((KV_TILE, n_cols), F32),
        ],
        compiler_params=pltpu.CompilerParams(dimension_semantics=("arbitrary", "arbitrary"),
                                             vmem_limit_bytes=VMEM_LIMIT),
        name="attn_prompt",
    )(qt, ksel, vselt, kwinb, vwint, kc, vct, gatest, ovlt)


def _finish_kernel(x_ref, o_ref, sa_ref, ga_ref, pb_ref, gate_ref, gpost_ref, wbra_ref, wout_ref, y_ref):
    ya = _dot((o_ref[...] * sa_ref[...]).astype(BF16), wbra_ref[...])
    mix = ga_ref[...] * ya + pb_ref[...]
    o = _dot(mix.astype(BF16), wout_ref[...])
    ms = jnp.mean(o * o, axis=-1, keepdims=True)
    on = o * lax.rsqrt(ms + RMS_EPS) * gpost_ref[...]
    y_ref[...] = x_ref[...] + gate_ref[...] * on


def _finish(x2d, o_attn, sa, ga, pb, gate, g_post, w_br_a, w_out, tm, rows_per_gate):
    n = x2d.shape[0]
    row = lambda w: pl.BlockSpec((tm, w), lambda i: (i, 0))
    const = lambda shp: pl.BlockSpec(shp, lambda i: (0,) * len(shp))
    if rows_per_gate == 1:
        gate_spec = row(D_MODEL)
    else:
        tiles_per_gate = rows_per_gate // tm
        gate = gate[:, None, :]
        gate_spec = pl.BlockSpec((None, 1, D_MODEL), lambda i: (i // tiles_per_gate, 0, 0))
    return pl.pallas_call(
        _finish_kernel,
        grid=(n // tm,),
        in_specs=[row(D_MODEL), row(ATTN_WIDTH), row(ATTN_WIDTH), row(D_MODEL), row(D_MODEL), gate_spec,
                  const((1, D_MODEL)), const((ATTN_WIDTH, D_MODEL)), const((D_MODEL, D_MODEL))],
        out_specs=row(D_MODEL),
        out_shape=jax.ShapeDtypeStruct((n, D_MODEL), F32),
        compiler_params=pltpu.CompilerParams(dimension_semantics=("arbitrary",), vmem_limit_bytes=VMEM_LIMIT),
        name="finish",
    )(x2d, o_attn, sa, ga, pb, gate, g_post, w_br_a, w_out)


def _attn_decode_kernel(pt_ref, *refs, n_req, n_pages, page_size, wbuf):
    del pt_ref
    page_refs = refs[:n_req * n_pages]
    (win_ref, q_ref, kvnew_ref, gates_ref, pelo_ref, pehi_ref, w1lo_ref, w1hi_ref, w2_ref, ovl_ref, expand_ref,
     o_ref, winout_ref, kcmp_ref, vcmp_ref) = refs[n_req * n_pages:]
    past_len = n_pages * page_size
    n_chunk = past_len // CMP_STRIDE
    reqs = range(n_req)
    heads = lambda b: slice(b * N_HEADS, (b + 1) * N_HEADS)
    per_req = lambda fn: jnp.concatenate([fn(b) for b in reqs], axis=0)
    q8f = per_req(lambda b: q_ref[b])
    q8 = q8f.astype(BF16)
    new_row = lambda which: per_req(lambda b: jnp.broadcast_to(kvnew_ref[b, which:which + 1, :], (N_HEADS, LANES)))

    def stream(ref, which):
        slab = ref[which * KV_HEADS:(which + 1) * KV_HEADS]
        return slab.reshape(KV_WIDTH, slab.shape[-1])

    for i, pr in enumerate(page_refs):
        rows = slice(i * page_size, (i + 1) * page_size)
        kcmp_ref[rows, :] = stream(pr, 0).T
        vcmp_ref[rows, :] = stream(pr, 1).T

    kvc = []
    for kind, src_ref in enumerate((kcmp_ref, vcmp_ref)):
        hid = _compress_hidden(src_ref, n_req * n_chunk, kind, pelo_ref, pehi_ref, w1lo_ref, w1hi_ref)
        kvc.append(_dot(_silu(hid).astype(BF16), w2_ref[kind]).astype(BF16))
    kc, vc = kvc
    chunks = lambda b: slice(b * n_chunk, (b + 1) * n_chunk)

    c_end = lax.broadcasted_iota(jnp.int32, (1, n_chunk), 1) * CMP_STRIDE + (CMP_BLOCK - 1)
    e, den = _masked_softmax_parts(per_req(lambda b: _dot_nt(q8[heads(b)], kc[chunks(b)])), c_end <= past_len)
    p = e * (1.0 / den)
    pb = p.astype(BF16)
    o_cmp = per_req(lambda b: _dot(pb[heads(b)], vc[chunks(b)]))

    row = lax.broadcasted_iota(jnp.int32, p.shape, 0)
    in_g0 = (row & (N_HEADS - 1)) < HEADS_PER_GROUP

    def group_sums(b):
        pr_, g0_ = p[heads(b)], in_g0[heads(b)]
        g0 = jnp.sum(jnp.where(g0_, pr_, 0.0), axis=0, keepdims=True)
        g1 = jnp.sum(jnp.where(g0_, 0.0, pr_), axis=0, keepdims=True)
        return jnp.where(g0_, g0, g1)

    hi, lo = _split_bf16(per_req(group_sums))
    imp = _dot(hi, ovl_ref[...]) + _dot(lo, ovl_ref[...])
    cur = past_len // SEL_BLOCK
    n_blk = -(-(cur + 1) // SUBLANES) * SUBLANES
    imp_t = imp.T[0:n_blk]
    j = lax.broadcasted_iota(jnp.int32, imp_t.shape, 0)
    forced = (j == 0) | (j == cur) | (j == cur - 1)
    bias_t = _top_k_bias_t(imp_t, forced, j <= cur)
    bias_t = jnp.concatenate([bias_t, jnp.full((LANES - n_blk, bias_t.shape[1]), NEG_BIG, F32)], axis=0)
    bias_keys = _dot(bias_t.T.astype(BF16), expand_ref[...])

    pages = lambda b: page_refs[b * n_pages:(b + 1) * n_pages]
    s = per_req(lambda b: jnp.concatenate([_dot(q8[heads(b)], stream(pr, 2).astype(BF16)) for pr in pages(b)],
                                          axis=1)) + bias_keys
    s_new = jnp.sum(q8f * new_row(2), axis=1, keepdims=True)
    m = jnp.maximum(jnp.max(s, axis=1, keepdims=True), s_new)
    e = jnp.exp(s - m)
    e_new = jnp.exp(s_new - m)
    den = jnp.sum(e, axis=1, keepdims=True) + e_new
    eb = e.astype(BF16)

    def sel_values(b):
        acc = None
        for i, pr in enumerate(pages(b)):
            part = _dot_nt(eb[heads(b), i * page_size:(i + 1) * page_size], stream(pr, 3).astype(BF16))
            acc = part if acc is None else acc + part
        return acc

    o_sel = (per_req(sel_values) + e_new * new_row(3)) * (1.0 / den)

    kwpos = past_len - wbuf + lax.broadcasted_iota(jnp.int32, (1, wbuf), 1)
    dt = past_len - kwpos
    valid = (dt >= 0) & (dt < WINDOW) & (kwpos >= 0)
    kw_t = [stream(win_ref.at[b], 0) for b in reqs]
    vw_t = [stream(win_ref.at[b], 1) for b in reqs]
    s = jnp.where(valid, per_req(lambda b: _dot(q8[heads(b)], kw_t[b].astype(BF16))), -jnp.inf)
    s_new = jnp.sum(q8f * new_row(4), axis=1, keepdims=True)
    m = jnp.maximum(jnp.max(s, axis=1, keepdims=True), s_new)
    e = jnp.exp(s - m)
    e_new = jnp.exp(s_new - m)
    den = jnp.sum(e, axis=1, keepdims=True) + e_new
    eb = e.astype(BF16)
    o_win = (per_req(lambda b: _dot_nt(eb[heads(b)], vw_t[b].astype(BF16))) + e_new * new_row(5)) * (1.0 / den)

    gates = per_req(lambda b: gates_ref[b])
    o = gates[:, 0:1] * o_cmp + gates[:, 1:2] * o_sel + gates[:, 2:3] * o_win
    o = jnp.where(in_g0, o, pltpu.roll(o, HEAD_DIM, 1))
    lane = lax.broadcasted_iota(jnp.int32, (KV_WIDTH, wbuf), 1)
    for b in reqs:
        o_ref[b] = o[heads(b)]
        new_cols = jnp.concatenate([kvnew_ref[b], jnp.zeros((SUBLANES - 6, LANES), F32)], axis=0).T
        for kind, old in enumerate((kw_t[b], vw_t[b])):
            shifted = jnp.where(lane == wbuf - 1, new_cols[:, 4 + kind:5 + kind], pltpu.roll(old, wbuf - 1, 1))
            winout_ref[b, kind * KV_HEADS:(kind + 1) * KV_HEADS] = shifted.reshape(KV_HEADS, HEAD_DIM, wbuf)


DECODE_REQS_PER_STEP = 2


def _attn_decode(cache_t, page_table, win_t, q8, kvnew, gates8, cw, ovl, expand):
    page_size = cache_t.shape[-1]
    batch, n_pages = page_table.shape
    wbuf = win_t.shape[-1]
    past_len = n_pages * page_size
    n_req = DECODE_REQS_PER_STEP if batch % DECODE_REQS_PER_STEP == 0 else 1

    def page_spec(r, k):
        return pl.BlockSpec((None, PAGE_ROWS, HEAD_DIM, page_size), lambda i, pt: (pt[i * n_req + r, k], 0, 0, 0))

    per_step = lambda a: pl.BlockSpec((n_req,) + a.shape[1:], lambda i, pt: (i,) + (0,) * (a.ndim - 1))
    const = lambda a: pl.BlockSpec(a.shape, lambda i, pt: (0,) * a.ndim, pipeline_mode=pl.Buffered(1))
    consts = [cw["pe_lo"], cw["pe_hi"], cw["w1_lo"], cw["w1_hi"], cw["w2"], ovl, expand]
    grid_spec = pltpu.PrefetchScalarGridSpec(
        num_scalar_prefetch=1,
        grid=(batch // n_req,),
        in_specs=[page_spec(r, k) for r in range(n_req) for k in range(n_pages)]
        + [per_step(win_t), per_step(q8), per_step(kvnew), per_step(gates8)]
        + [const(a) for a in consts],
        out_specs=(pl.BlockSpec((n_req, N_HEADS, LANES), lambda i, pt: (i, 0, 0)), per_step(win_t)),
        scratch_shapes=[pltpu.VMEM((n_req * past_len, LANES), F32), pltpu.VMEM((n_req * past_len, LANES), F32)],
    )
    o8, win_out = pl.pallas_call(
        functools.partial(_attn_decode_kernel, n_req=n_req, n_pages=n_pages, page_size=page_size, wbuf=wbuf),
        grid_spec=grid_spec,
        out_shape=(jax.ShapeDtypeStruct((batch, N_HEADS, LANES), F32),
                   jax.ShapeDtypeStruct(win_t.shape, F32)),
        compiler_params=pltpu.CompilerParams(dimension_semantics=("arbitrary",), vmem_limit_bytes=VMEM_LIMIT),
        name="attn_decode",
    )(page_table, *([cache_t] * (n_req * n_pages)), win_t, q8, kvnew, gates8, *consts)
    return o8, win_out


def _overlap_matrix(n_cmp_pad, n_cmp, n_selb):
    cs = jnp.arange(n_cmp_pad) * CMP_STRIDE
    ss = jnp.arange(LANES) * SEL_BLOCK
    ov = (cs[:, None] < ss[None, :] + SEL_BLOCK) & (cs[:, None] + CMP_BLOCK > ss[None, :])
    ov = ov & (jnp.arange(n_cmp_pad) < n_cmp)[:, None] & (jnp.arange(LANES) < n_selb)[None, :]
    return ov.astype(BF16)


def _prompt_layer(x, mod, wts):
    batch, seq, _ = x.shape
    x2d = x.reshape(batch * seq, D_MODEL)
    shift, scale, gate = mod[:, 0:D_MODEL], mod[:, D_MODEL:2 * D_MODEL], mod[:, 2 * D_MODEL:]
    tabs = _rope_cos_sin(jnp.arange(seq))
    tm = min(512, seq)
    (qt, kvt, kvc, kwin, ksel, kwinb, vselt, vwint, gatest, sa, ga, pb, utail) = _proj_prompt(
        x2d, shift, scale, wts["g_pre"], wts["w_in"], tabs, wts["conv_w"], wts["w_br_b"], batch, seq, tm)
    kc, vct = _compress_prompt(kvc, wts["cmp"], batch, seq)
    n_chunk = seq // CMP_STRIDE
    ovlt = _overlap_matrix(n_chunk, n_chunk - 1, -(-seq // SEL_BLOCK)).T
    o_attn = _attn_prompt_t(qt, ksel, kwinb, vselt, vwint, kc, vct, gatest, ovlt, batch, seq)
    y = _finish(x2d, o_attn, sa, ga, pb, gate, wts["g_post"], wts["w_br_a"], wts["w_out"], tm, seq)
    return (y.reshape(batch, seq, D_MODEL),
            jnp.transpose(kvt.reshape(batch, 4, KV_HEADS, HEAD_DIM, seq), (0, 4, 1, 2, 3)),
            jnp.transpose(kwin.reshape(batch, 2, KV_HEADS, HEAD_DIM, kwin.shape[-1]), (0, 4, 1, 2, 3)),
            utail[:, SUBLANES - (CONV_K - 1):])


def _sample_layer(x, mod, cache, page_table, win, conv_state, wts):
    batch, dec_seq, _ = x.shape
    assert dec_seq == 1
    n_pages, page_size, wbuf = page_table.shape[1], cache.shape[1], win.shape[1]
    past_len = n_pages * page_size
    assert past_len % SEL_BLOCK == 0 and wbuf == WINDOW and past_len // SEL_BLOCK < LANES
    x2d = x.reshape(batch, D_MODEL)
    shift, scale, gate = mod[:, 0:D_MODEL], mod[:, D_MODEL:2 * D_MODEL], mod[:, 2 * D_MODEL:]
    tabs = _rope_cos_sin(jnp.full((1,), past_len, jnp.int32))
    cbuf = conv_state.reshape(batch, (CONV_K - 1) * CONV_WIDTH)
    qpad, kvnew, gates, sa, ga, pb, u = _proj_sample(
        x2d, shift, scale, wts["g_pre"], wts["w_in"], tabs, wts["conv_w"], wts["w_br_b"], cbuf)
    n_gate = N_HEADS * 3
    gates8 = jnp.pad(gates[:, :n_gate].reshape(batch, N_HEADS, 3), ((0, 0), (0, 0), (0, LANES - 3)))
    n_chunk = past_len // CMP_STRIDE
    ovl = _overlap_matrix(n_chunk, n_chunk - 1, past_len // SEL_BLOCK + 1)
    expand = (jnp.arange(LANES)[:, None] == (jnp.arange(past_len) // SEL_BLOCK)[None, :]).astype(BF16)
    cache_t = jnp.transpose(cache, (0, 2, 3, 4, 1)).reshape(cache.shape[0], PAGE_ROWS, HEAD_DIM, page_size)
    win_t = jnp.transpose(win, (0, 2, 3, 4, 1)).reshape(batch, WIN_ROWS, HEAD_DIM, wbuf)
    o8, win_out = _attn_decode(cache_t, page_table, win_t, qpad.reshape(batch, N_HEADS, LANES).astype(F32),
                               kvnew.reshape(batch, 6, LANES), gates8, wts["cmp"], ovl, expand)
    win_out = jnp.transpose(win_out.reshape(batch, 2, KV_HEADS, HEAD_DIM, wbuf), (0, 4, 1, 2, 3))
    y = _finish(x2d, o8[:, :, :HEAD_DIM].reshape(batch, ATTN_WIDTH), sa, ga, pb, gate, wts["g_post"], wts["w_br_a"],
                wts["w_out"], batch, 1)
    return (y.reshape(batch, 1, D_MODEL),
            kvnew[:, :4 * KV_WIDTH].reshape(batch, 1, 4, KV_HEADS, HEAD_DIM),
            win_out,
            jnp.stack([conv_state[:, CONV_K - 2], u], axis=1))


def _prep_weights(w_ada, b_ada, g_pre, g_post, w_in, pe_cmp, w_cmp1, w_cmp2, conv_w, w_br_a, w_br_b, w_out):
    n_unpadded_gate = HEADS_PER_GROUP * KV_HEADS * 3
    w_t = w_in.T
    w_pad = jnp.concatenate(
        [w_t[:C_G + n_unpadded_gate], jnp.zeros((LANES - n_unpadded_gate, D_MODEL), w_in.dtype),
         w_t[C_G + n_unpadded_gate:]], axis=0).astype(BF16)
    half = CMP_STRIDE * HEAD_DIM
    return dict(
        w_ada=w_ada, b_ada=b_ada, g_pre=g_pre.reshape(1, -1), g_post=g_post.reshape(1, -1), w_in=w_pad,
        cmp=_compress_weights(pe_cmp, w_cmp1, w_cmp2), conv_w=conv_w,
        w_br_a=w_br_a.astype(BF16), w_br_b=w_br_b.astype(BF16), w_out=w_out.astype(BF16))


def kernel(x_prompt, x_sample, cache_kv_pages, state_win_kv, state_conv, page_table, c_prompt, c_sample, w_ada, b_ada, g_pre, g_post, w_in, pe_cmp, w_cmp1, w_cmp2, conv_w, w_br_a, w_br_b, w_out):
    depth = w_in.shape[0]
    assert depth == 1
    wts = _prep_weights(w_ada[0], b_ada[0], g_pre[0], g_post[0], w_in[0], pe_cmp[0], w_cmp1[0], w_cmp2[0],
                        conv_w[0], w_br_a[0], w_br_b[0], w_out[0])
    n_prompt = c_prompt.shape[0]
    mod = _ada(jnp.concatenate([c_prompt, c_sample], axis=0), wts["w_ada"], wts["b_ada"])
    yp, kvp, wp, cp = _prompt_layer(x_prompt, mod[:n_prompt], wts)
    ys, kvs, ws, cs = _sample_layer(x_sample, mod[n_prompt:], cache_kv_pages[0], page_table, state_win_kv[0],
                                    state_conv[0], wts)
    return (yp, ys, kvp[None], wp[None], cp[None], kvs[None], ws[None], cs[None])
```

```python
import functools

import jax
import jax.numpy as jnp
from jax import lax
from jax.experimental import pallas as pl
from jax.experimental.pallas import tpu as pltpu

F32 = jnp.float32
BF16 = jnp.bfloat16

D_MODEL = 1024
N_HEADS = 8
KV_HEADS = 2
HEADS_PER_GROUP = N_HEADS // KV_HEADS
HEAD_DIM = 64
ROPE_DIM = HEAD_DIM // 4
ROPE_THETA = 500000.0
CMP_BLOCK = 32
CMP_STRIDE = 16
CMP_HIDDEN = 256
SEL_BLOCK = 64
N_SEL = 16
WINDOW = 512
Q_BLOCK = 256
CONV_WIDTH = D_MODEL // 2
CONV_K = 3
ATTN_WIDTH = N_HEADS * HEAD_DIM
KV_WIDTH = KV_HEADS * HEAD_DIM
RMS_EPS = 1e-6

LANES = 128
SUBLANES = 8
VMEM_LIMIT = 56 * 1024 * 1024

C_Q = 0
C_KV = C_Q + ATTN_WIDTH
C_G = C_KV + 6 * KV_WIDTH
C_A = C_G + LANES
C_CB = C_A + ATTN_WIDTH
C_CC = C_CB + CONV_WIDTH
C_CX = C_CC + CONV_WIDTH
C_CG = C_CX + CONV_WIDTH
C_MA = C_CG + CONV_WIDTH
C_MB = C_MA + D_MODEL
IN_PAD = C_MB + D_MODEL

PAGE_ROWS = 4 * KV_HEADS
WIN_ROWS = 2 * KV_HEADS

GATE_ROWS = 32
LOG2_E = 1.4426950408889634

NEG_BIG = -1e30
KV_TILE = 512


def _sigmoid(x):
    return 1.0 / (1.0 + jnp.exp(-x))


def _silu(x):
    return x * _sigmoid(x)


def _dot(a, b):
    return jnp.dot(a, b, preferred_element_type=F32)


def _dot_nt(a, b):
    return lax.dot_general(a, b, (((1,), (1,)), ((), ())), preferred_element_type=F32)


def _rope_cos_sin(pos):
    half = ROPE_DIM // 2
    inv = ROPE_THETA ** (-jnp.arange(half, dtype=F32) / half)
    ang = pos.astype(F32)[:, None] * inv[None, :]
    return jnp.concatenate([jnp.cos(ang), jnp.sin(ang)], axis=1)


def _rope_lane_tables(cs):
    half = ROPE_DIM // 2
    shape = (cs.shape[0], LANES)
    d = lax.broadcasted_iota(jnp.int32, shape, 1) & (HEAD_DIM - 1)
    a, p, m = jnp.ones(shape, F32), jnp.zeros(shape, F32), jnp.zeros(shape, F32)
    for f in range(half):
        cos_f, sin_f = cs[:, f:f + 1], cs[:, half + f:half + f + 1]
        a = jnp.where(d == f, cos_f, jnp.where(d == f + half, cos_f, a))
        p = jnp.where(d == f + half, sin_f, p)
        m = jnp.where(d == f, -sin_f, m)
    return a, p, m


def _rope(x, ra, rp, rm):
    half = ROPE_DIM // 2
    return x * ra + pltpu.roll(x, half, 1) * rp + pltpu.roll(x, LANES - half, 1) * rm


def _ada_kernel(c_ref, w_ref, b_ref, o_ref):
    c = _silu(c_ref[...]).astype(BF16)
    o_ref[...] = _dot(c, w_ref[...].astype(BF16)) + b_ref[...]


def _ada(c_all, w_ada, b_ada):
    n = c_all.shape[0]
    tn = 512
    return pl.pallas_call(
        _ada_kernel,
        grid=(3 * D_MODEL // tn,),
        in_specs=[
            pl.BlockSpec((n, D_MODEL), lambda j: (0, 0)),
            pl.BlockSpec((D_MODEL, tn), lambda j: (0, j)),
            pl.BlockSpec((1, tn), lambda j: (0, j)),
        ],
        out_specs=pl.BlockSpec((n, tn), lambda j: (0, j)),
        out_shape=jax.ShapeDtypeStruct((n, 3 * D_MODEL), F32),
        compiler_params=pltpu.CompilerParams(dimension_semantics=("arbitrary",), vmem_limit_bytes=VMEM_LIMIT),
        name="ada",
    )(c_all, w_ada, b_ada.reshape(1, -1))


def _proj_common(x_ref, shift_ref, scale_ref, gpre_ref, w_ref, cs_ref):
    x = x_ref[...]
    ms = jnp.mean(x * x, axis=-1, keepdims=True)
    xn = x * lax.rsqrt(ms + RMS_EPS) * gpre_ref[...]
    h = xn * (1.0 + scale_ref[...]) + shift_ref[...]
    hb = h.astype(BF16)

    def seg(lo, hi):
        return _dot_nt(hb, w_ref[lo:hi, :])

    return seg, _rope_lane_tables(cs_ref[...])


def _padded_q_heads(seg, rope, scale):
    zq = seg(C_Q, C_Q + ATTN_WIDTH)
    lane = lax.broadcasted_iota(jnp.int32, (zq.shape[0], LANES), 1)
    lower = lane < HEAD_DIM
    heads = []
    for j in range(ATTN_WIDTH // LANES):
        c = _rope(zq[:, j * LANES:(j + 1) * LANES], *rope) * scale
        r = pltpu.roll(c, HEAD_DIM, 1)
        if (2 * j) // HEADS_PER_GROUP == 0:
            heads += [jnp.where(lower, c, 0.0), jnp.where(lower, r, 0.0)]
        else:
            heads += [jnp.where(lower, 0.0, r), jnp.where(lower, 0.0, c)]
    return heads


def _kv_pieces(seg, rope):
    zkv = seg(C_KV, C_KV + 6 * KV_WIDTH)
    pieces = []
    for p in range(6):
        c = zkv[:, p * LANES:(p + 1) * LANES]
        pieces.append(_rope(c, *rope) if p % 2 == 0 else c)
    return pieces


def _branch_b(seg, um2, um1, u, convw_ref, wbrb_ref):
    cb = seg(C_CB, C_CB + CONV_WIDTH)
    conv = convw_ref[0:1, :] * um2
    conv = conv + convw_ref[1:2, :] * um1
    conv = conv + convw_ref[2:3, :] * u
    ybin = cb * conv * _silu(seg(C_CG, C_CG + CONV_WIDTH))
    yb = _dot(ybin.astype(BF16), wbrb_ref[...])
    gb = _sigmoid(seg(C_MB, C_MB + D_MODEL))
    return gb * yb


def _proj_prompt_kernel(x_ref, shift_ref, scale_ref, gpre_ref, w_ref, cs_ref, convw_ref, wbrb_ref,
                        qt_ref, kvt_ref, kvcmp_ref, kwin_ref, ksel_ref, kwinb_ref, vselt_ref, vwint_ref, gatest_ref,
                        sa_ref, ga_ref, pb_ref, utail_ref, carry_ref):
    ti = pl.program_id(1)
    tm = x_ref.shape[0]
    seg, rope = _proj_common(x_ref, shift_ref, scale_ref, gpre_ref, w_ref, cs_ref)
    for n, head in enumerate(_padded_q_heads(seg, rope, HEAD_DIM ** -0.5 * LOG2_E)):
        qt_ref[n] = head.T.astype(BF16)

    pieces = _kv_pieces(seg, rope)
    pieces_t = [p.T for p in pieces]
    for p in range(4):
        kvt_ref[p * KV_WIDTH:(p + 1) * KV_WIDTH, :] = pieces_t[p]
    kvcmp_ref[:, 0:LANES] = pieces[0]
    kvcmp_ref[:, LANES:2 * LANES] = pieces[1]
    kwin_ref[0:KV_WIDTH, :] = pieces_t[4]
    kwin_ref[KV_WIDTH:2 * KV_WIDTH, :] = pieces_t[5]
    kwinb_ref[...] = pieces[4].astype(BF16)
    vselt_ref[...] = pieces_t[3].astype(BF16)
    vwint_ref[...] = pieces_t[5].astype(BF16)
    row = ti * tm + lax.broadcasted_iota(jnp.int32, (tm, LANES), 0)
    lane = lax.broadcasted_iota(jnp.int32, (tm, LANES), 1)
    onehot = jnp.where(lane == row // SEL_BLOCK, 1.0, 0.0)
    ksel_ref[:, 0:LANES] = pieces[2].astype(BF16)
    ksel_ref[:, LANES:2 * LANES] = onehot.astype(BF16)

    gatest_ref[...] = _sigmoid(seg(C_G, C_G + LANES)).T[0:GATE_ROWS, :]
    sa_ref[...] = _silu(seg(C_A, C_A + ATTN_WIDTH)).astype(BF16)
    ga_ref[...] = _sigmoid(seg(C_MA, C_MA + D_MODEL)).astype(BF16)

    @pl.when(ti == 0)
    def _():
        carry_ref[...] = jnp.zeros_like(carry_ref)

    u = seg(C_CC, C_CC + CONV_WIDTH) * seg(C_CX, C_CX + CONV_WIDTH)
    r = lax.broadcasted_iota(jnp.int32, u.shape, 0)
    c7 = carry_ref[SUBLANES - 1:SUBLANES, :]
    c6 = carry_ref[SUBLANES - 2:SUBLANES - 1, :]
    um1 = jnp.where(r == 0, c7, pltpu.roll(u, 1, 0))
    um2 = jnp.where(r == 0, c6, jnp.where(r == 1, c7, pltpu.roll(u, 2, 0)))
    pb_ref[...] = _branch_b(seg, um2, um1, u, convw_ref, wbrb_ref).astype(BF16)
    tail = u[tm - SUBLANES:tm, :]
    carry_ref[...] = tail
    utail_ref[0] = tail


def _proj_prompt(x2d, shift, scale, g_pre, w_pad, rope_cs, conv_w, w_br_b, batch, seq, tm):
    n = batch * seq
    nt = seq // tm
    n_keep = min(WINDOW, seq)
    assert n_keep % tm == 0
    row = lambda w: pl.BlockSpec((tm, w), lambda b, t: (b * nt + t, 0))
    per_b = lambda w: pl.BlockSpec((None, 1, w), lambda b, t: (b, 0, 0))
    const = lambda shp: pl.BlockSpec(shp, lambda b, t: (0,) * len(shp))
    tab = pl.BlockSpec((tm, ROPE_DIM), lambda b, t: (t, 0))
    feat = lambda rows: pl.BlockSpec((None, rows, tm), lambda b, t: (b, 0, t))
    out_shapes = (
        jax.ShapeDtypeStruct((batch, N_HEADS, LANES, seq), BF16),
        jax.ShapeDtypeStruct((batch, 4 * KV_WIDTH, seq), F32),
        jax.ShapeDtypeStruct((n, 2 * KV_WIDTH), F32),
        jax.ShapeDtypeStruct((batch, 2 * KV_WIDTH, n_keep), F32),
        jax.ShapeDtypeStruct((n, 2 * LANES), BF16),
        jax.ShapeDtypeStruct((n, KV_WIDTH), BF16),
        jax.ShapeDtypeStruct((batch, KV_WIDTH, seq), BF16),
        jax.ShapeDtypeStruct((batch, KV_WIDTH, seq), BF16),
        jax.ShapeDtypeStruct((batch, GATE_ROWS, seq), F32),
        jax.ShapeDtypeStruct((n, ATTN_WIDTH), BF16),
        jax.ShapeDtypeStruct((n, D_MODEL), BF16),
        jax.ShapeDtypeStruct((n, D_MODEL), BF16),
        jax.ShapeDtypeStruct((batch, SUBLANES, CONV_WIDTH), F32),
    )
    out_specs = (
        pl.BlockSpec((None, N_HEADS, LANES, tm), lambda b, t: (b, 0, 0, t)), feat(4 * KV_WIDTH),
        row(2 * KV_WIDTH),
        pl.BlockSpec((None, 2 * KV_WIDTH, tm), lambda b, t: (b, 0, jnp.maximum(t - (nt - n_keep // tm), 0))),
        row(2 * LANES), row(KV_WIDTH), feat(KV_WIDTH), feat(KV_WIDTH),
        feat(GATE_ROWS), row(ATTN_WIDTH), row(D_MODEL), row(D_MODEL),
        pl.BlockSpec((1, SUBLANES, CONV_WIDTH), lambda b, t: (b, 0, 0)),
    )
    return pl.pallas_call(
        _proj_prompt_kernel,
        grid=(batch, nt),
        in_specs=[row(D_MODEL), per_b(D_MODEL), per_b(D_MODEL), const((1, D_MODEL)), const((IN_PAD, D_MODEL)),
                  tab, const((CONV_K, CONV_WIDTH)), const((CONV_WIDTH, D_MODEL))],
        out_specs=out_specs,
        out_shape=out_shapes,
        scratch_shapes=[pltpu.VMEM((SUBLANES, CONV_WIDTH), F32)],
        compiler_params=pltpu.CompilerParams(dimension_semantics=("arbitrary", "arbitrary"),
                                             vmem_limit_bytes=VMEM_LIMIT),
        name="proj_prompt",
    )(x2d, shift[:, None, :], scale[:, None, :], g_pre, w_pad, rope_cs, conv_w, w_br_b)


def _proj_sample_kernel(x_ref, shift_ref, scale_ref, gpre_ref, w_ref, cs_ref, convw_ref, wbrb_ref,
                        cbuf_ref, qpad_ref, kvnew_ref, gates_ref, sa_ref, ga_ref, pb_ref, u_ref):
    seg, rope = _proj_common(x_ref, shift_ref, scale_ref, gpre_ref, w_ref, cs_ref)
    for n, head in enumerate(_padded_q_heads(seg, rope, HEAD_DIM ** -0.5)):
        qpad_ref[:, n * LANES:(n + 1) * LANES] = head.astype(BF16)
    pieces = _kv_pieces(seg, rope)
    for p in range(6):
        kvnew_ref[:, p * LANES:(p + 1) * LANES] = pieces[p]
    gates_ref[...] = _sigmoid(seg(C_G, C_G + LANES))
    sa_ref[...] = _silu(seg(C_A, C_A + ATTN_WIDTH)).astype(BF16)
    ga_ref[...] = _sigmoid(seg(C_MA, C_MA + D_MODEL)).astype(BF16)
    u = seg(C_CC, C_CC + CONV_WIDTH) * seg(C_CX, C_CX + CONV_WIDTH)
    um2 = cbuf_ref[:, 0:CONV_WIDTH]
    um1 = cbuf_ref[:, CONV_WIDTH:2 * CONV_WIDTH]
    pb_ref[...] = _branch_b(seg, um2, um1, u, convw_ref, wbrb_ref).astype(BF16)
    u_ref[...] = u


def _proj_sample(x2d, shift, scale, g_pre, w_pad, rope_cs, conv_w, w_br_b, cbuf):
    n = x2d.shape[0]
    full = lambda shp: pl.BlockSpec(shp, lambda i: (0,) * len(shp))
    out_shapes = (
        jax.ShapeDtypeStruct((n, N_HEADS * LANES), BF16),
        jax.ShapeDtypeStruct((n, 6 * KV_WIDTH), F32),
        jax.ShapeDtypeStruct((n, LANES), F32),
        jax.ShapeDtypeStruct((n, ATTN_WIDTH), BF16),
        jax.ShapeDtypeStruct((n, D_MODEL), BF16),
        jax.ShapeDtypeStruct((n, D_MODEL), BF16),
        jax.ShapeDtypeStruct((n, CONV_WIDTH), F32),
    )
    return pl.pallas_call(
        _proj_sample_kernel,
        grid=(1,),
        in_specs=[full((n, D_MODEL)), full((n, D_MODEL)), full((n, D_MODEL)), full((1, D_MODEL)),
                  full((IN_PAD, D_MODEL)), full((1, ROPE_DIM)),
                  full((CONV_K, CONV_WIDTH)), full((CONV_WIDTH, D_MODEL)), full((n, 2 * CONV_WIDTH))],
        out_specs=tuple(full(s.shape) for s in out_shapes),
        out_shape=out_shapes,
        compiler_params=pltpu.CompilerParams(dimension_semantics=("arbitrary",), vmem_limit_bytes=VMEM_LIMIT),
        name="proj_sample",
    )(x2d, shift, scale, g_pre, w_pad, rope_cs, conv_w, w_br_b, cbuf)


CHUNK_LANES = CMP_STRIDE * KV_WIDTH


def _compress_weights(pe_cmp, w_cmp1, w_cmp2):
    half = CHUNK_LANES // KV_HEADS
    z2 = jnp.zeros_like(w_cmp2)
    w2 = jnp.concatenate([jnp.concatenate([w_cmp2, z2], axis=2), jnp.concatenate([z2, w_cmp2], axis=2)], axis=1)
    return dict(w1_lo=w_cmp1[:, :half].astype(BF16), w1_hi=w_cmp1[:, half:].astype(BF16), w2=w2.astype(BF16),
                pe_lo=pe_cmp[:, :CMP_STRIDE].reshape(2, 1, half), pe_hi=pe_cmp[:, CMP_STRIDE:].reshape(2, 1, half))


def _compress_hidden(src_ref, n_chunk, kind, pelo_ref, pehi_ref, w1lo_ref, w1hi_ref):
    rows = [src_ref[pl.ds(r, n_chunk, stride=CMP_STRIDE), :] for r in range(CMP_STRIDE)]
    lower = lax.broadcasted_iota(jnp.int32, rows[0].shape, 1) < HEAD_DIM
    hidden = []
    for g in range(KV_HEADS):
        pairs = []
        for r in range(0, CMP_STRIDE, 2):
            a, b = rows[r], rows[r + 1]
            pairs.append(jnp.where(lower, a, pltpu.roll(b, HEAD_DIM, 1)) if g == 0
                         else jnp.where(lower, pltpu.roll(a, HEAD_DIM, 1), b))
        c = jnp.concatenate(pairs, axis=1)
        lo = _dot((c + pelo_ref[kind]).astype(BF16), w1lo_ref[kind])
        hi = _dot((c + pehi_ref[kind]).astype(BF16), w1hi_ref[kind])
        hidden.append(lo + pltpu.roll(hi, n_chunk - 1, 0))
    return jnp.concatenate(hidden, axis=1)


def _compress_prompt_kernel(kc_ref, vc_ref, pelo_ref, pehi_ref, w1lo_ref, w1hi_ref, w2_ref, kc_out_ref, vct_out_ref):
    n_chunk = kc_ref.shape[0] // CMP_STRIDE
    outs = []
    for kind, src_ref in enumerate((kc_ref, vc_ref)):
        hid = _compress_hidden(src_ref, n_chunk, kind, pelo_ref, pehi_ref, w1lo_ref, w1hi_ref)
        outs.append(_dot(_silu(hid).astype(BF16), w2_ref[kind]))
    kc_out_ref[0] = outs[0].astype(BF16)
    vct_out_ref[0] = outs[1].T.astype(BF16)


def _compress_prompt(kvc, cw, batch, seq):
    n_chunk = seq // CMP_STRIDE
    const = lambda a: pl.BlockSpec(a.shape, lambda b: (0,) * a.ndim)
    return pl.pallas_call(
        _compress_prompt_kernel,
        grid=(batch,),
        in_specs=[
            pl.BlockSpec((seq, KV_WIDTH), lambda b: (b, 0)),
            pl.BlockSpec((seq, KV_WIDTH), lambda b: (b, 1)),
            const(cw["pe_lo"]), const(cw["pe_hi"]), const(cw["w1_lo"]), const(cw["w1_hi"]), const(cw["w2"]),
        ],
        out_specs=(pl.BlockSpec((1, n_chunk, KV_WIDTH), lambda b: (b, 0, 0)),
                   pl.BlockSpec((1, KV_WIDTH, n_chunk), lambda b: (b, 0, 0))),
        out_shape=(jax.ShapeDtypeStruct((batch, n_chunk, KV_WIDTH), BF16),
                   jax.ShapeDtypeStruct((batch, KV_WIDTH, n_chunk), BF16)),
        compiler_params=pltpu.CompilerParams(dimension_semantics=("arbitrary",), vmem_limit_bytes=VMEM_LIMIT),
        name="compress_prompt",
    )(kvc, kvc, cw["pe_lo"], cw["pe_hi"], cw["w1_lo"], cw["w1_hi"], cw["w2"])


def _split_bf16(x):
    hi = x.astype(BF16)
    lo = (x - hi.astype(F32)).astype(BF16)
    return hi, lo


def _masked_softmax_parts(s, valid):
    s = jnp.where(valid, s, -jnp.inf)
    m = jnp.max(s, axis=-1, keepdims=True)
    m = jnp.where(m == -jnp.inf, 0.0, m)
    e = jnp.exp(s - m)
    den = jnp.maximum(jnp.sum(e, axis=-1, keepdims=True), 1e-30)
    return e, den


COL_CHUNK = 256


def _skewed(n, stages):
    for step in range(n + len(stages) - 1):
        for si, stage in enumerate(stages):
            if 0 <= step - si < n:
                stage(step - si)


def _skewed_thunks(n, stages):
    return [functools.partial(stage, step - si) for step in range(n + len(stages) - 1)
            for si, stage in enumerate(stages) if 0 <= step - si < n]


N_FORCED = 3


def _top_k_bias_t(imp, forced, valid, fillers=()):
    blk = lax.broadcasted_iota(jnp.int32, imp.shape, 0).astype(F32)
    bias = jnp.where(forced, 0.0, NEG_BIG)
    score = jnp.where(forced, -jnp.inf, jnp.where(valid, imp, -1.0))
    fillers = list(fillers)
    for _ in range(N_SEL - N_FORCED):
        m = jnp.max(score, axis=0, keepdims=True)
        idx = jnp.min(jnp.where(score == m, blk, float(LANES)), axis=0, keepdims=True)
        hit = blk == idx
        bias = jnp.where(hit, 0.0, bias)
        score = jnp.where(hit, -jnp.inf, score)
        for thunk in fillers[:2]:
            thunk()
        fillers = fillers[2:]
    for thunk in fillers:
        thunk()
    return bias


def _masked_softmax_parts_t(s, valid):
    s = jnp.where(valid, s, -jnp.inf)
    m = jnp.max(s, axis=0, keepdims=True)
    m = jnp.where(m == -jnp.inf, 0.0, m)
    e = jnp.exp2(s - m)
    den = jnp.maximum(jnp.sum(e, axis=0, keepdims=True), 1e-30)
    return e, den


def _attn_prompt_t_kernel(qt_ref, kaug_ref, vselt_ref, kwin_ref, vwint_ref, kc_ref, vct_ref, gatest_ref, ovlt_ref,
                          o_ref, qaugt_ref, m_ref, l_ref, acct_ref, oct_ref, owt_ref, s_ref):
    qb = pl.program_id(1)
    q0 = qb * Q_BLOCK
    n_cmp = kc_ref.shape[1]
    n_cols = N_HEADS * Q_BLOCK
    n_chunks = n_cols // COL_CHUNK
    heads_per_chunk = COL_CHUNK // Q_BLOCK

    for n in range(N_HEADS):
        qaugt_ref[0:LANES, n * Q_BLOCK:(n + 1) * Q_BLOCK] = qt_ref[n]

    def qpos_cols(cols):
        c = lax.broadcasted_iota(jnp.int32, (1, cols), 1)
        return q0 + (c & (Q_BLOCK - 1))

    def chunk_cols(cc):
        return slice(cc * COL_CHUNK, (cc + 1) * COL_CHUNK)

    kc = kc_ref[0]
    vct = vct_ref[0]
    c_end = lax.broadcasted_iota(jnp.int32, (n_cmp, 1), 0) * CMP_STRIDE + (CMP_BLOCK - 1)
    psum = [None] * KV_HEADS
    cs, cp = {}, {}

    def cmp_scores(cc):
        cs[cc] = _dot(kc, qaugt_ref[0:LANES, chunk_cols(cc)])

    def cmp_softmax(cc):
        e, den = _masked_softmax_parts_t(cs.pop(cc), c_end <= qpos_cols(COL_CHUNK))
        p = e * (1.0 / den)
        cp[cc] = p.astype(BF16)
        g = (cc * heads_per_chunk) // HEADS_PER_GROUP
        part = p[:, 0:Q_BLOCK]
        for h in range(1, heads_per_chunk):
            part = part + p[:, h * Q_BLOCK:(h + 1) * Q_BLOCK]
        psum[g] = part if psum[g] is None else psum[g] + part

    def cmp_values(cc):
        oct_ref[:, chunk_cols(cc)] = _dot(vct, cp.pop(cc))

    _skewed(n_chunks, (cmp_scores, cmp_softmax, cmp_values))

    cur = qpos_cols(KV_HEADS * Q_BLOCK) // SEL_BLOCK
    j = lax.broadcasted_iota(jnp.int32, (LANES, KV_HEADS * Q_BLOCK), 0)
    forced = (j == 0) | (j == cur) | (j == cur - 1)
    hi, lo = _split_bf16(jnp.concatenate(psum, axis=1))
    imp = _dot(ovlt_ref[...], hi) + _dot(ovlt_ref[...], lo)

    wk = kwin_ref.shape[0] if kwin_ref.shape[0] < WINDOW + Q_BLOCK else WINDOW + Q_BLOCK
    start = pl.multiple_of(jnp.maximum(q0 - WINDOW, 0), Q_BLOCK)
    kw = kwin_ref[pl.ds(start, wk), :]
    vwt = vwint_ref[:, pl.ds(start, wk)]
    kwpos = start + lax.broadcasted_iota(jnp.int32, (wk, 1), 0)
    ws, we, wden = {}, {}, {}

    def win_scores(cc):
        ws[cc] = _dot(kw, qaugt_ref[0:LANES, chunk_cols(cc)])

    def win_softmax(cc):
        dt = qpos_cols(COL_CHUNK) - kwpos
        e, wden[cc] = _masked_softmax_parts_t(ws.pop(cc), (dt >= 0) & (dt < WINDOW))
        we[cc] = e.astype(BF16)

    def win_values(cc):
        owt_ref[:, chunk_cols(cc)] = _dot(vwt, we.pop(cc)) * (1.0 / wden.pop(cc))

    bias = _top_k_bias_t(imp, forced, j <= cur,
                         _skewed_thunks(n_chunks, (win_scores, win_softmax, win_values))).astype(BF16)
    for n in range(N_HEADS):
        g = n // HEADS_PER_GROUP
        qaugt_ref[LANES:2 * LANES, n * Q_BLOCK:(n + 1) * Q_BLOCK] = bias[:, g * Q_BLOCK:(g + 1) * Q_BLOCK]

    m_ref[...] = jnp.full(m_ref.shape, NEG_BIG, F32)
    l_ref[...] = jnp.zeros(l_ref.shape, F32)
    acct_ref[...] = jnp.zeros(acct_ref.shape, F32)
    kt_last = (q0 + Q_BLOCK - 1) // KV_TILE

    def key_tile(kt):
        return pl.multiple_of(kt * KV_TILE, KV_TILE)

    def scores(kt, cc):
        s_ref[:, chunk_cols(cc)] = _dot(kaug_ref[pl.ds(key_tile(kt), KV_TILE), :], qaugt_ref[:, chunk_cols(cc)])

    def sel_tile(kt, causal, issue_next):
        k0 = key_tile(kt)
        vt = vselt_ref[:, pl.ds(k0, KV_TILE)]
        pb, alpha = {}, {}

        def softmax(cc):
            cols = chunk_cols(cc)
            s = s_ref[:, cols]
            if causal:
                kpos = k0 + lax.broadcasted_iota(jnp.int32, (KV_TILE, 1), 0)
                s = jnp.where(kpos <= qpos_cols(COL_CHUNK), s, NEG_BIG)
            if issue_next:
                scores(kt + 1, cc)
            m_old = m_ref[:, cols]
            m_new = jnp.maximum(m_old, jnp.max(s, axis=0, keepdims=True))
            alpha[cc] = jnp.exp2(m_old - m_new)
            p = jnp.exp2(s - m_new)
            l_ref[:, cols] = alpha[cc] * l_ref[:, cols] + jnp.sum(p, axis=0, keepdims=True)
            m_ref[:, cols] = m_new
            pb[cc] = p.astype(BF16)

        def values(cc):
            cols = chunk_cols(cc)
            acct_ref[:, cols] = alpha.pop(cc) * acct_ref[:, cols] + _dot(vt, pb.pop(cc))

        _skewed(n_chunks, (softmax, values))

    for cc in range(n_chunks):
        scores(0, cc)

    def body(kt, carry):
        sel_tile(kt, False, True)
        return carry

    lax.fori_loop(0, kt_last, body, 0)
    sel_tile(kt_last, True, False)

    gates = gatest_ref[...]
    for pair in range(N_HEADS // 2):
        g = (2 * pair) // HEADS_PER_GROUP
        feat = slice(g * HEAD_DIM, (g + 1) * HEAD_DIM)
        halves = []
        for n in (2 * pair, 2 * pair + 1):
            cols = slice(n * Q_BLOCK, (n + 1) * Q_BLOCK)
            o_sel = acct_ref[feat, cols] * (1.0 / l_ref[:, cols])
            halves.append(gates[3 * n:3 * n + 1, :] * oct_ref[feat, cols] + gates[3 * n + 1:3 * n + 2, :] * o_sel
                          + gates[3 * n + 2:3 * n + 3, :] * owt_ref[feat, cols])
        o_ref[:, pair * LANES:(pair + 1) * LANES] = jnp.concatenate(halves, axis=0).T


def _attn_prompt_t(qt, ksel, kwinb, vselt, vwint, kc, vct, gatest, ovlt, batch, seq):
    nq = seq // Q_BLOCK
    n_cmp = kc.shape[1]
    n_cols = N_HEADS * Q_BLOCK
    per_b = lambda a: pl.BlockSpec((None,) + a.shape[1:], lambda b, i: (b,) + (0,) * (a.ndim - 1))
    return pl.pallas_call(
        _attn_prompt_t_kernel,
        grid=(batch, nq),
        in_specs=[
            pl.BlockSpec((None, N_HEADS, LANES, Q_BLOCK), lambda b, i: (b, 0, 0, i)),
            pl.BlockSpec((seq, 2 * LANES), lambda b, i: (b, 0)),
            per_b(vselt),
            pl.BlockSpec((seq, LANES), lambda b, i: (b, 0)),
            per_b(vwint),
            pl.BlockSpec((1, n_cmp, LANES), lambda b, i: (b, 0, 0)),
            pl.BlockSpec((1, LANES, n_cmp), lambda b, i: (b, 0, 0)),
            pl.BlockSpec((None, GATE_ROWS, Q_BLOCK), lambda b, i: (b, 0, i)),
            pl.BlockSpec((LANES, n_cmp), lambda b, i: (0, 0)),
        ],
        out_specs=pl.BlockSpec((Q_BLOCK, ATTN_WIDTH), lambda b, i: (b * nq + i, 0)),
        out_shape=jax.ShapeDtypeStruct((batch * seq, ATTN_WIDTH), F32),
        scratch_shapes=[
            pltpu.VMEM((2 * LANES, n_cols), BF16),
            pltpu.VMEM((1, n_cols), F32),
            pltpu.VMEM((1, n_cols), F32),
            pltpu.VMEM((LANES, n_cols), F32),
            pltpu.VMEM((LANES, n_cols), F32),
            pltpu.VMEM((LANES, n_cols), F32),
            pltpu.VMEM((KV_TILE, n_cols), F32),
        ],
        compiler_params=pltpu.CompilerParams(dimension_semantics=("arbitrary", "arbitrary"),
                                             vmem_limit_bytes=VMEM_LIMIT),
        name="attn_prompt",
    )(qt, ksel, vselt, kwinb, vwint, kc, vct, gatest, ovlt)


def _finish_kernel(x_ref, o_ref, sa_ref, ga_ref, pb_ref, gate_ref, gpost_ref, wbra_ref, wout_ref, y_ref):
    ya = _dot((o_ref[...] * sa_ref[...]).astype(BF16), wbra_ref[...])
    mix = ga_ref[...] * ya + pb_ref[...]
    o = _dot(mix.astype(BF16), wout_ref[...])
    ms = jnp.mean(o * o, axis=-1, keepdims=True)
    on = o * lax.rsqrt(ms + RMS_EPS) * gpost_ref[...]
    y_ref[...] = x_ref[...] + gate_ref[...] * on


def _finish(x2d, o_attn, sa, ga, pb, gate, g_post, w_br_a, w_out, tm, rows_per_gate):
    n = x2d.shape[0]
    row = lambda w: pl.BlockSpec((tm, w), lambda i: (i, 0))
    const = lambda shp: pl.BlockSpec(shp, lambda i: (0,) * len(shp))
    if rows_per_gate == 1:
        gate_spec = row(D_MODEL)
    else:
        tiles_per_gate = rows_per_gate // tm
        gate = gate[:, None, :]
        gate_spec = pl.BlockSpec((None, 1, D_MODEL), lambda i: (i // tiles_per_gate, 0, 0))
    return pl.pallas_call(
        _finish_kernel,
        grid=(n // tm,),
        in_specs=[row(D_MODEL), row(ATTN_WIDTH), row(ATTN_WIDTH), row(D_MODEL), row(D_MODEL), gate_spec,
                  const((1, D_MODEL)), const((ATTN_WIDTH, D_MODEL)), const((D_MODEL, D_MODEL))],
        out_specs=row(D_MODEL),
        out_shape=jax.ShapeDtypeStruct((n, D_MODEL), F32),
        compiler_params=pltpu.CompilerParams(dimension_semantics=("arbitrary",), vmem_limit_bytes=VMEM_LIMIT),
        name="finish",
    )(x2d, o_attn, sa, ga, pb, gate, g_post, w_br_a, w_out)


def _attn_decode_kernel(pt_ref, *refs, n_req, n_pages, page_size, wbuf):
    del pt_ref
    page_refs = refs[:n_req * n_pages]
    (win_ref, q_ref, kvnew_ref, gates_ref, pelo_ref, pehi_ref, w1lo_ref, w1hi_ref, w2_ref, ovl_ref, expand_ref,
     o_ref, winout_ref, kcmp_ref, vcmp_ref) = refs[n_req * n_pages:]
    past_len = n_pages * page_size
    n_chunk = past_len // CMP_STRIDE
    reqs = range(n_req)
    heads = lambda b: slice(b * N_HEADS, (b + 1) * N_HEADS)
    per_req = lambda fn: jnp.concatenate([fn(b) for b in reqs], axis=0)
    q8f = per_req(lambda b: q_ref[b])
    q8 = q8f.astype(BF16)
    new_row = lambda which: per_req(lambda b: jnp.broadcast_to(kvnew_ref[b, which:which + 1, :], (N_HEADS, LANES)))

    def stream(ref, which):
        slab = ref[which * KV_HEADS:(which + 1) * KV_HEADS]
        return slab.reshape(KV_WIDTH, slab.shape[-1])

    for i, pr in enumerate(page_refs):
        rows = slice(i * page_size, (i + 1) * page_size)
        kcmp_ref[rows, :] = stream(pr, 0).T
        vcmp_ref[rows, :] = stream(pr, 1).T

    kvc = []
    for kind, src_ref in enumerate((kcmp_ref, vcmp_ref)):
        hid = _compress_hidden(src_ref, n_req * n_chunk, kind, pelo_ref, pehi_ref, w1lo_ref, w1hi_ref)
        kvc.append(_dot(_silu(hid).astype(BF16), w2_ref[kind]).astype(BF16))
    kc, vc = kvc
    chunks = lambda b: slice(b * n_chunk, (b + 1) * n_chunk)

    c_end = lax.broadcasted_iota(jnp.int32, (1, n_chunk), 1) * CMP_STRIDE + (CMP_BLOCK - 1)
    e, den = _masked_softmax_parts(per_req(lambda b: _dot_nt(q8[heads(b)], kc[chunks(b)])), c_end <= past_len)
    p = e * (1.0 / den)
    pb = p.astype(BF16)
    o_cmp = per_req(lambda b: _dot(pb[heads(b)], vc[chunks(b)]))

    row = lax.broadcasted_iota(jnp.int32, p.shape, 0)
    in_g0 = (row & (N_HEADS - 1)) < HEADS_PER_GROUP

    def group_sums(b):
        pr_, g0_ = p[heads(b)], in_g0[heads(b)]
        g0 = jnp.sum(jnp.where(g0_, pr_, 0.0), axis=0, keepdims=True)
        g1 = jnp.sum(jnp.where(g0_, 0.0, pr_), axis=0, keepdims=True)
        return jnp.where(g0_, g0, g1)

    hi, lo = _split_bf16(per_req(group_sums))
    imp = _dot(hi, ovl_ref[...]) + _dot(lo, ovl_ref[...])
    cur = past_len // SEL_BLOCK
    n_blk = -(-(cur + 1) // SUBLANES) * SUBLANES
    imp_t = imp.T[0:n_blk]
    j = lax.broadcasted_iota(jnp.int32, imp_t.shape, 0)
    forced = (j == 0) | (j == cur) | (j == cur - 1)
    bias_t = _top_k_bias_t(imp_t, forced, j <= cur)
    bias_t = jnp.concatenate([bias_t, jnp.full((LANES - n_blk, bias_t.shape[1]), NEG_BIG, F32)], axis=0)
    bias_keys = _dot(bias_t.T.astype(BF16), expand_ref[...])

    pages = lambda b: page_refs[b * n_pages:(b + 1) * n_pages]
    s = per_req(lambda b: jnp.concatenate([_dot(q8[heads(b)], stream(pr, 2).astype(BF16)) for pr in pages(b)],
                                          axis=1)) + bias_keys
    s_new = jnp.sum(q8f * new_row(2), axis=1, keepdims=True)
    m = jnp.maximum(jnp.max(s, axis=1, keepdims=True), s_new)
    e = jnp.exp(s - m)
    e_new = jnp.exp(s_new - m)
    den = jnp.sum(e, axis=1, keepdims=True) + e_new
    eb = e.astype(BF16)

    def sel_values(b):
        acc = None
        for i, pr in enumerate(pages(b)):
            part = _dot_nt(eb[heads(b), i * page_size:(i + 1) * page_size], stream(pr, 3).astype(BF16))
            acc = part if acc is None else acc + part
        return acc

    o_sel = (per_req(sel_values) + e_new * new_row(3)) * (1.0 / den)

    kwpos = past_len - wbuf + lax.broadcasted_iota(jnp.int32, (1, wbuf), 1)
    dt = past_len - kwpos
    valid = (dt >= 0) & (dt < WINDOW) & (kwpos >= 0)
    kw_t = [stream(win_ref.at[b], 0) for b in reqs]
    vw_t = [stream(win_ref.at[b], 1) for b in reqs]
    s = jnp.where(valid, per_req(lambda b: _dot(q8[heads(b)], kw_t[b].astype(BF16))), -jnp.inf)
    s_new = jnp.sum(q8f * new_row(4), axis=1, keepdims=True)
    m = jnp.maximum(jnp.max(s, axis=1, keepdims=True), s_new)
    e = jnp.exp(s - m)
    e_new = jnp.exp(s_new - m)
    den = jnp.sum(e, axis=1, keepdims=True) + e_new
    eb = e.astype(BF16)
    o_win = (per_req(lambda b: _dot_nt(eb[heads(b)], vw_t[b].astype(BF16))) + e_new * new_row(5)) * (1.0 / den)

    gates = per_req(lambda b: gates_ref[b])
    o = gates[:, 0:1] * o_cmp + gates[:, 1:2] * o_sel + gates[:, 2:3] * o_win
    o = jnp.where(in_g0, o, pltpu.roll(o, HEAD_DIM, 1))
    lane = lax.broadcasted_iota(jnp.int32, (KV_WIDTH, wbuf), 1)
    for b in reqs:
        o_ref[b] = o[heads(b)]
        new_cols = jnp.concatenate([kvnew_ref[b], jnp.zeros((SUBLANES - 6, LANES), F32)], axis=0).T
        for kind, old in enumerate((kw_t[b], vw_t[b])):
            shifted = jnp.where(lane == wbuf - 1, new_cols[:, 4 + kind:5 + kind], pltpu.roll(old, wbuf - 1, 1))
            winout_ref[b, kind * KV_HEADS:(kind + 1) * KV_HEADS] = shifted.reshape(KV_HEADS, HEAD_DIM, wbuf)


DECODE_REQS_PER_STEP = 4


def _attn_decode(cache_t, page_table, win_t, q8, kvnew, gates8, cw, ovl, expand):
    page_size = cache_t.shape[-1]
    batch, n_pages = page_table.shape
    wbuf = win_t.shape[-1]
    past_len = n_pages * page_size
    n_req = DECODE_REQS_PER_STEP if batch % DECODE_REQS_PER_STEP == 0 else 1

    def page_spec(r, k):
        return pl.BlockSpec((None, PAGE_ROWS, HEAD_DIM, page_size), lambda i, pt: (pt[i * n_req + r, k], 0, 0, 0))

    per_step = lambda a: pl.BlockSpec((n_req,) + a.shape[1:], lambda i, pt: (i,) + (0,) * (a.ndim - 1))
    const = lambda a: pl.BlockSpec(a.shape, lambda i, pt: (0,) * a.ndim, pipeline_mode=pl.Buffered(1))
    consts = [cw["pe_lo"], cw["pe_hi"], cw["w1_lo"], cw["w1_hi"], cw["w2"], ovl, expand]
    grid_spec = pltpu.PrefetchScalarGridSpec(
        num_scalar_prefetch=1,
        grid=(batch // n_req,),
        in_specs=[page_spec(r, k) for r in range(n_req) for k in range(n_pages)]
        + [per_step(win_t), per_step(q8), per_step(kvnew), per_step(gates8)]
        + [const(a) for a in consts],
        out_specs=(pl.BlockSpec((n_req, N_HEADS, LANES), lambda i, pt: (i, 0, 0)), per_step(win_t)),
        scratch_shapes=[pltpu.VMEM((n_req * past_len, LANES), F32), pltpu.VMEM((n_req * past_len, LANES), F32)],
    )
    o8, win_out = pl.pallas_call(
        functools.partial(_attn_decode_kernel, n_req=n_req, n_pages=n_pages, page_size=page_size, wbuf=wbuf),
        grid_spec=grid_spec,
        out_shape=(jax.ShapeDtypeStruct((batch, N_HEADS, LANES), F32),
                   jax.ShapeDtypeStruct(win_t.shape, F32)),
        compiler_params=pltpu.CompilerParams(dimension_semantics=("arbitrary",), vmem_limit_bytes=VMEM_LIMIT),
        name="attn_decode",
    )(page_table, *([cache_t] * (n_req * n_pages)), win_t, q8, kvnew, gates8, *consts)
    return o8, win_out


def _overlap_matrix(n_cmp_pad, n_cmp, n_selb):
    cs = jnp.arange(n_cmp_pad) * CMP_STRIDE
    ss = jnp.arange(LANES) * SEL_BLOCK
    ov = (cs[:, None] < ss[None, :] + SEL_BLOCK) & (cs[:, None] + CMP_BLOCK > ss[None, :])
    ov = ov & (jnp.arange(n_cmp_pad) < n_cmp)[:, None] & (jnp.arange(LANES) < n_selb)[None, :]
    return ov.astype(BF16)


def _prompt_layer(x, mod, wts):
    batch, seq, _ = x.shape
    x2d = x.reshape(batch * seq, D_MODEL)
    shift, scale, gate = mod[:, 0:D_MODEL], mod[:, D_MODEL:2 * D_MODEL], mod[:, 2 * D_MODEL:]
    tabs = _rope_cos_sin(jnp.arange(seq))
    tm = min(512, seq)
    (qt, kvt, kvc, kwin, ksel, kwinb, vselt, vwint, gatest, sa, ga, pb, utail) = _proj_prompt(
        x2d, shift, scale, wts["g_pre"], wts["w_in"], tabs, wts["conv_w"], wts["w_br_b"], batch, seq, tm)
    kc, vct = _compress_prompt(kvc, wts["cmp"], batch, seq)
    n_chunk = seq // CMP_STRIDE
    ovlt = _overlap_matrix(n_chunk, n_chunk - 1, -(-seq // SEL_BLOCK)).T
    o_attn = _attn_prompt_t(qt, ksel, kwinb, vselt, vwint, kc, vct, gatest, ovlt, batch, seq)
    y = _finish(x2d, o_attn, sa, ga, pb, gate, wts["g_post"], wts["w_br_a"], wts["w_out"], tm, seq)
    return (y.reshape(batch, seq, D_MODEL),
            jnp.transpose(kvt.reshape(batch, 4, KV_HEADS, HEAD_DIM, seq), (0, 4, 1, 2, 3)),
            jnp.transpose(kwin.reshape(batch, 2, KV_HEADS, HEAD_DIM, kwin.shape[-1]), (0, 4, 1, 2, 3)),
            utail[:, SUBLANES - (CONV_K - 1):])


def _sample_layer(x, mod, cache, page_table, win, conv_state, wts):
    batch, dec_seq, _ = x.shape
    assert dec_seq == 1
    n_pages, page_size, wbuf = page_table.shape[1], cache.shape[1], win.shape[1]
    past_len = n_pages * page_size
    assert past_len % SEL_BLOCK == 0 and wbuf == WINDOW and past_len // SEL_BLOCK < LANES
    x2d = x.reshape(batch, D_MODEL)
    shift, scale, gate = mod[:, 0:D_MODEL], mod[:, D_MODEL:2 * D_MODEL], mod[:, 2 * D_MODEL:]
    tabs = _rope_cos_sin(jnp.full((1,), past_len, jnp.int32))
    cbuf = conv_state.reshape(batch, (CONV_K - 1) * CONV_WIDTH)
    qpad, kvnew, gates, sa, ga, pb, u = _proj_sample(
        x2d, shift, scale, wts["g_pre"], wts["w_in"], tabs, wts["conv_w"], wts["w_br_b"], cbuf)
    n_gate = N_HEADS * 3
    gates8 = jnp.pad(gates[:, :n_gate].reshape(batch, N_HEADS, 3), ((0, 0), (0, 0), (0, LANES - 3)))
    n_chunk = past_len // CMP_STRIDE
    ovl = _overlap_matrix(n_chunk, n_chunk - 1, past_len // SEL_BLOCK + 1)
    expand = (jnp.arange(LANES)[:, None] == (jnp.arange(past_len) // SEL_BLOCK)[None, :]).astype(BF16)
    cache_t = jnp.transpose(cache, (0, 2, 3, 4, 1)).reshape(cache.shape[0], PAGE_ROWS, HEAD_DIM, page_size)
    win_t = jnp.transpose(win, (0, 2, 3, 4, 1)).reshape(batch, WIN_ROWS, HEAD_DIM, wbuf)
    o8, win_out = _attn_decode(cache_t, page_table, win_t, qpad.reshape(batch, N_HEADS, LANES).astype(F32),
                               kvnew.reshape(batch, 6, LANES), gates8, wts["cmp"], ovl, expand)
    win_out = jnp.transpose(win_out.reshape(batch, 2, KV_HEADS, HEAD_DIM, wbuf), (0, 4, 1, 2, 3))
    y = _finish(x2d, o8[:, :, :HEAD_DIM].reshape(batch, ATTN_WIDTH), sa, ga, pb, gate, wts["g_post"], wts["w_br_a"],
                wts["w_out"], batch, 1)
    return (y.reshape(batch, 1, D_MODEL),
            kvnew[:, :4 * KV_WIDTH].reshape(batch, 1, 4, KV_HEADS, HEAD_DIM),
            win_out,
            jnp.stack([conv_state[:, CONV_K - 2], u], axis=1))


def _prep_weights(w_ada, b_ada, g_pre, g_post, w_in, pe_cmp, w_cmp1, w_cmp2, conv_w, w_br_a, w_br_b, w_out):
    n_unpadded_gate = HEADS_PER_GROUP * KV_HEADS * 3
    w_t = w_in.T
    w_pad = jnp.concatenate(
        [w_t[:C_G + n_unpadded_gate], jnp.zeros((LANES - n_unpadded_gate, D_MODEL), w_in.dtype),
         w_t[C_G + n_unpadded_gate:]], axis=0).astype(BF16)
    half = CMP_STRIDE * HEAD_DIM
    return dict(
        w_ada=w_ada, b_ada=b_ada, g_pre=g_pre.reshape(1, -1), g_post=g_post.reshape(1, -1), w_in=w_pad,
        cmp=_compress_weights(pe_cmp, w_cmp1, w_cmp2), conv_w=conv_w,
        w_br_a=w_br_a.astype(BF16), w_br_b=w_br_b.astype(BF16), w_out=w_out.astype(BF16))


def kernel(x_prompt, x_sample, cache_kv_pages, state_win_kv, state_conv, page_table, c_prompt, c_sample, w_ada, b_ada, g_pre, g_post, w_in, pe_cmp, w_cmp1, w_cmp2, conv_w, w_br_a, w_br_b, w_out):
    depth = w_in.shape[0]
    assert depth == 1
    wts = _prep_weights(w_ada[0], b_ada[0], g_pre[0], g_post[0], w_in[0], pe_cmp[0], w_cmp1[0], w_cmp2[0],
                        conv_w[0], w_br_a[0], w_br_b[0], w_out[0])
    n_prompt = c_prompt.shape[0]
    mod = _ada(jnp.concatenate([c_prompt, c_sample], axis=0), wts["w_ada"], wts["b_ada"])
    yp, kvp, wp, cp = _prompt_layer(x_prompt, mod[:n_prompt], wts)
    ys, kvs, ws, cs = _sample_layer(x_sample, mod[n_prompt:], cache_kv_pages[0], page_table, state_win_kv[0],
                                    state_conv[0], wts)
    return (yp, ys, kvp[None], wp[None], cp[None], kvs[None], ws[None], cs[None])
```
